```python
import math
import jax, jax.numpy as jnp
from jax import lax
import numpy as np

D_MODEL = 2048
BATCH = 2
SEQ = 4096
DEPTH = 2
DEC_BATCH = 4
DEC_SEQ = 2048
PAST_LEN = 128

N_ATT_HEADS = 12
ATT_HEAD_DIM = 64
ATT_WIDTH = N_ATT_HEADS * ATT_HEAD_DIM
DILATED_PATTERNS = ((128, 1), (512, 4), (2048, 16))
REL_BUCKETS = 32
REL_MAX_DIST = 1024
CONV_WIDTH = 512
CONV_K = 3
N_RET_HEADS = 6
RET_HEAD_DIM = 128
RET_WIDTH = N_RET_HEADS * RET_HEAD_DIM
RET_CHUNK = 128
ROPE_BASE = 10000.0
MIX_WIDTH = ATT_WIDTH + CONV_WIDTH + RET_WIDTH
IN_COLS = 3 * ATT_WIDTH + 3 * CONV_WIDTH + 4 * RET_WIDTH
PEER_HEADS = 8
PEER_NKEYS = 128
PEER_EXPERTS = PEER_NKEYS * PEER_NKEYS
PEER_DKEY = 256
PEER_DHALF = PEER_DKEY // 2
PEER_TOPK = 16
PEER_TOKEN_BLOCK = 128
RMS_EPS = 1e-6
GN_EPS = 1e-6
NEG_INF = -1e30

kernel_name = "hymba_longnet_conv_retnet_peer_encoder"


def rmsnorm(x, g):
    xf = x.astype(jnp.float32)
    y = xf * lax.rsqrt(jnp.mean(xf * xf, axis=-1, keepdims=True) + RMS_EPS)
    return (y * g.astype(jnp.float32)).astype(x.dtype)


def t5_bucket(rel):
    nb = REL_BUCKETS // 2
    ret = (rel > 0).astype(np.int32) * nb
    n = np.abs(rel)
    max_exact = nb // 2
    nf = np.maximum(n, 1).astype(np.float32)
    large = max_exact + (np.log(nf / max_exact) / math.log(REL_MAX_DIST / max_exact)
                         * (nb - max_exact)).astype(np.int32)
    large = np.minimum(large, nb - 1)
    return ret + np.where(n < max_exact, n, large)


def dilated_window_attention(q, k, v, rel_bias, window, dil):
    B, S, H, dh = q.shape
    W = window // (2 * dil)
    L = S // dil
    nb = -(-L // W)
    Lp = nb * W
    N = B * dil

    def to_sub(t):
        t = t.reshape(B, L, dil, H, dh).transpose(0, 2, 3, 1, 4).reshape(N, H, L, dh)
        return jnp.pad(t, ((0, 0), (0, 0), (0, Lp - L), (0, 0)))

    def windows(t):
        tb = jnp.pad(t, ((0, 0), (0, 0), (W, W), (0, 0))).reshape(N, H, nb + 2, W, dh)
        return jnp.concatenate([tb[:, :, :-2], tb[:, :, 1:-1], tb[:, :, 2:]], axis=3)

    qb = to_sub(q).reshape(N, H, nb, W, dh)
    kw = windows(to_sub(k))
    vw = windows(to_sub(v))

    r = np.arange(W)[:, None]
    c = np.arange(3 * W)[None, :]
    rel = c - W - r
    bias = jnp.transpose(rel_bias[t5_bucket(dil * rel)], (2, 0, 1)).astype(jnp.float32)
    blk = np.arange(nb)[:, None, None]
    kpos = (blk - 1) * W + c[None]
    valid = (np.abs(rel)[None] <= W) & (kpos >= 0) & (kpos < L)

    s = jnp.einsum('nhbqd,nhbkd->nhbqk', qb, kw, preferred_element_type=jnp.float32) * (dh ** -0.5)
    s = jnp.where(valid[None, None], s + bias[None, :, None], NEG_INF)
    lse = jax.nn.logsumexp(s, axis=-1)
    p = jnp.exp(s - lse[..., None]).astype(v.dtype)
    o = jnp.einsum('nhbqk,nhbkd->nhbqd', p, vw)
    o = o.reshape(N, H, Lp, dh)[:, :, :L].reshape(B, dil, H, L, dh).transpose(0, 3, 1, 2, 4).reshape(B, S, H, dh)
    lse = lse.reshape(N, H, Lp)[:, :, :L].reshape(B, dil, H, L).transpose(0, 3, 1, 2).reshape(B, S, H)
    return o, lse


def attention_mixer(q, k, v, rel_bias):
    B, S, _ = q.shape
    qh = q.reshape(B, S, N_ATT_HEADS, ATT_HEAD_DIM)
    kh = k.reshape(B, S, N_ATT_HEADS, ATT_HEAD_DIM)
    vh = v.reshape(B, S, N_ATT_HEADS, ATT_HEAD_DIM)
    outs, lses = [], []
    for window, dil in DILATED_PATTERNS:
        o, lse = dilated_window_attention(qh, kh, vh, rel_bias, window, dil)
        outs.append(o)
        lses.append(lse)
    w = jax.nn.softmax(jnp.stack(lses, axis=0), axis=0)
    o = jnp.einsum('pbsh,pbshd->bshd', w, jnp.stack(outs, axis=0).astype(jnp.float32))
    return o.reshape(B, S, ATT_WIDTH).astype(q.dtype)


def short_conv_mixer(bg, cg, h, conv_w):
    S = h.shape[1]
    z = cg * h
    zp = jnp.pad(z, ((0, 0), (1, 1), (0, 0)))
    y = conv_w[0] * zp[:, :S] + conv_w[1] * zp[:, 1:S + 1] + conv_w[2] * zp[:, 2:]
    return bg * y


def rotary(t):
    S, d = t.shape[2], t.shape[3]
    inv = ROPE_BASE ** (-jnp.arange(0, d, 2, dtype=jnp.float32) / d)
    ang = jnp.arange(S, dtype=jnp.float32)[:, None] * inv[None, :]
    cos, sin = jnp.cos(ang), jnp.sin(ang)
    t1, t2 = t[..., : d // 2], t[..., d // 2:]
    return jnp.concatenate([t1 * cos - t2 * sin, t1 * sin + t2 * cos], axis=-1)


def retention_direction(q, k, v, log_g, strict):
    B, H, S, dk = q.shape
    dv = v.shape[-1]
    C = RET_CHUNK
    n = S // C
    qc = q.reshape(B, H, n, C, dk)
    kc = k.reshape(B, H, n, C, dk)
    vc = v.reshape(B, H, n, C, dv)
    idx = jnp.arange(C, dtype=jnp.float32)
    diff = idx[:, None] - idx[None, :]
    allowed = (diff > 0) if strict else (diff >= 0)
    decay_in = jnp.where(allowed[None], jnp.exp(log_g[:, None, None] * jnp.maximum(diff, 0.0)[None]), 0.0)
    inner = jnp.einsum('bhncd,bhnmd->bhncm', qc, kc) * decay_in[None, :, None]
    o_in = jnp.einsum('bhncm,bhnme->bhnce', inner, vc)
    k_w = jnp.exp(log_g[:, None] * (C - 1 - idx)[None])
    q_w = jnp.exp(log_g[:, None] * (idx + 1)[None])
    g_chunk = jnp.exp(log_g * C)
    kv = jnp.einsum('bhncd,hc,bhnce->nbhde', kc, k_w, vc)

    def step(state, kv_i):
        return g_chunk[None, :, None, None] * state + kv_i, state

    _, prev = lax.scan(step, jnp.zeros((B, H, dk, dv), jnp.float32), kv)
    o_x = jnp.einsum('bhncd,nbhde->bhnce', qc, prev) * q_w[None, :, None, :, None]
    return (o_in + o_x).reshape(B, H, S, dv)


def retention_mixer(q, k, v, g, decay_logit):
    B, S, _ = q.shape

    def heads(t):
        return t.astype(jnp.float32).reshape(B, S, N_RET_HEADS, RET_HEAD_DIM).transpose(0, 2, 1, 3)

    qh = rotary(heads(q))
    kh = rotary(heads(k)) * (RET_HEAD_DIM ** -0.5)
    vh = heads(v)
    log_g = jax.nn.log_sigmoid(decay_logit.astype(jnp.float32))
    fwd = retention_direction(qh, kh, vh, log_g[0], strict=False)
    bwd = retention_direction(jnp.flip(qh, 2), jnp.flip(kh, 2), jnp.flip(vh, 2), log_g[1], strict=True)
    o = fwd + jnp.flip(bwd, 2)
    mu = jnp.mean(o, axis=-1, keepdims=True)
    var = jnp.mean(jnp.square(o - mu), axis=-1, keepdims=True)
    o = ((o - mu) * lax.rsqrt(var + GN_EPS)).transpose(0, 2, 1, 3).reshape(B, S, RET_WIDTH)
    return (jax.nn.silu(g.astype(jnp.float32)) * o).astype(q.dtype)


def peer_ffn(x, w_q, sub_keys, u_tab, v_tab):
    B, S, D = x.shape
    T = PEER_TOKEN_BLOCK
    K = PEER_TOPK

    def block(xt):
        q = (xt @ w_q).reshape(T, PEER_HEADS, 2, PEER_DHALF)
        s = jnp.einsum('thpc,hpnc->thpn', q, sub_keys, preferred_element_type=jnp.float32)
        sv, si = lax.top_k(s, K)
        cand = sv[:, :, 0, :, None] + sv[:, :, 1, None, :]
        cv, ci = lax.top_k(cand.reshape(T, PEER_HEADS, K * K), K)
        i1 = jnp.take_along_axis(si[:, :, 0], ci // K, axis=-1)
        i2 = jnp.take_along_axis(si[:, :, 1], ci % K, axis=-1)
        eidx = i1 * PEER_NKEYS + i2
        gate = jax.nn.softmax(cv, axis=-1).astype(xt.dtype)
        u = u_tab[eidx]
        act = jax.nn.gelu(jnp.einsum('td,thkd->thk', xt, u))
        vv = v_tab[eidx]
        return jnp.einsum('thk,thkd->td', gate * act, vv)

    out = lax.map(block, x.reshape(B * S // T, T, D))
    return out.reshape(B, S, D)


def trunk(x, rel_bias, final_norm, norm_mix, norm_ffn, w_in, conv_w, ret_decay, w_out,
          peer_wq, peer_keys, peer_u, peer_v):
    B, S, _ = x.shape
    sizes = [ATT_WIDTH] * 3 + [CONV_WIDTH] * 3 + [RET_WIDTH] * 4
    offsets = [int(o) for o in np.cumsum(sizes)[:-1]]
    for l in range(DEPTH):
        h = rmsnorm(x, norm_mix[l])
        proj = h @ w_in[l]
        aq, ak, av, cb, cc, ch, rq, rk, rv, rg = jnp.split(proj, offsets, axis=-1)
        att = attention_mixer(aq, ak, av, rel_bias)
        conv = short_conv_mixer(cb, cc, ch, conv_w[l])
        ret = retention_mixer(rq, rk, rv, rg, ret_decay[l])
        mixed = jnp.concatenate([att, conv.astype(x.dtype), ret], axis=-1)
        x = x + mixed @ w_out[l]
        x = x + peer_ffn(rmsnorm(x, norm_ffn[l]), peer_wq[l], peer_keys[l], peer_u[l], peer_v[l])
    return rmsnorm(x, final_norm)


def setup_inputs(seed: int = 0) -> dict:
    key = jax.random.key(seed)
    ks = jax.random.split(key, 14)
    f32 = jnp.float32
    decay_base = np.log(2.0 ** (5.0 + np.arange(N_RET_HEADS)) - 1.0).astype(np.float32)
    return {
        "x_prompt": jax.random.normal(ks[0], (BATCH, SEQ, D_MODEL), f32),
        "x_sample": jax.random.normal(ks[1], (DEC_BATCH, DEC_SEQ, D_MODEL), f32),
        "rel_bias": 0.5 * jax.random.normal(ks[2], (REL_BUCKETS, N_ATT_HEADS), f32),
        "final_norm": 1.0 + 0.01 * jax.random.normal(ks[3], (D_MODEL,), f32),
        "norm_mix": 1.0 + 0.01 * jax.random.normal(ks[4], (DEPTH, D_MODEL), f32),
        "norm_ffn": 1.0 + 0.01 * jax.random.normal(ks[5], (DEPTH, D_MODEL), f32),
        "w_in": jax.random.normal(ks[6], (DEPTH, D_MODEL, IN_COLS), f32) * D_MODEL ** -0.5,
        "conv_w": 0.5 * jax.random.normal(ks[7], (DEPTH, CONV_K, CONV_WIDTH), f32),
        "ret_decay": jnp.asarray(decay_base)[None, None, :] + 0.1 * jax.random.normal(ks[8], (DEPTH, 2, N_RET_HEADS), f32),
        "w_out": jax.random.normal(ks[9], (DEPTH, MIX_WIDTH, D_MODEL), f32) * MIX_WIDTH ** -0.5,
        "peer_wq": jax.random.normal(ks[10], (DEPTH, D_MODEL, PEER_HEADS * PEER_DKEY), f32) * D_MODEL ** -0.5,
        "peer_keys": jax.random.normal(ks[11], (DEPTH, PEER_HEADS, 2, PEER_NKEYS, PEER_DHALF), f32) * PEER_DHALF ** -0.5,
        "peer_u": jax.random.normal(ks[12], (DEPTH, PEER_EXPERTS, D_MODEL), f32) * D_MODEL ** -0.5,
        "peer_v": jax.random.normal(ks[13], (DEPTH, PEER_EXPERTS, D_MODEL), f32) * PEER_HEADS ** -0.5,
    }


def reference(x_prompt, x_sample, rel_bias, final_norm, norm_mix, norm_ffn, w_in, conv_w, ret_decay,
              w_out, peer_wq, peer_keys, peer_u, peer_v):
    y_prompt = trunk(x_prompt, rel_bias, final_norm, norm_mix, norm_ffn, w_in, conv_w, ret_decay,
                     w_out, peer_wq, peer_keys, peer_u, peer_v)
    y_sample = trunk(x_sample, rel_bias, final_norm, norm_mix, norm_ffn, w_in, conv_w, ret_decay,
                     w_out, peer_wq, peer_keys, peer_u, peer_v)
    return (y_prompt, y_sample)
```

```python
import functools
import math

import jax
import jax.numpy as jnp
import numpy as np
from jax import lax
from jax.experimental import pallas as pl
from jax.experimental.pallas import tpu as pltpu

F32 = jnp.float32
BF16 = jnp.bfloat16

D_MODEL = 2048
DEPTH = 2
N_ATT_HEADS = 12
ATT_HEAD_DIM = 64
ATT_WIDTH = N_ATT_HEADS * ATT_HEAD_DIM
DILATED_PATTERNS = ((128, 1), (512, 4), (2048, 16))
REL_BUCKETS = 32
REL_MAX_DIST = 1024
CONV_WIDTH = 512
N_RET_HEADS = 6
RET_HEAD_DIM = 128
RET_WIDTH = N_RET_HEADS * RET_HEAD_DIM
RET_CHUNK = 128
ROPE_BASE = 10000.0
MIX_WIDTH = ATT_WIDTH + CONV_WIDTH + RET_WIDTH
IN_COLS = 3 * ATT_WIDTH + 3 * CONV_WIDTH + 4 * RET_WIDTH
PEER_HEADS = 8
PEER_NKEYS = 128
PEER_EXPERTS = PEER_NKEYS * PEER_NKEYS
PEER_DHALF = 128
PEER_TOPK = 16
PEER_SEL = PEER_HEADS * PEER_TOPK
RMS_EPS = 1e-6
GN_EPS = 1e-6
NEG_INF = -1e30

LANES = 128
SUBLANES = 8
VMEM_LIMIT_BYTES = 56 * 1024 * 1024

ATT_W = 64
ATT_TQ = 128
ATT_TK = ATT_TQ + 2 * ATT_W

COL_AQ, COL_AK, COL_AV = 0, ATT_WIDTH // LANES, 2 * ATT_WIDTH // LANES
COL_CB = 3 * ATT_WIDTH // 256
COL_CC = COL_CB + CONV_WIDTH // 256
COL_CH = COL_CC + CONV_WIDTH // 256
COL_RQ = (3 * ATT_WIDTH + 3 * CONV_WIDTH) // LANES
COL_RK = COL_RQ + N_RET_HEADS
COL_RV = COL_RK + N_RET_HEADS
COL_RG = COL_RV + N_RET_HEADS


def _cparams(*sem):
    return pltpu.CompilerParams(dimension_semantics=sem, vmem_limit_bytes=VMEM_LIMIT_BYTES)


def _rms(x, g):
    ms = jnp.mean(x * x, axis=-1, keepdims=True)
    return x * lax.rsqrt(ms + RMS_EPS) * g


def _inproj_kernel(x_ref, g_ref, w_ref, o_ref):
    h = _rms(x_ref[...], g_ref[...]).astype(BF16)
    o_ref[...] = jnp.dot(h, w_ref[...], preferred_element_type=F32)


def _inproj(x, g, w, *, tm=512, tn=2304):
    t, d = x.shape
    n = w.shape[1]
    return pl.pallas_call(
        _inproj_kernel,
        grid=(n // tn, t // tm),
        in_specs=[pl.BlockSpec((tm, d), lambda j, i: (i, 0)),
                  pl.BlockSpec((1, d), lambda j, i: (0, 0)),
                  pl.BlockSpec((d, tn), lambda j, i: (0, j))],
        out_specs=pl.BlockSpec((tm, tn), lambda j, i: (i, j)),
        out_shape=jax.ShapeDtypeStruct((t, n), F32),
        compiler_params=_cparams("arbitrary", "arbitrary"),
        name="inproj",
    )(x, g.reshape(1, d), w)


def _attn_kernel(q_ref, k_ref, v_ref, band_ref, o_ref, kbuf, vbuf, acc_ref, m_ref, l_ref):
    s_len = q_ref.shape[1]
    lane = lax.broadcasted_iota(jnp.int32, (1, LANES), 1)
    head0 = lane < ATT_HEAD_DIM
    kcol = lax.broadcasted_iota(jnp.int32, (1, ATT_TK), 1)
    zpad = jnp.zeros((ATT_W, LANES), BF16)

    for p, (_, dil) in enumerate(DILATED_PATTERNS):
        sub_len = s_len // dil
        n_chunk = sub_len // ATT_TQ

        def rows(start, size):
            if dil == 1:
                return pl.ds(start, size)
            return pl.ds(start, size, stride=dil)

        def residue_body(r, carry):
            kbuf[pl.ds(0, ATT_W), :] = zpad
            vbuf[pl.ds(0, ATT_W), :] = zpad
            kbuf[pl.ds(ATT_W + sub_len, ATT_W), :] = zpad
            vbuf[pl.ds(ATT_W + sub_len, ATT_W), :] = zpad

            def stage_body(c, carry2):
                src = rows(r + c * ATT_TQ * dil, ATT_TQ)
                dst = pl.ds(pl.multiple_of(ATT_W + c * ATT_TQ, ATT_W), ATT_TQ)
                kbuf[dst, :] = k_ref[0, src, :].astype(BF16)
                vbuf[dst, :] = v_ref[0, src, :].astype(BF16)
                return carry2

            lax.fori_loop(0, n_chunk, stage_body, 0)

            def chunk_body(c, carry2):
                qrows = rows(r + c * ATT_TQ * dil, ATT_TQ)
                q = q_ref[0, qrows, :] * (ATT_HEAD_DIM ** -0.5)
                win = pl.ds(pl.multiple_of(c * ATT_TQ, ATT_TQ), ATT_TK)
                kw = kbuf[win, :]
                vw = vbuf[win, :]
                kpos = c * ATT_TQ - ATT_W + kcol
                pos_ok = (kpos >= 0) & (kpos < sub_len)
                pv, mm, ll = [], [], []
                for h in range(2):
                    sel = head0 if h == 0 else jnp.logical_not(head0)
                    qh = jnp.where(sel, q, 0.0).astype(BF16)
                    s = lax.dot_general(qh, kw, (((1,), (1,)), ((), ())),
                                        preferred_element_type=F32)
                    s = jnp.where(pos_ok, s + band_ref[p, h], NEG_INF)
                    m = jnp.max(s, axis=1, keepdims=True)
                    e = jnp.exp(s - m)
                    ll.append(jnp.sum(e, axis=1, keepdims=True))
                    mm.append(m)
                    pv.append(jnp.dot(e.astype(BF16), vw, preferred_element_type=F32))
                acc_n = jnp.where(head0, pv[0], pv[1])
                m_n = jnp.where(head0, mm[0], mm[1])
                l_n = jnp.where(head0, ll[0], ll[1])
                if p == 0:
                    acc_ref[qrows, :] = acc_n
                    m_ref[qrows, :] = m_n
                    l_ref[qrows, :] = l_n
                else:
                    m_o = m_ref[qrows, :]
                    m_t = jnp.maximum(m_o, m_n)
                    a_o = jnp.exp(m_o - m_t)
                    a_n = jnp.exp(m_n - m_t)
                    acc_ref[qrows, :] = acc_ref[qrows, :] * a_o + acc_n * a_n
                    l_ref[qrows, :] = l_ref[qrows, :] * a_o + l_n * a_n
                    m_ref[qrows, :] = m_t
                return carry2

            lax.fori_loop(0, n_chunk, chunk_body, 0)
            return carry

        lax.fori_loop(0, dil, residue_body, 0)

    def out_body(c, carry):
        rws = pl.ds(pl.multiple_of(c * ATT_TQ, ATT_TQ), ATT_TQ)
        o_ref[0, rws, :] = (acc_ref[rws, :] / l_ref[rws, :]).astype(o_ref.dtype)
        return carry

    lax.fori_loop(0, s_len // ATT_TQ, out_body, 0)


def _attention(proj, band):
    b, s, _ = proj.shape
    n_pair = N_ATT_HEADS // 2
    blk = lambda off: pl.BlockSpec((1, s, LANES), lambda bi, hp: (bi, 0, off + hp))
    return pl.pallas_call(
        _attn_kernel,
        grid=(b, n_pair),
        in_specs=[blk(COL_AQ), blk(COL_AK), blk(COL_AV),
                  pl.BlockSpec((len(DILATED_PATTERNS), 2, ATT_TQ, ATT_TK),
                               lambda bi, hp: (0, hp, 0, 0))],
        out_specs=pl.BlockSpec((1, s, LANES), lambda bi, hp: (bi, 0, hp)),
        out_shape=jax.ShapeDtypeStruct((b, s, ATT_WIDTH), BF16),
        scratch_shapes=[pltpu.VMEM((s + 2 * ATT_W, LANES), BF16),
                        pltpu.VMEM((s + 2 * ATT_W, LANES), BF16),
                        pltpu.VMEM((s, LANES), F32),
                        pltpu.VMEM((s, LANES), F32),
                        pltpu.VMEM((s, LANES), F32)],
        compiler_params=_cparams("arbitrary", "arbitrary"),
        name="dilated_attention",
    )(proj, proj, proj, band)


def _t5_bucket(rel):
    nb = REL_BUCKETS // 2
    ret = (rel > 0).astype(np.int32) * nb
    n = np.abs(rel)
    max_exact = nb // 2
    nf = np.maximum(n, 1).astype(np.float32)
    large = max_exact + (np.log(nf / max_exact) / math.log(REL_MAX_DIST / max_exact)
                         * (nb - max_exact)).astype(np.int32)
    large = np.minimum(large, nb - 1)
    return ret + np.where(n < max_exact, n, large)


def _attention_band(rel_bias):
    i = np.arange(ATT_TQ)[:, None]
    c = np.arange(ATT_TK)[None, :]
    rel = c - ATT_W - i
    in_band = np.abs(rel) <= ATT_W
    out = []
    for _, dil in DILATED_PATTERNS:
        bias = jnp.transpose(rel_bias[_t5_bucket(dil * rel)], (2, 0, 1)).astype(F32)
        out.append(jnp.where(in_band[None], bias, NEG_INF))
    return jnp.stack(out, axis=0)


CONV_ROWS = 512


def _conv_kernel(cb_ref, cc_ref, ch_ref, w_ref, o_ref, zbuf):
    s_len = cb_ref.shape[1]
    width = cb_ref.shape[2]
    zero = jnp.zeros((SUBLANES, width), F32)
    zbuf[pl.ds(0, SUBLANES), :] = zero
    zbuf[pl.ds(SUBLANES + s_len, SUBLANES), :] = zero
    n_blk = s_len // CONV_ROWS

    for c in range(n_blk):
        r0 = c * CONV_ROWS
        zbuf[pl.ds(SUBLANES + r0, CONV_ROWS), :] = (
            cc_ref[0, pl.ds(r0, CONV_ROWS), :] * ch_ref[0, pl.ds(r0, CONV_ROWS), :])
    w0 = w_ref[0:1, :]
    w1 = w_ref[1:2, :]
    w2 = w_ref[2:3, :]

    for c in range(n_blk):
        r0 = c * CONV_ROWS
        y = (w0 * zbuf[pl.ds(SUBLANES - 1 + r0, CONV_ROWS), :]
             + w1 * zbuf[pl.ds(SUBLANES + r0, CONV_ROWS), :]
             + w2 * zbuf[pl.ds(SUBLANES + 1 + r0, CONV_ROWS), :])
        o_ref[0, pl.ds(r0, CONV_ROWS), :] = (cb_ref[0, pl.ds(r0, CONV_ROWS), :] * y).astype(o_ref.dtype)


def _short_conv(proj, conv_w):
    b, s, _ = proj.shape
    cw = 256
    blk = lambda off: pl.BlockSpec((1, s, cw), lambda bi, j: (bi, 0, off + j))
    return pl.pallas_call(
        _conv_kernel,
        grid=(b, CONV_WIDTH // cw),
        in_specs=[blk(COL_CB), blk(COL_CC), blk(COL_CH),
                  pl.BlockSpec((3, cw), lambda bi, j: (0, j))],
        out_specs=pl.BlockSpec((1, s, cw), lambda bi, j: (bi, 0, j)),
        out_shape=jax.ShapeDtypeStruct((b, s, CONV_WIDTH), BF16),
        scratch_shapes=[pltpu.VMEM((s + 2 * SUBLANES, cw), F32)],
        compiler_params=_cparams("arbitrary", "arbitrary"),
        name="short_conv",
    )(proj, proj, proj, conv_w)


def _ret_kernel(lg_ref, q_ref, k_ref, v_ref, g_ref, cos_ref, sin_ref, o_ref, qs, ks, os):
    s_len = q_ref.shape[1]
    c_len = RET_CHUNK
    n_chunk = s_len // c_len
    h = pl.program_id(1)
    lg_f = lg_ref[0, h]
    lg_b = lg_ref[1, h]

    def rot_body(n, carry):
        rws = pl.ds(pl.multiple_of(n * c_len, c_len), c_len)
        cs = cos_ref[rws, :]
        sn = sin_ref[rws, :]
        q = q_ref[0, rws, :]
        k = k_ref[0, rws, :]
        qs[rws, :] = q * cs + pltpu.roll(q, RET_HEAD_DIM // 2, 1) * sn
        ks[rws, :] = (k * cs + pltpu.roll(k, RET_HEAD_DIM // 2, 1) * sn) * (RET_HEAD_DIM ** -0.5)
        return carry

    lax.fori_loop(0, n_chunk, rot_body, 0)

    ri = lax.broadcasted_iota(jnp.int32, (c_len, c_len), 0)
    ci = lax.broadcasted_iota(jnp.int32, (c_len, c_len), 1)
    diff = (ri - ci).astype(F32)
    col = lax.broadcasted_iota(jnp.int32, (c_len, 1), 0).astype(F32)
    one = jnp.ones((1, 1), F32)

    dec_f = jnp.where(diff >= 0, jnp.exp(lg_f * jnp.maximum(diff, 0.0)), 0.0)
    kw_f = jnp.exp(lg_f * (c_len - 1 - col))
    qw_f = jnp.exp(lg_f * (col + 1))
    gc_f = jnp.exp(lg_f * c_len * one)
    dec_b = jnp.where(diff < 0, jnp.exp(lg_b * jnp.maximum(-diff, 0.0)), 0.0)
    kw_b = jnp.exp(lg_b * col)
    qw_b = jnp.exp(lg_b * (c_len - col))
    gc_b = jnp.exp(lg_b * c_len * one)

    def direction(n, state, dec, kw, qw, gc):
        rws = pl.ds(pl.multiple_of(n * c_len, c_len), c_len)
        q = qs[rws, :]
        k = ks[rws, :]
        v = v_ref[0, rws, :].astype(BF16)
        qb = q.astype(BF16)
        sc = lax.dot_general(qb, k.astype(BF16), (((1,), (1,)), ((), ())),
                             preferred_element_type=F32)
        o_in = jnp.dot((sc * dec).astype(BF16), v, preferred_element_type=F32)
        o_x = jnp.dot(qb, state.astype(BF16), preferred_element_type=F32) * qw
        kv = jnp.dot(jnp.transpose(k * kw).astype(BF16), v, preferred_element_type=F32)
        return rws, o_in + o_x, gc * state + kv

    def fwd_body(n, state):
        rws, o, state = direction(n, state, dec_f, kw_f, qw_f, gc_f)
        os[rws, :] = o
        return state

    lax.fori_loop(0, n_chunk, fwd_body, jnp.zeros((RET_HEAD_DIM, RET_HEAD_DIM), F32))

    def bwd_body(i, state):
        n = n_chunk - 1 - i
        rws, o, state = direction(n, state, dec_b, kw_b, qw_b, gc_b)
        o = o + os[rws, :]
        mu = jnp.mean(o, axis=-1, keepdims=True)
        oc = o - mu
        var = jnp.mean(oc * oc, axis=-1, keepdims=True)
        on = oc * lax.rsqrt(var + GN_EPS)
        g = g_ref[0, rws, :]
        o_ref[0, rws, :] = (g * jax.nn.sigmoid(g) * on).astype(o_ref.dtype)
        return state

    lax.fori_loop(0, n_chunk, bwd_body, jnp.zeros((RET_HEAD_DIM, RET_HEAD_DIM), F32))


def _retention(proj, log_g, cos2, sin2):
    b, s, _ = proj.shape
    blk = lambda off: pl.BlockSpec((1, s, LANES), lambda bi, h: (bi, 0, off + h))
    tab = pl.BlockSpec((s, LANES), lambda bi, h: (0, 0))
    return pl.pallas_call(
        _ret_kernel,
        grid=(b, N_RET_HEADS),
        in_specs=[pl.BlockSpec(memory_space=pltpu.SMEM),
                  blk(COL_RQ), blk(COL_RK), blk(COL_RV), blk(COL_RG), tab, tab],
        out_specs=pl.BlockSpec((1, s, LANES), lambda bi, h: (bi, 0, h)),
        out_shape=jax.ShapeDtypeStruct((b, s, RET_WIDTH), BF16),
        scratch_shapes=[pltpu.VMEM((s, LANES), F32)] * 3,
        compiler_params=_cparams("arbitrary", "arbitrary"),
        name="retention",
    )(log_g, proj, proj, proj, proj, cos2, sin2)


def _rotary_tables(s):
    d = RET_HEAD_DIM
    inv = ROPE_BASE ** (-jnp.arange(0, d, 2, dtype=F32) / d)
    ang = jnp.arange(s, dtype=F32)[:, None] * inv[None, :]
    cos, sin = jnp.cos(ang), jnp.sin(ang)
    return jnp.concatenate([cos, cos], axis=-1), jnp.concatenate([-sin, sin], axis=-1)


def _outproj_kernel(att_ref, conv_ref, ret_ref, x_ref, w_ref, o_ref):
    acc = jnp.dot(att_ref[...], w_ref[pl.ds(0, ATT_WIDTH), :], preferred_element_type=F32)
    acc += jnp.dot(conv_ref[...], w_ref[pl.ds(ATT_WIDTH, CONV_WIDTH), :], preferred_element_type=F32)
    acc += jnp.dot(ret_ref[...], w_ref[pl.ds(ATT_WIDTH + CONV_WIDTH, RET_WIDTH), :],
                   preferred_element_type=F32)
    o_ref[...] = x_ref[...] + acc


def _outproj(att, conv, ret, x, w, *, tm=512):
    t, d = x.shape
    row = lambda width: pl.BlockSpec((tm, width), lambda i: (i, 0))
    return pl.pallas_call(
        _outproj_kernel,
        grid=(t // tm,),
        in_specs=[row(ATT_WIDTH), row(CONV_WIDTH), row(RET_WIDTH), row(d),
                  pl.BlockSpec((MIX_WIDTH, d), lambda i: (0, 0))],
        out_specs=row(d),
        out_shape=jax.ShapeDtypeStruct((t, d), F32),
        compiler_params=_cparams("arbitrary"),
        name="outproj",
    )(att, conv, ret, x, w)


def _top16(s, payload=None):
    n = s.shape[0]
    row = lax.broadcasted_iota(jnp.int32, s.shape, 0)
    vals, picks = [], []
    for _ in range(PEER_TOPK):
        m = jnp.max(s, axis=0, keepdims=True)
        idx = jnp.min(jnp.where(s == m, row, n), axis=0, keepdims=True)
        hit = row == idx
        vals.append(m)
        if payload is None:
            picks.append(idx)
        else:
            picks.append(jnp.max(jnp.where(hit, payload, -1), axis=0, keepdims=True))
        s = jnp.where(hit, -jnp.inf, s)
    return vals, picks


def _peer_topk_kernel(x_ref, g_ref, wq_ref, keys_ref, h_ref, eidx_ref, gate_ref, sv_s, si_s):
    h = _rms(x_ref[...], g_ref[...])
    h_ref[...] = h
    q = jnp.dot(h.astype(BF16), wq_ref[...], preferred_element_type=F32).astype(BF16)
    for grp in range(2 * PEER_HEADS):
        s = lax.dot_general(keys_ref[grp], q[:, grp * PEER_DHALF:(grp + 1) * PEER_DHALF],
                            (((1,), (1,)), ((), ())), preferred_element_type=F32)
        vals, idxs = _top16(s)
        for j in range(PEER_TOPK):
            sv_s[grp, pl.ds(j, 1), :] = vals[j]
            si_s[grp, pl.ds(j, 1), :] = idxs[j]
    for hd in range(PEER_HEADS):
        sv0, sv1 = sv_s[2 * hd], sv_s[2 * hd + 1]
        si0, si1 = si_s[2 * hd], si_s[2 * hd + 1]
        cand = jnp.concatenate([sv0[a:a + 1, :] + sv1 for a in range(PEER_TOPK)], axis=0)
        eid = jnp.concatenate([si0[a:a + 1, :] * PEER_NKEYS + si1 for a in range(PEER_TOPK)], axis=0)
        vals, picks = _top16(cand, eid)
        cv = jnp.concatenate(vals, axis=0)
        e = jnp.exp(cv - vals[0])
        gate_ref[pl.ds(hd * PEER_TOPK, PEER_TOPK), :] = e / jnp.sum(e, axis=0, keepdims=True)
        eidx_ref[pl.ds(hd * PEER_TOPK, PEER_TOPK), :] = jnp.concatenate(picks, axis=0)


def _peer_topk(x, g, wq, keys, *, tm=256):
    t, d = x.shape
    return pl.pallas_call(
        _peer_topk_kernel,
        grid=(t // tm,),
        in_specs=[pl.BlockSpec((tm, d), lambda i: (i, 0)),
                  pl.BlockSpec((1, d), lambda i: (0, 0)),
                  pl.BlockSpec((d, 2 * PEER_HEADS * PEER_DHALF), lambda i: (0, 0)),
                  pl.BlockSpec((2 * PEER_HEADS, PEER_NKEYS, PEER_DHALF), lambda i: (0, 0, 0))],
        out_specs=[pl.BlockSpec((tm, d), lambda i: (i, 0)),
                   pl.BlockSpec((PEER_SEL, tm), lambda i: (0, i)),
                   pl.BlockSpec((PEER_SEL, tm), lambda i: (0, i))],
        out_shape=[jax.ShapeDtypeStruct((t, d), F32),
                   jax.ShapeDtypeStruct((PEER_SEL, t), jnp.int32),
                   jax.ShapeDtypeStruct((PEER_SEL, t), F32)],
        scratch_shapes=[pltpu.VMEM((2 * PEER_HEADS, PEER_TOPK, tm), F32),
                        pltpu.VMEM((2 * PEER_HEADS, PEER_TOPK, tm), jnp.int32)],
        compiler_params=_cparams("arbitrary"),
        name="peer_topk",
    )(x, g.reshape(1, d), wq, keys)


PEER_TB = 128
N_SLOT = 2


def _gelu_tanh(x):
    return 0.5 * x * (1.0 + jnp.tanh(math.sqrt(2.0 / math.pi) * (x + 0.044715 * (x * x * x))))


def _peer_expert_kernel(idx_hbm, tab_hbm, h_ref, gate_ref, x_ref, o_ref, idx_s, gbuf, isem, gsem):
    i = pl.program_id(0)
    icp = pltpu.make_async_copy(idx_hbm.at[i], idx_s, isem)
    icp.start()
    icp.wait()

    def row_copy(e, slot, j):
        return pltpu.make_async_copy(tab_hbm.at[pl.ds(e, 1), :], gbuf.at[slot, pl.ds(j, 1), :],
                                     gsem.at[slot])

    def issue(t, slot):
        for j in range(PEER_SEL):
            row_copy(idx_s[t * PEER_SEL + j], slot, j).start()

    def wait(slot):
        pltpu.make_async_copy(tab_hbm.at[pl.ds(0, PEER_SEL), :], gbuf.at[slot], gsem.at[slot]).wait()

    lane_t = lax.broadcasted_iota(jnp.int32, (1, PEER_TB), 1)
    sub = lax.broadcasted_iota(jnp.int32, (SUBLANES, 1), 0)
    o_ref[...] = x_ref[...]

    def compute(t, slot):
        grp = pl.ds(pl.multiple_of((t // SUBLANES) * SUBLANES, SUBLANES), SUBLANES)
        mine = sub == (t % SUBLANES)
        a1 = jnp.zeros((PEER_SEL, LANES), F32)
        for c in range(D_MODEL // LANES):
            cols = pl.ds(c * LANES, LANES)
            xrow = jnp.sum(jnp.where(mine, h_ref[grp, cols], 0.0), axis=0, keepdims=True)
            u = pltpu.bitcast(gbuf[slot, :, cols] << 16, F32)
            a1 = a1 + u * xrow
        act = jnp.sum(a1, axis=1, keepdims=True)
        gate = jnp.sum(jnp.where(lane_t == t, gate_ref[...], 0.0), axis=1, keepdims=True)
        w = jnp.broadcast_to(gate * _gelu_tanh(act), (PEER_SEL, LANES))
        for c in range(D_MODEL // LANES):
            cols = pl.ds(c * LANES, LANES)
            v = pltpu.bitcast(gbuf[slot, :, cols] & jnp.uint32(0xFFFF0000), F32)
            mix = jnp.sum(w * v, axis=0, keepdims=True)
            o_ref[grp, cols] = o_ref[grp, cols] + jnp.where(mine, mix, 0.0)

    issue(0, 0)

    def pair_body(tp, carry):
        for slot in range(N_SLOT):
            t = tp * N_SLOT + slot

            @pl.when(t + 1 < PEER_TB)
            def _():
                issue(t + 1, (slot + 1) % N_SLOT)

            wait(slot)
            compute(t, slot)
        return carry

    lax.fori_loop(0, PEER_TB // N_SLOT, pair_body, 0)


def _peer_experts(idx, tab, h, gate_t, x):
    t, d = x.shape
    n_blk = t // PEER_TB
    row = pl.BlockSpec((PEER_TB, d), lambda i: (i, 0))
    return pl.pallas_call(
        _peer_expert_kernel,
        grid=(n_blk,),
        in_specs=[pl.BlockSpec(memory_space=pl.ANY),
                  pl.BlockSpec(memory_space=pl.ANY),
                  row,
                  pl.BlockSpec((PEER_SEL, PEER_TB), lambda i: (0, i)),
                  row],
        out_specs=row,
        out_shape=jax.ShapeDtypeStruct((t, d), F32),
        scratch_shapes=[pltpu.SMEM((PEER_TB * PEER_SEL,), jnp.int32),
                        pltpu.VMEM((N_SLOT, PEER_SEL, d), jnp.uint32),
                        pltpu.SemaphoreType.DMA,
                        pltpu.SemaphoreType.DMA((N_SLOT,))],
        compiler_params=_cparams("arbitrary"),
        name="peer_experts",
    )(idx, tab, h, gate_t, x)


def _pack_expert_table(u, v):
    ub = lax.bitcast_convert_type(u.astype(BF16), jnp.uint16).astype(jnp.uint32)
    vb = lax.bitcast_convert_type(v.astype(BF16), jnp.uint16).astype(jnp.uint32)
    return ub | (vb << 16)


def _final_norm_kernel(x_ref, g_ref, o_ref):
    o_ref[...] = _rms(x_ref[...], g_ref[...])


def _final_norm(x, g, *, tm=512):
    t, d = x.shape
    row = pl.BlockSpec((tm, d), lambda i: (i, 0))
    return pl.pallas_call(
        _final_norm_kernel,
        grid=(t // tm,),
        in_specs=[row, pl.BlockSpec((1, d), lambda i: (0, 0))],
        out_specs=row,
        out_shape=jax.ShapeDtypeStruct((t, d), F32),
        compiler_params=_cparams("arbitrary"),
        name="final_norm",
    )(x, g.reshape(1, d))


def _trunk(x, params):
    b, s, d = x.shape
    t = b * s
    x = x.reshape(t, d)
    cos2, sin2 = _rotary_tables(s)
    for l in range(DEPTH):
        p = params[l]
        proj = _inproj(x, p["norm_mix"], p["w_in"]).reshape(b, s, IN_COLS)
        att = _attention(proj, params["band"])
        conv = _short_conv(proj, p["conv_w"])
        ret = _retention(proj, p["log_g"], cos2, sin2)
        x = _outproj(att.reshape(t, ATT_WIDTH), conv.reshape(t, CONV_WIDTH),
                     ret.reshape(t, RET_WIDTH), x, p["w_out"])
        h, eidx_t, gate_t = _peer_topk(x, p["norm_ffn"], p["peer_wq"], p["peer_keys"])
        idx = jnp.transpose(eidx_t).reshape(t // PEER_TB, PEER_TB * PEER_SEL)
        x = _peer_experts(idx, p["peer_tab"], h, gate_t, x)
    return _final_norm(x, params["final_norm"]).reshape(b, s, d)


def kernel(x_prompt, x_sample, rel_bias, final_norm, norm_mix, norm_ffn, w_in, conv_w, ret_decay,
           w_out, peer_wq, peer_keys, peer_u, peer_v):
    params = {"band": _attention_band(rel_bias), "final_norm": final_norm}
    for l in range(DEPTH):
        params[l] = {
            "norm_mix": norm_mix[l],
            "norm_ffn": norm_ffn[l],
            "w_in": w_in[l].astype(BF16),
            "conv_w": conv_w[l],
            "log_g": jax.nn.log_sigmoid(ret_decay[l].astype(F32)),
            "w_out": w_out[l].astype(BF16),
            "peer_wq": peer_wq[l].astype(BF16),
            "peer_keys": peer_keys[l].reshape(2 * PEER_HEADS, PEER_NKEYS, PEER_DHALF).astype(BF16),
            "peer_tab": _pack_expert_table(peer_u[l], peer_v[l]),
        }
    return (_trunk(x_prompt, params), _trunk(x_sample, params))
```

```python
import functools
import math

import jax
import jax.numpy as jnp
import numpy as np
from jax import lax
from jax.experimental import pallas as pl
from jax.experimental.pallas import tpu as pltpu

F32 = jnp.float32
BF16 = jnp.bfloat16

D_MODEL = 2048
DEPTH = 2
N_ATT_HEADS = 12
ATT_HEAD_DIM = 64
ATT_WIDTH = N_ATT_HEADS * ATT_HEAD_DIM
DILATED_PATTERNS = ((128, 1), (512, 4), (2048, 16))
REL_BUCKETS = 32
REL_MAX_DIST = 1024
CONV_WIDTH = 512
N_RET_HEADS = 6
RET_HEAD_DIM = 128
RET_WIDTH = N_RET_HEADS * RET_HEAD_DIM
RET_CHUNK = 128
ROPE_BASE = 10000.0
MIX_WIDTH = ATT_WIDTH + CONV_WIDTH + RET_WIDTH
IN_COLS = 3 * ATT_WIDTH + 3 * CONV_WIDTH + 4 * RET_WIDTH
PEER_HEADS = 8
PEER_NKEYS = 128
PEER_EXPERTS = PEER_NKEYS * PEER_NKEYS
PEER_DHALF = 128
PEER_TOPK = 16
PEER_SEL = PEER_HEADS * PEER_TOPK
RMS_EPS = 1e-6
GN_EPS = 1e-6
NEG_INF = -1e30

LANES = 128
SUBLANES = 8
VMEM_LIMIT_BYTES = 56 * 1024 * 1024

ATT_W = 64
ATT_TQ = 128
ATT_TK = ATT_TQ + 2 * ATT_W

COL_AQ, COL_AK, COL_AV = 0, ATT_WIDTH // LANES, 2 * ATT_WIDTH // LANES
COL_CB = 3 * ATT_WIDTH // 256
COL_CC = COL_CB + CONV_WIDTH // 256
COL_CH = COL_CC + CONV_WIDTH // 256
COL_RQ = (3 * ATT_WIDTH + 3 * CONV_WIDTH) // LANES
COL_RK = COL_RQ + N_RET_HEADS
COL_RV = COL_RK + N_RET_HEADS
COL_RG = COL_RV + N_RET_HEADS


def _cparams(*sem):
    return pltpu.CompilerParams(dimension_semantics=sem, vmem_limit_bytes=VMEM_LIMIT_BYTES)


def _rms(x, g):
    ms = jnp.mean(x * x, axis=-1, keepdims=True)
    return x * lax.rsqrt(ms + RMS_EPS) * g


def _inproj_kernel(x_ref, g_ref, w_ref, o_ref):
    h = _rms(x_ref[...], g_ref[...]).astype(BF16)
    o_ref[...] = jnp.dot(h, w_ref[...], preferred_element_type=F32)


def _inproj(x, g, w, *, tm=512, tn=2304):
    t, d = x.shape
    n = w.shape[1]
    return pl.pallas_call(
        _inproj_kernel,
        grid=(n // tn, t // tm),
        in_specs=[pl.BlockSpec((tm, d), lambda j, i: (i, 0)),
                  pl.BlockSpec((1, d), lambda j, i: (0, 0)),
                  pl.BlockSpec((d, tn), lambda j, i: (0, j))],
        out_specs=pl.BlockSpec((tm, tn), lambda j, i: (i, j)),
        out_shape=jax.ShapeDtypeStruct((t, n), F32),
        compiler_params=_cparams("arbitrary", "arbitrary"),
        name="inproj",
    )(x, g.reshape(1, d), w)


def _attn_kernel(q_ref, k_ref, v_ref, band_ref, o_ref, kbuf, vbuf, acc_ref, m_ref, l_ref):
    s_len = q_ref.shape[1]
    lane = lax.broadcasted_iota(jnp.int32, (1, LANES), 1)
    head0 = lane < ATT_HEAD_DIM
    kcol = lax.broadcasted_iota(jnp.int32, (1, ATT_TK), 1)
    zpad = jnp.zeros((ATT_W, LANES), BF16)

    for p, (_, dil) in enumerate(DILATED_PATTERNS):
        sub_len = s_len // dil
        n_chunk = sub_len // ATT_TQ

        def rows(start, size):
            if dil == 1:
                return pl.ds(start, size)
            return pl.ds(start, size, stride=dil)

        def residue_body(r, carry):
            kbuf[pl.ds(0, ATT_W), :] = zpad
            vbuf[pl.ds(0, ATT_W), :] = zpad
            kbuf[pl.ds(ATT_W + sub_len, ATT_W), :] = zpad
            vbuf[pl.ds(ATT_W + sub_len, ATT_W), :] = zpad

            def stage_body(c, carry2):
                src = rows(r + c * ATT_TQ * dil, ATT_TQ)
                dst = pl.ds(pl.multiple_of(ATT_W + c * ATT_TQ, ATT_W), ATT_TQ)
                kbuf[dst, :] = k_ref[0, src, :].astype(BF16)
                vbuf[dst, :] = v_ref[0, src, :].astype(BF16)
                return carry2

            lax.fori_loop(0, n_chunk, stage_body, 0)

            def chunk_body(c, carry2):
                qrows = rows(r + c * ATT_TQ * dil, ATT_TQ)
                q = q_ref[0, qrows, :] * (ATT_HEAD_DIM ** -0.5)
                win = pl.ds(pl.multiple_of(c * ATT_TQ, ATT_TQ), ATT_TK)
                kw = kbuf[win, :]
                vw = vbuf[win, :]
                kpos = c * ATT_TQ - ATT_W + kcol
                pos_ok = (kpos >= 0) & (kpos < sub_len)
                pv, mm, ll = [], [], []
                for h in range(2):
                    sel = head0 if h == 0 else jnp.logical_not(head0)
                    qh = jnp.where(sel, q, 0.0).astype(BF16)
                    s = lax.dot_general(qh, kw, (((1,), (1,)), ((), ())),
                                        preferred_element_type=F32)
                    s = jnp.where(pos_ok, s + band_ref[p, h], NEG_INF)
                    m = jnp.max(s, axis=1, keepdims=True)
                    e = jnp.exp(s - m)
                    ll.append(jnp.sum(e, axis=1, keepdims=True))
                    mm.append(m)
                    pv.append(jnp.dot(e.astype(BF16), vw, preferred_element_type=F32))
                acc_n = jnp.where(head0, pv[0], pv[1])
                m_n = jnp.where(head0, mm[0], mm[1])
                l_n = jnp.where(head0, ll[0], ll[1])
                if p == 0:
                    acc_ref[qrows, :] = acc_n
                    m_ref[qrows, :] = m_n
                    l_ref[qrows, :] = l_n
                else:
                    m_o = m_ref[qrows, :]
                    m_t = jnp.maximum(m_o, m_n)
                    a_o = jnp.exp(m_o - m_t)
                    a_n = jnp.exp(m_n - m_t)
                    acc_ref[qrows, :] = acc_ref[qrows, :] * a_o + acc_n * a_n
                    l_ref[qrows, :] = l_ref[qrows, :] * a_o + l_n * a_n
                    m_ref[qrows, :] = m_t
                return carry2

            lax.fori_loop(0, n_chunk, chunk_body, 0)
            return carry

        lax.fori_loop(0, dil, residue_body, 0)

    def out_body(c, carry):
        rws = pl.ds(pl.multiple_of(c * ATT_TQ, ATT_TQ), ATT_TQ)
        o_ref[0, rws, :] = (acc_ref[rws, :] / l_ref[rws, :]).astype(o_ref.dtype)
        return carry

    lax.fori_loop(0, s_len // ATT_TQ, out_body, 0)


def _attention(proj, band):
    b, s, _ = proj.shape
    n_pair = N_ATT_HEADS // 2
    blk = lambda off: pl.BlockSpec((1, s, LANES), lambda bi, hp: (bi, 0, off + hp))
    return pl.pallas_call(
        _attn_kernel,
        grid=(b, n_pair),
        in_specs=[blk(COL_AQ), blk(COL_AK), blk(COL_AV),
                  pl.BlockSpec((len(DILATED_PATTERNS), 2, ATT_TQ, ATT_TK),
                               lambda bi, hp: (0, hp, 0, 0))],
        out_specs=pl.BlockSpec((1, s, LANES), lambda bi, hp: (bi, 0, hp)),
        out_shape=jax.ShapeDtypeStruct((b, s, ATT_WIDTH), BF16),
        scratch_shapes=[pltpu.VMEM((s + 2 * ATT_W, LANES), BF16),
                        pltpu.VMEM((s + 2 * ATT_W, LANES), BF16),
                        pltpu.VMEM((s, LANES), F32),
                        pltpu.VMEM((s, LANES), F32),
                        pltpu.VMEM((s, LANES), F32)],
        compiler_params=_cparams("arbitrary", "arbitrary"),
        name="dilated_attention",
    )(proj, proj, proj, band)


def _t5_bucket(rel):
    nb = REL_BUCKETS // 2
    ret = (rel > 0).astype(np.int32) * nb
    n = np.abs(rel)
    max_exact = nb // 2
    nf = np.maximum(n, 1).astype(np.float32)
    large = max_exact + (np.log(nf / max_exact) / math.log(REL_MAX_DIST / max_exact)
                         * (nb - max_exact)).astype(np.int32)
    large = np.minimum(large, nb - 1)
    return ret + np.where(n < max_exact, n, large)


def _attention_band(rel_bias):
    i = np.arange(ATT_TQ)[:, None]
    c = np.arange(ATT_TK)[None, :]
    rel = c - ATT_W - i
    in_band = np.abs(rel) <= ATT_W
    out = []
    for _, dil in DILATED_PATTERNS:
        bias = jnp.transpose(rel_bias[_t5_bucket(dil * rel)], (2, 0, 1)).astype(F32)
        out.append(jnp.where(in_band[None], bias, NEG_INF))
    return jnp.stack(out, axis=0)


CONV_ROWS = 512


def _conv_kernel(cb_ref, cc_ref, ch_ref, w_ref, o_ref, zbuf):
    s_len = cb_ref.shape[1]
    width = cb_ref.shape[2]
    zero = jnp.zeros((SUBLANES, width), F32)
    zbuf[pl.ds(0, SUBLANES), :] = zero
    zbuf[pl.ds(SUBLANES + s_len, SUBLANES), :] = zero
    n_blk = s_len // CONV_ROWS

    for c in range(n_blk):
        r0 = c * CONV_ROWS
        zbuf[pl.ds(SUBLANES + r0, CONV_ROWS), :] = (
            cc_ref[0, pl.ds(r0, CONV_ROWS), :] * ch_ref[0, pl.ds(r0, CONV_ROWS), :])
    w0 = w_ref[0:1, :]
    w1 = w_ref[1:2, :]
    w2 = w_ref[2:3, :]

    for c in range(n_blk):
        r0 = c * CONV_ROWS
        y = (w0 * zbuf[pl.ds(SUBLANES - 1 + r0, CONV_ROWS), :]
             + w1 * zbuf[pl.ds(SUBLANES + r0, CONV_ROWS), :]
             + w2 * zbuf[pl.ds(SUBLANES + 1 + r0, CONV_ROWS), :])
        o_ref[0, pl.ds(r0, CONV_ROWS), :] = (cb_ref[0, pl.ds(r0, CONV_ROWS), :] * y).astype(o_ref.dtype)


def _short_conv(proj, conv_w):
    b, s, _ = proj.shape
    cw = 256
    blk = lambda off: pl.BlockSpec((1, s, cw), lambda bi, j: (bi, 0, off + j))
    return pl.pallas_call(
        _conv_kernel,
        grid=(b, CONV_WIDTH // cw),
        in_specs=[blk(COL_CB), blk(COL_CC), blk(COL_CH),
                  pl.BlockSpec((3, cw), lambda bi, j: (0, j))],
        out_specs=pl.BlockSpec((1, s, cw), lambda bi, j: (bi, 0, j)),
        out_shape=jax.ShapeDtypeStruct((b, s, CONV_WIDTH), BF16),
        scratch_shapes=[pltpu.VMEM((s + 2 * SUBLANES, cw), F32)],
        compiler_params=_cparams("arbitrary", "arbitrary"),
        name="short_conv",
    )(proj, proj, proj, conv_w)


def _ret_kernel(lg_ref, q_ref, k_ref, v_ref, g_ref, cos_ref, sin_ref, o_ref, qs, ks, os):
    s_len = q_ref.shape[1]
    c_len = RET_CHUNK
    n_chunk = s_len // c_len
    h = pl.program_id(1)
    lg_f = lg_ref[0, h]
    lg_b = lg_ref[1, h]

    def rot_body(n, carry):
        rws = pl.ds(pl.multiple_of(n * c_len, c_len), c_len)
        cs = cos_ref[rws, :]
        sn = sin_ref[rws, :]
        q = q_ref[0, rws, :]
        k = k_ref[0, rws, :]
        qs[rws, :] = q * cs + pltpu.roll(q, RET_HEAD_DIM // 2, 1) * sn
        ks[rws, :] = (k * cs + pltpu.roll(k, RET_HEAD_DIM // 2, 1) * sn) * (RET_HEAD_DIM ** -0.5)
        return carry

    lax.fori_loop(0, n_chunk, rot_body, 0)

    ri = lax.broadcasted_iota(jnp.int32, (c_len, c_len), 0)
    ci = lax.broadcasted_iota(jnp.int32, (c_len, c_len), 1)
    diff = (ri - ci).astype(F32)
    col = lax.broadcasted_iota(jnp.int32, (c_len, 1), 0).astype(F32)
    one = jnp.ones((1, 1), F32)

    dec_f = jnp.where(diff >= 0, jnp.exp(lg_f * jnp.maximum(diff, 0.0)), 0.0)
    kw_f = jnp.exp(lg_f * (c_len - 1 - col))
    qw_f = jnp.exp(lg_f * (col + 1))
    gc_f = jnp.exp(lg_f * c_len * one)
    dec_b = jnp.where(diff < 0, jnp.exp(lg_b * jnp.maximum(-diff, 0.0)), 0.0)
    kw_b = jnp.exp(lg_b * col)
    qw_b = jnp.exp(lg_b * (c_len - col))
    gc_b = jnp.exp(lg_b * c_len * one)

    def direction(n, state, dec, kw, qw, gc):
        rws = pl.ds(pl.multiple_of(n * c_len, c_len), c_len)
        q = qs[rws, :]
        k = ks[rws, :]
        v = v_ref[0, rws, :].astype(BF16)
        qb = q.astype(BF16)
        sc = lax.dot_general(qb, k.astype(BF16), (((1,), (1,)), ((), ())),
                             preferred_element_type=F32)
        o_in = jnp.dot((sc * dec).astype(BF16), v, preferred_element_type=F32)
        o_x = jnp.dot(qb, state.astype(BF16), preferred_element_type=F32) * qw
        kv = jnp.dot(jnp.transpose(k * kw).astype(BF16), v, preferred_element_type=F32)
        return rws, o_in + o_x, gc * state + kv

    def fwd_body(n, state):
        rws, o, state = direction(n, state, dec_f, kw_f, qw_f, gc_f)
        os[rws, :] = o
        return state

    lax.fori_loop(0, n_chunk, fwd_body, jnp.zeros((RET_HEAD_DIM, RET_HEAD_DIM), F32))

    def bwd_body(i, state):
        n = n_chunk - 1 - i
        rws, o, state = direction(n, state, dec_b, kw_b, qw_b, gc_b)
        o = o + os[rws, :]
        mu = jnp.mean(o, axis=-1, keepdims=True)
        oc = o - mu
        var = jnp.mean(oc * oc, axis=-1, keepdims=True)
        on = oc * lax.rsqrt(var + GN_EPS)
        g = g_ref[0, rws, :]
        o_ref[0, rws, :] = (g * jax.nn.sigmoid(g) * on).astype(o_ref.dtype)
        return state

    lax.fori_loop(0, n_chunk, bwd_body, jnp.zeros((RET_HEAD_DIM, RET_HEAD_DIM), F32))


def _retention(proj, log_g, cos2, sin2):
    b, s, _ = proj.shape
    blk = lambda off: pl.BlockSpec((1, s, LANES), lambda bi, h: (bi, 0, off + h))
    tab = pl.BlockSpec((s, LANES), lambda bi, h: (0, 0))
    return pl.pallas_call(
        _ret_kernel,
        grid=(b, N_RET_HEADS),
        in_specs=[pl.BlockSpec(memory_space=pltpu.SMEM),
                  blk(COL_RQ), blk(COL_RK), blk(COL_RV), blk(COL_RG), tab, tab],
        out_specs=pl.BlockSpec((1, s, LANES), lambda bi, h: (bi, 0, h)),
        out_shape=jax.ShapeDtypeStruct((b, s, RET_WIDTH), BF16),
        scratch_shapes=[pltpu.VMEM((s, LANES), F32)] * 3,
        compiler_params=_cparams("arbitrary", "arbitrary"),
        name="retention",
    )(log_g, proj, proj, proj, proj, cos2, sin2)


def _rotary_tables(s):
    d = RET_HEAD_DIM
    inv = ROPE_BASE ** (-jnp.arange(0, d, 2, dtype=F32) / d)
    ang = jnp.arange(s, dtype=F32)[:, None] * inv[None, :]
    cos, sin = jnp.cos(ang), jnp.sin(ang)
    return jnp.concatenate([cos, cos], axis=-1), jnp.concatenate([-sin, sin], axis=-1)


def _outproj_kernel(att_ref, conv_ref, ret_ref, x_ref, w_ref, o_ref):
    acc = jnp.dot(att_ref[...], w_ref[pl.ds(0, ATT_WIDTH), :], preferred_element_type=F32)
    acc += jnp.dot(conv_ref[...], w_ref[pl.ds(ATT_WIDTH, CONV_WIDTH), :], preferred_element_type=F32)
    acc += jnp.dot(ret_ref[...], w_ref[pl.ds(ATT_WIDTH + CONV_WIDTH, RET_WIDTH), :],
                   preferred_element_type=F32)
    o_ref[...] = x_ref[...] + acc


def _outproj(att, conv, ret, x, w, *, tm=512):
    t, d = x.shape
    row = lambda width: pl.BlockSpec((tm, width), lambda i: (i, 0))
    return pl.pallas_call(
        _outproj_kernel,
        grid=(t // tm,),
        in_specs=[row(ATT_WIDTH), row(CONV_WIDTH), row(RET_WIDTH), row(d),
                  pl.BlockSpec((MIX_WIDTH, d), lambda i: (0, 0))],
        out_specs=row(d),
        out_shape=jax.ShapeDtypeStruct((t, d), F32),
        compiler_params=_cparams("arbitrary"),
        name="outproj",
    )(att, conv, ret, x, w)


def _top16(s, payload=None):
    n = s.shape[0]
    row = lax.broadcasted_iota(jnp.int32, s.shape, 0)
    vals, picks = [], []
    for _ in range(PEER_TOPK):
        m = jnp.max(s, axis=0, keepdims=True)
        idx = jnp.min(jnp.where(s == m, row, n), axis=0, keepdims=True)
        hit = row == idx
        vals.append(m)
        if payload is None:
            picks.append(idx)
        else:
            picks.append(jnp.max(jnp.where(hit, payload, -1), axis=0, keepdims=True))
        s = jnp.where(hit, -jnp.inf, s)
    return vals, picks


def _peer_topk_kernel(x_ref, g_ref, wq_ref, keys_ref, h_ref, eidx_ref, gate_ref, sv_s, si_s):
    h = _rms(x_ref[...], g_ref[...])
    h_ref[...] = h
    q = jnp.dot(h.astype(BF16), wq_ref[...], preferred_element_type=F32).astype(BF16)
    for grp in range(2 * PEER_HEADS):
        s = lax.dot_general(keys_ref[grp], q[:, grp * PEER_DHALF:(grp + 1) * PEER_DHALF],
                            (((1,), (1,)), ((), ())), preferred_element_type=F32)
        vals, idxs = _top16(s)
        for j in range(PEER_TOPK):
            sv_s[grp, pl.ds(j, 1), :] = vals[j]
            si_s[grp, pl.ds(j, 1), :] = idxs[j]
    for hd in range(PEER_HEADS):
        sv0, sv1 = sv_s[2 * hd], sv_s[2 * hd + 1]
        si0, si1 = si_s[2 * hd], si_s[2 * hd + 1]
        cand = jnp.concatenate([sv0[a:a + 1, :] + sv1 for a in range(PEER_TOPK)], axis=0)
        eid = jnp.concatenate([si0[a:a + 1, :] * PEER_NKEYS + si1 for a in range(PEER_TOPK)], axis=0)
        vals, picks = _top16(cand, eid)
        cv = jnp.concatenate(vals, axis=0)
        e = jnp.exp(cv - vals[0])
        gate_ref[pl.ds(hd * PEER_TOPK, PEER_TOPK), :] = e / jnp.sum(e, axis=0, keepdims=True)
        eidx_ref[pl.ds(hd * PEER_TOPK, PEER_TOPK), :] = jnp.concatenate(picks, axis=0)


def _peer_topk(x, g, wq, keys, *, tm=256):
    t, d = x.shape
    return pl.pallas_call(
        _peer_topk_kernel,
        grid=(t // tm,),
        in_specs=[pl.BlockSpec((tm, d), lambda i: (i, 0)),
                  pl.BlockSpec((1, d), lambda i: (0, 0)),
                  pl.BlockSpec((d, 2 * PEER_HEADS * PEER_DHALF), lambda i: (0, 0)),
                  pl.BlockSpec((2 * PEER_HEADS, PEER_NKEYS, PEER_DHALF), lambda i: (0, 0, 0))],
        out_specs=[pl.BlockSpec((tm, d), lambda i: (i, 0)),
                   pl.BlockSpec((PEER_SEL, tm), lambda i: (0, i)),
                   pl.BlockSpec((PEER_SEL, tm), lambda i: (0, i))],
        out_shape=[jax.ShapeDtypeStruct((t, d), F32),
                   jax.ShapeDtypeStruct((PEER_SEL, t), jnp.int32),
                   jax.ShapeDtypeStruct((PEER_SEL, t), F32)],
        scratch_shapes=[pltpu.VMEM((2 * PEER_HEADS, PEER_TOPK, tm), F32),
                        pltpu.VMEM((2 * PEER_HEADS, PEER_TOPK, tm), jnp.int32)],
        compiler_params=_cparams("arbitrary"),
        name="peer_topk",
    )(x, g.reshape(1, d), wq, keys)


PEER_TB = 256
N_SLOT = 8
LOOKAHEAD = N_SLOT - 1
SLAB = D_MODEL // LANES
EXPERT_GROUP = 16
N_GROUP = PEER_SEL // EXPERT_GROUP
ISSUE_PER_SEGMENT = PEER_SEL // (2 * N_GROUP)


def _gelu_tanh(x):
    return 0.5 * x * (1.0 + jnp.tanh(math.sqrt(2.0 / math.pi) * (x + 0.044715 * (x * x * x))))


def _peer_expert_kernel(idx_hbm, tab_hbm, sel_ref, h_ref, gate_ref, x_ref, o_ref,
                        idx_s, gbuf, wsc, isem, gsem):
    i = pl.program_id(0)
    icp = pltpu.make_async_copy(idx_hbm.at[i], idx_s, isem)
    icp.start()
    icp.wait()

    def issue(t, slot, j0, n):
        for j in range(j0, j0 + n):
            pltpu.make_async_copy(tab_hbm.at[idx_s[t * PEER_SEL + j]], gbuf.at[slot, j],
                                  gsem.at[slot]).start(priority=j % 2)

    def wait(slot):
        pltpu.make_async_copy(tab_hbm.at[pl.ds(0, PEER_SEL)], gbuf.at[slot], gsem.at[slot]).wait()

    lane_t = lax.broadcasted_iota(jnp.int32, (1, PEER_TB), 1)
    hi_mask = jnp.uint32(0xFFFF0000)

    def key_side(t, slot, issue_args=None):
        x3 = h_ref[t]
        x0, x1 = x3[0:SUBLANES], x3[SUBLANES:SLAB]
        q = jnp.zeros((PEER_SEL, LANES), F32)
        for grp in range(N_GROUP):
            parts = []
            for k in range(EXPERT_GROUP):
                e = grp * EXPERT_GROUP + k
                u0 = pltpu.bitcast(gbuf[slot, e, 0:SUBLANES, :] << 16, F32)
                u1 = pltpu.bitcast(gbuf[slot, e, SUBLANES:SLAB, :] << 16, F32)
                parts.append(u0 * x0 + u1 * x1)
            pb = jnp.concatenate(parts, axis=0).astype(BF16)
            q = q + jnp.dot(sel_ref[:, grp * LANES:(grp + 1) * LANES], pb, preferred_element_type=F32)
            if issue_args is not None:
                issue(*issue_args, grp * ISSUE_PER_SEGMENT, ISSUE_PER_SEGMENT)
        return q

    def weights(t, q):
        act = jnp.sum(q, axis=1, keepdims=True)
        gate = jnp.sum(jnp.where(lane_t == t, gate_ref[...], 0.0), axis=1, keepdims=True)
        wsc[...] = jnp.broadcast_to(gate * _gelu_tanh(act), (PEER_SEL, LANES))

    def value_side(t, slot, issue_args=None):
        acc = [jnp.zeros((SUBLANES, LANES), F32) for _ in range(4)]
        for grp in range(N_GROUP):
            for k in range(EXPERT_GROUP):
                e = grp * EXPERT_GROUP + k
                w = wsc[pl.ds(e, 1), :]
                v0 = pltpu.bitcast(gbuf[slot, e, 0:SUBLANES, :] & hi_mask, F32)
                v1 = pltpu.bitcast(gbuf[slot, e, SUBLANES:SLAB, :] & hi_mask, F32)
                acc[2 * (k % 2)] = acc[2 * (k % 2)] + w * v0
                acc[2 * (k % 2) + 1] = acc[2 * (k % 2) + 1] + w * v1
            if issue_args is not None:
                issue(*issue_args, (N_GROUP + grp) * ISSUE_PER_SEGMENT, ISSUE_PER_SEGMENT)
        o_ref[t] = x_ref[t] + jnp.concatenate([acc[0] + acc[2], acc[1] + acc[3]], axis=0)

    def prologue_body(n, carry):
        k = n // (PEER_SEL // 2)
        j = (n % (PEER_SEL // 2)) * 2
        for b in range(2):
            pltpu.make_async_copy(tab_hbm.at[idx_s[k * PEER_SEL + j + b]], gbuf.at[k, j + b],
                                  gsem.at[k]).start(priority=b)
        return carry

    lax.fori_loop(0, LOOKAHEAD * (PEER_SEL // 2), prologue_body, 0)
    wait(0)
    q0 = key_side(0, 0)

    def token_body(t, q):
        slot = t % N_SLOT
        slot_1 = (t + 1) % N_SLOT
        gather = (jnp.minimum(t + LOOKAHEAD, PEER_TB - 1), (t + LOOKAHEAD) % N_SLOT)
        weights(t, q)
        wait(slot_1)
        q = key_side(t + 1, slot_1, gather)
        value_side(t, slot, gather)
        return q

    n_here = jnp.minimum(PEER_TB, pl.num_programs(0) * PEER_TB - i * PEER_TB)
    q_last = lax.fori_loop(0, n_here - 1, token_body, q0)
    weights(PEER_TB - 1, q_last)
    value_side(PEER_TB - 1, (PEER_TB - 1) % N_SLOT)
    for k in range(LOOKAHEAD - 1):
        wait(k)


def _peer_experts(idx, tab, h, gate_t, x):
    t, d = x.shape
    n_blk = t // PEER_TB
    slabs = pl.BlockSpec((PEER_TB, SLAB, LANES), lambda i: (i, 0, 0))
    sel = (jnp.arange(PEER_SEL)[:, None] == jnp.arange(PEER_SEL * SUBLANES)[None, :] // SUBLANES).astype(BF16)
    out = pl.pallas_call(
        _peer_expert_kernel,
        grid=(n_blk,),
        in_specs=[pl.BlockSpec(memory_space=pl.ANY),
                  pl.BlockSpec(memory_space=pl.ANY),
                  pl.BlockSpec((PEER_SEL, PEER_SEL * SUBLANES), lambda i: (0, 0)),
                  slabs,
                  pl.BlockSpec((PEER_SEL, PEER_TB), lambda i: (0, i)),
                  slabs],
        out_specs=slabs,
        out_shape=jax.ShapeDtypeStruct((t, SLAB, LANES), F32),
        scratch_shapes=[pltpu.SMEM((PEER_TB * PEER_SEL,), jnp.int32),
                        pltpu.VMEM((N_SLOT, PEER_SEL, SLAB, LANES), jnp.uint32),
                        pltpu.VMEM((PEER_SEL, LANES), F32),
                        pltpu.SemaphoreType.DMA,
                        pltpu.SemaphoreType.DMA((N_SLOT,))],
        compiler_params=_cparams("arbitrary"),
        name="peer_experts",
    )(idx, tab, sel, h.reshape(t, SLAB, LANES), gate_t, x.reshape(t, SLAB, LANES))
    return out.reshape(t, d)


def _pack_expert_table(u, v):
    ub = lax.bitcast_convert_type(u.astype(BF16), jnp.uint16).astype(jnp.uint32)
    vb = lax.bitcast_convert_type(v.astype(BF16), jnp.uint16).astype(jnp.uint32)
    return (ub | (vb << 16)).reshape(u.shape[0], SLAB, LANES)


def _final_norm_kernel(x_ref, g_ref, o_ref):
    o_ref[...] = _rms(x_ref[...], g_ref[...])


def _final_norm(x, g, *, tm=512):
    t, d = x.shape
    row = pl.BlockSpec((tm, d), lambda i: (i, 0))
    return pl.pallas_call(
        _final_norm_kernel,
        grid=(t // tm,),
        in_specs=[row, pl.BlockSpec((1, d), lambda i: (0, 0))],
        out_specs=row,
        out_shape=jax.ShapeDtypeStruct((t, d), F32),
        compiler_params=_cparams("arbitrary"),
        name="final_norm",
    )(x, g.reshape(1, d))


def _trunk(x, params):
    b, s, d = x.shape
    t = b * s
    x = x.reshape(t, d)
    cos2, sin2 = _rotary_tables(s)
    for l in range(DEPTH):
        p = params[l]
        proj = _inproj(x, p["norm_mix"], p["w_in"]).reshape(b, s, IN_COLS)
        att = _attention(proj, params["band"])
        conv = _short_conv(proj, p["conv_w"])
        ret = _retention(proj, p["log_g"], cos2, sin2)
        x = _outproj(att.reshape(t, ATT_WIDTH), conv.reshape(t, CONV_WIDTH),
                     ret.reshape(t, RET_WIDTH), x, p["w_out"])
        h, eidx_t, gate_t = _peer_topk(x, p["norm_ffn"], p["peer_wq"], p["peer_keys"])
        idx = jnp.transpose(eidx_t).reshape(t // PEER_TB, PEER_TB * PEER_SEL)
        x = _peer_experts(idx, p["peer_tab"], h, gate_t, x)
    return _final_norm(x, params["final_norm"]).reshape(b, s, d)


def kernel(x_prompt, x_sample, rel_bias, final_norm, norm_mix, norm_ffn, w_in, conv_w, ret_decay,
           w_out, peer_wq, peer_keys, peer_u, peer_v):
    params = {"band": _attention_band(rel_bias), "final_norm": final_norm}
    for l in range(DEPTH):
        params[l] = {
            "norm_mix": norm_mix[l],
            "norm_ffn": norm_ffn[l],
            "w_in": w_in[l].astype(BF16),
            "conv_w": conv_w[l],
            "log_g": jax.nn.log_sigmoid(ret_decay[l].astype(F32)),
            "w_out": w_out[l].astype(BF16),
            "peer_wq": peer_wq[l].astype(BF16),
            "peer_keys": peer_keys[l].reshape(2 * PEER_HEADS, PEER_NKEYS, PEER_DHALF).astype(BF16),
            "peer_tab": _pack_expert_table(peer_u[l], peer_v[l]),
        }
    return (_trunk(x_prompt, params), _trunk(x_sample, params))
```

```python
import functools
import math

import jax
import jax.numpy as jnp
import numpy as np
from jax import lax
from jax.experimental import pallas as pl
from jax.experimental.pallas import tpu as pltpu

F32 = jnp.float32
BF16 = jnp.bfloat16

D_MODEL = 2048
DEPTH = 2
N_ATT_HEADS = 12
ATT_HEAD_DIM = 64
ATT_WIDTH = N_ATT_HEADS * ATT_HEAD_DIM
DILATED_PATTERNS = ((128, 1), (512, 4), (2048, 16))
REL_BUCKETS = 32
REL_MAX_DIST = 1024
CONV_WIDTH = 512
N_RET_HEADS = 6
RET_HEAD_DIM = 128
RET_WIDTH = N_RET_HEADS * RET_HEAD_DIM
RET_CHUNK = 128
ROPE_BASE = 10000.0
MIX_WIDTH = ATT_WIDTH + CONV_WIDTH + RET_WIDTH
IN_COLS = 3 * ATT_WIDTH + 3 * CONV_WIDTH + 4 * RET_WIDTH
PEER_HEADS = 8
PEER_NKEYS = 128
PEER_EXPERTS = PEER_NKEYS * PEER_NKEYS
PEER_DHALF = 128
PEER_TOPK = 16
PEER_SEL = PEER_HEADS * PEER_TOPK
RMS_EPS = 1e-6
GN_EPS = 1e-6
NEG_INF = -1e30

LANES = 128
SUBLANES = 8
VMEM_LIMIT_BYTES = 56 * 1024 * 1024

ATT_W = 64
ATT_TQ = 128
ATT_TK = ATT_TQ + 2 * ATT_W
ATT_UNROLL = 8

COL_AQ, COL_AK, COL_AV = 0, ATT_WIDTH // LANES, 2 * ATT_WIDTH // LANES
COL_CB = 3 * ATT_WIDTH // 256
COL_CC = COL_CB + CONV_WIDTH // 256
COL_CH = COL_CC + CONV_WIDTH // 256
COL_RQ = (3 * ATT_WIDTH + 3 * CONV_WIDTH) // LANES
COL_RK = COL_RQ + N_RET_HEADS
COL_RV = COL_RK + N_RET_HEADS
COL_RG = COL_RV + N_RET_HEADS


def _cparams(*sem):
    return pltpu.CompilerParams(dimension_semantics=sem, vmem_limit_bytes=VMEM_LIMIT_BYTES)


def _rms(x, g):
    ms = jnp.mean(x * x, axis=-1, keepdims=True)
    return x * lax.rsqrt(ms + RMS_EPS) * g


def _inproj_kernel(x_ref, g_ref, w_ref, o_ref):
    h = _rms(x_ref[...], g_ref[...]).astype(BF16)
    o_ref[...] = jnp.dot(h, w_ref[...], preferred_element_type=F32)


def _inproj(x, g, w, *, tm=512, tn=2304):
    t, d = x.shape
    n = w.shape[1]
    return pl.pallas_call(
        _inproj_kernel,
        grid=(n // tn, t // tm),
        in_specs=[pl.BlockSpec((tm, d), lambda j, i: (i, 0)),
                  pl.BlockSpec((1, d), lambda j, i: (0, 0)),
                  pl.BlockSpec((d, tn), lambda j, i: (0, j))],
        out_specs=pl.BlockSpec((tm, tn), lambda j, i: (i, j)),
        out_shape=jax.ShapeDtypeStruct((t, n), F32),
        compiler_params=_cparams("arbitrary", "arbitrary"),
        name="inproj",
    )(x, g.reshape(1, d), w)


def _attn_kernel(q_ref, k_ref, v_ref, band_ref, o_ref, kbuf, vbuf0, vbuf1, acc_ref, m_ref, l_ref):
    s_len = q_ref.shape[1]
    n_flat = s_len // ATT_TQ
    lane = lax.broadcasted_iota(jnp.int32, (1, LANES), 1)
    head0 = lane < ATT_HEAD_DIM
    kcol = lax.broadcasted_iota(jnp.int32, (1, ATT_TK), 1)
    zpad = jnp.zeros((ATT_W, LANES), BF16)
    one_b = jnp.ones((), BF16)

    for p, (_, dil) in enumerate(DILATED_PATTERNS):
        sub_len = s_len // dil
        n_chunk = sub_len // ATT_TQ
        res_rows = sub_len + 2 * ATT_W

        def rows(start, size):
            if dil == 1:
                return pl.ds(start, size)
            return pl.ds(start, size, stride=dil)

        def split(f):
            return f // n_chunk, f % n_chunk

        def pad_body(r, carry):
            for off in (0, ATT_W + sub_len):
                dst = pl.ds(pl.multiple_of(r * res_rows + off, ATT_W), ATT_W)
                kbuf[dst, :] = zpad
                vbuf0[dst, :] = zpad
                vbuf1[dst, :] = zpad
            return carry

        lax.fori_loop(0, dil, pad_body, 0)

        def stage_body(f, carry):
            r, c = split(f)
            src = rows(r + c * ATT_TQ * dil, ATT_TQ)
            dst = pl.ds(pl.multiple_of(r * res_rows + ATT_W + c * ATT_TQ, ATT_W), ATT_TQ)
            kbuf[dst, :] = k_ref[0, src, :].astype(BF16)
            v = v_ref[0, src, :].astype(BF16)
            vbuf0[dst, :] = jnp.where(head0, v, one_b)
            vbuf1[dst, :] = jnp.where(head0, one_b, v)
            return carry

        lax.fori_loop(0, n_flat, stage_body, 0)

        def chunk(f):
            r, c = split(f)
            qrows = rows(r + c * ATT_TQ * dil, ATT_TQ)
            q = q_ref[0, qrows, :] * (ATT_HEAD_DIM ** -0.5)
            win = pl.ds(pl.multiple_of(r * res_rows + c * ATT_TQ, ATT_W), ATT_TK)
            kw = kbuf[win, :]
            kpos = c * ATT_TQ - ATT_W + kcol
            pos_ok = (kpos >= 0) & (kpos < sub_len)
            pv, mm = [], []
            for h, vbuf in enumerate((vbuf0, vbuf1)):
                sel = head0 if h == 0 else jnp.logical_not(head0)
                qh = jnp.where(sel, q, 0.0).astype(BF16)
                s = lax.dot_general(qh, kw, (((1,), (1,)), ((), ())), preferred_element_type=F32)
                s = jnp.where(pos_ok, s + band_ref[p, h], NEG_INF)
                m = jnp.max(s, axis=1, keepdims=True)
                mm.append(m)
                pv.append(jnp.dot(jnp.exp(s - m).astype(BF16), vbuf[win, :], preferred_element_type=F32))
            acc_n = jnp.where(head0, pv[0], pv[1])
            m_n = jnp.where(head0, mm[0], mm[1])
            l_n = pltpu.roll(jnp.where(head0, pv[1], pv[0]), ATT_HEAD_DIM, 1)
            if p == 0:
                acc_ref[qrows, :] = acc_n
                m_ref[qrows, :] = m_n
                l_ref[qrows, :] = l_n
            else:
                m_o = m_ref[qrows, :]
                m_t = jnp.maximum(m_o, m_n)
                a_o = jnp.exp(m_o - m_t)
                a_n = jnp.exp(m_n - m_t)
                acc_ref[qrows, :] = acc_ref[qrows, :] * a_o + acc_n * a_n
                l_ref[qrows, :] = l_ref[qrows, :] * a_o + l_n * a_n
                m_ref[qrows, :] = m_t

        def chunks_body(g, carry):
            for u in range(ATT_UNROLL):
                chunk(g * ATT_UNROLL + u)
            return carry

        lax.fori_loop(0, n_flat // ATT_UNROLL, chunks_body, 0)

    def out_body(c, carry):
        rws = pl.ds(pl.multiple_of(c * ATT_TQ, ATT_TQ), ATT_TQ)
        o_ref[0, rws, :] = (acc_ref[rws, :] / l_ref[rws, :]).astype(o_ref.dtype)
        return carry

    lax.fori_loop(0, s_len // ATT_TQ, out_body, 0)


def _attention(proj, band):
    b, s, _ = proj.shape
    n_pair = N_ATT_HEADS // 2
    staged_rows = s + 2 * ATT_W * max(dil for _, dil in DILATED_PATTERNS)
    blk = lambda off: pl.BlockSpec((1, s, LANES), lambda bi, hp: (bi, 0, off + hp))
    return pl.pallas_call(
        _attn_kernel,
        grid=(b, n_pair),
        in_specs=[blk(COL_AQ), blk(COL_AK), blk(COL_AV),
                  pl.BlockSpec((len(DILATED_PATTERNS), 2, ATT_TQ, ATT_TK),
                               lambda bi, hp: (0, hp, 0, 0))],
        out_specs=pl.BlockSpec((1, s, LANES), lambda bi, hp: (bi, 0, hp)),
        out_shape=jax.ShapeDtypeStruct((b, s, ATT_WIDTH), BF16),
        scratch_shapes=[pltpu.VMEM((staged_rows, LANES), BF16)] * 3 + [
                        pltpu.VMEM((s, LANES), F32),
                        pltpu.VMEM((s, LANES), F32),
                        pltpu.VMEM((s, LANES), F32)],
        compiler_params=_cparams("arbitrary", "arbitrary"),
        name="dilated_attention",
    )(proj, proj, proj, band)


def _t5_bucket(rel):
    nb = REL_BUCKETS // 2
    ret = (rel > 0).astype(np.int32) * nb
    n = np.abs(rel)
    max_exact = nb // 2
    nf = np.maximum(n, 1).astype(np.float32)
    large = max_exact + (np.log(nf / max_exact) / math.log(REL_MAX_DIST / max_exact)
                         * (nb - max_exact)).astype(np.int32)
    large = np.minimum(large, nb - 1)
    return ret + np.where(n < max_exact, n, large)


def _attention_band(rel_bias):
    span = ATT_TQ - 1 + ATT_W
    rel = np.arange(-span, ATT_TK - ATT_W)
    out = []
    for _, dil in DILATED_PATTERNS:
        diag = jnp.where((np.abs(rel) <= ATT_W)[:, None], rel_bias[_t5_bucket(dil * rel)].astype(F32), NEG_INF)
        diag = jnp.transpose(diag)
        out.append(jnp.stack([diag[:, ATT_TQ - 1 - i:ATT_TQ - 1 - i + ATT_TK] for i in range(ATT_TQ)], axis=1))
    return jnp.stack(out, axis=0)


CONV_ROWS = 512


def _conv_kernel(cb_ref, cc_ref, ch_ref, w_ref, o_ref, zbuf):
    s_len = cb_ref.shape[1]
    width = cb_ref.shape[2]
    zero = jnp.zeros((SUBLANES, width), F32)
    zbuf[pl.ds(0, SUBLANES), :] = zero
    zbuf[pl.ds(SUBLANES + s_len, SUBLANES), :] = zero
    n_blk = s_len // CONV_ROWS

    for c in range(n_blk):
        r0 = c * CONV_ROWS
        zbuf[pl.ds(SUBLANES + r0, CONV_ROWS), :] = (
            cc_ref[0, pl.ds(r0, CONV_ROWS), :] * ch_ref[0, pl.ds(r0, CONV_ROWS), :])
    w0 = w_ref[0:1, :]
    w1 = w_ref[1:2, :]
    w2 = w_ref[2:3, :]

    for c in range(n_blk):
        r0 = c * CONV_ROWS
        y = (w0 * zbuf[pl.ds(SUBLANES - 1 + r0, CONV_ROWS), :]
             + w1 * zbuf[pl.ds(SUBLANES + r0, CONV_ROWS), :]
             + w2 * zbuf[pl.ds(SUBLANES + 1 + r0, CONV_ROWS), :])
        o_ref[0, pl.ds(r0, CONV_ROWS), :] = (cb_ref[0, pl.ds(r0, CONV_ROWS), :] * y).astype(o_ref.dtype)


def _short_conv(proj, conv_w):
    b, s, _ = proj.shape
    cw = 256
    blk = lambda off: pl.BlockSpec((1, s, cw), lambda bi, j: (bi, 0, off + j))
    return pl.pallas_call(
        _conv_kernel,
        grid=(b, CONV_WIDTH // cw),
        in_specs=[blk(COL_CB), blk(COL_CC), blk(COL_CH),
                  pl.BlockSpec((3, cw), lambda bi, j: (0, j))],
        out_specs=pl.BlockSpec((1, s, cw), lambda bi, j: (bi, 0, j)),
        out_shape=jax.ShapeDtypeStruct((b, s, CONV_WIDTH), BF16),
        scratch_shapes=[pltpu.VMEM((s + 2 * SUBLANES, cw), F32)],
        compiler_params=_cparams("arbitrary", "arbitrary"),
        name="short_conv",
    )(proj, proj, proj, conv_w)


RET_UNROLL = 4


def _ret_kernel(lg_ref, q_ref, k_ref, v_ref, g_ref, cos_ref, sin_ref, o_ref, qs, ks, os_f, os_b):
    s_len = q_ref.shape[1]
    c_len = RET_CHUNK
    n_chunk = s_len // c_len
    h = pl.program_id(1)
    lg_f = lg_ref[0, h]
    lg_b = lg_ref[1, h]

    def rot_body(n, carry):
        rws = pl.ds(pl.multiple_of(n * c_len, c_len), c_len)
        cs = cos_ref[rws, :]
        sn = sin_ref[rws, :]
        q = q_ref[0, rws, :]
        k = k_ref[0, rws, :]
        qs[rws, :] = q * cs + pltpu.roll(q, RET_HEAD_DIM // 2, 1) * sn
        ks[rws, :] = (k * cs + pltpu.roll(k, RET_HEAD_DIM // 2, 1) * sn) * (RET_HEAD_DIM ** -0.5)
        return carry

    lax.fori_loop(0, n_chunk, rot_body, 0)

    ri = lax.broadcasted_iota(jnp.int32, (c_len, c_len), 0)
    ci = lax.broadcasted_iota(jnp.int32, (c_len, c_len), 1)
    diff = (ri - ci).astype(F32)
    col = lax.broadcasted_iota(jnp.int32, (c_len, 1), 0).astype(F32)
    one = jnp.ones((1, 1), F32)

    dec_f = jnp.where(diff >= 0, jnp.exp(lg_f * jnp.maximum(diff, 0.0)), 0.0)
    kw_f = jnp.exp(lg_f * (c_len - 1 - col))
    qw_f = jnp.exp(lg_f * (col + 1))
    gc_f = jnp.exp(lg_f * c_len * one)
    dec_b = jnp.where(diff < 0, jnp.exp(lg_b * jnp.maximum(-diff, 0.0)), 0.0)
    kw_b = jnp.exp(lg_b * col)
    qw_b = jnp.exp(lg_b * (c_len - col))
    gc_b = jnp.exp(lg_b * c_len * one)

    def direction(n, state, dec, kw, qw, gc):
        rws = pl.ds(pl.multiple_of(n * c_len, c_len), c_len)
        q = qs[rws, :]
        k = ks[rws, :]
        v = v_ref[0, rws, :].astype(BF16)
        qb = q.astype(BF16)
        sc = lax.dot_general(qb, k.astype(BF16), (((1,), (1,)), ((), ())),
                             preferred_element_type=F32)
        o_in = jnp.dot((sc * dec).astype(BF16), v, preferred_element_type=F32)
        o_x = jnp.dot(qb, state.astype(BF16), preferred_element_type=F32) * qw
        kv = jnp.dot(jnp.transpose(k * kw).astype(BF16), v, preferred_element_type=F32)
        return rws, o_in + o_x, gc * state + kv

    def sweep_body(g, states):
        s_f, s_b = states
        for u in range(RET_UNROLL):
            n = g * RET_UNROLL + u
            rws, o, s_f = direction(n, s_f, dec_f, kw_f, qw_f, gc_f)
            os_f[rws, :] = o
            rws, o, s_b = direction(n_chunk - 1 - n, s_b, dec_b, kw_b, qw_b, gc_b)
            os_b[rws, :] = o
        return s_f, s_b

    zero_state = jnp.zeros((RET_HEAD_DIM, RET_HEAD_DIM), F32)
    lax.fori_loop(0, n_chunk // RET_UNROLL, sweep_body, (zero_state, zero_state))

    def norm_body(g, carry):
        for u in range(RET_UNROLL):
            rws = pl.ds(pl.multiple_of((g * RET_UNROLL + u) * c_len, c_len), c_len)
            o = os_f[rws, :] + os_b[rws, :]
            mu = jnp.mean(o, axis=-1, keepdims=True)
            oc = o - mu
            var = jnp.mean(oc * oc, axis=-1, keepdims=True)
            on = oc * lax.rsqrt(var + GN_EPS)
            gt = g_ref[0, rws, :]
            o_ref[0, rws, :] = (gt * jax.nn.sigmoid(gt) * on).astype(o_ref.dtype)
        return carry

    lax.fori_loop(0, n_chunk // RET_UNROLL, norm_body, 0)


def _retention(proj, log_g, cos2, sin2):
    b, s, _ = proj.shape
    blk = lambda off: pl.BlockSpec((1, s, LANES), lambda bi, h: (bi, 0, off + h))
    tab = pl.BlockSpec((s, LANES), lambda bi, h: (0, 0))
    return pl.pallas_call(
        _ret_kernel,
        grid=(b, N_RET_HEADS),
        in_specs=[pl.BlockSpec(memory_space=pltpu.SMEM),
                  blk(COL_RQ), blk(COL_RK), blk(COL_RV), blk(COL_RG), tab, tab],
        out_specs=pl.BlockSpec((1, s, LANES), lambda bi, h: (bi, 0, h)),
        out_shape=jax.ShapeDtypeStruct((b, s, RET_WIDTH), BF16),
        scratch_shapes=[pltpu.VMEM((s, LANES), F32)] * 4,
        compiler_params=_cparams("arbitrary", "arbitrary"),
        name="retention",
    )(log_g, proj, proj, proj, proj, cos2, sin2)


def _rotary_tables(s):
    d = RET_HEAD_DIM
    inv = ROPE_BASE ** (-jnp.arange(0, d, 2, dtype=F32) / d)
    ang = jnp.arange(s, dtype=F32)[:, None] * inv[None, :]
    cos, sin = jnp.cos(ang), jnp.sin(ang)
    return jnp.concatenate([cos, cos], axis=-1), jnp.concatenate([-sin, sin], axis=-1)


def _outproj_kernel(att_ref, conv_ref, ret_ref, x_ref, w_ref, o_ref):
    acc = jnp.dot(att_ref[...], w_ref[pl.ds(0, ATT_WIDTH), :], preferred_element_type=F32)
    acc += jnp.dot(conv_ref[...], w_ref[pl.ds(ATT_WIDTH, CONV_WIDTH), :], preferred_element_type=F32)
    acc += jnp.dot(ret_ref[...], w_ref[pl.ds(ATT_WIDTH + CONV_WIDTH, RET_WIDTH), :],
                   preferred_element_type=F32)
    o_ref[...] = x_ref[...] + acc


def _outproj(att, conv, ret, x, w, *, tm=512):
    t, d = x.shape
    row = lambda width: pl.BlockSpec((tm, width), lambda i: (i, 0))
    return pl.pallas_call(
        _outproj_kernel,
        grid=(t // tm,),
        in_specs=[row(ATT_WIDTH), row(CONV_WIDTH), row(RET_WIDTH), row(d),
                  pl.BlockSpec((MIX_WIDTH, d), lambda i: (0, 0))],
        out_specs=row(d),
        out_shape=jax.ShapeDtypeStruct((t, d), F32),
        compiler_params=_cparams("arbitrary"),
        name="outproj",
    )(att, conv, ret, x, w)


def _top16(s, payload=None):
    n = s.shape[0]
    row = lax.broadcasted_iota(jnp.int32, s.shape, 0).astype(F32)
    vals, picks = [], []
    for _ in range(PEER_TOPK):
        m = jnp.max(s, axis=0, keepdims=True)
        idx = jnp.min(jnp.where(s == m, row, float(n)), axis=0, keepdims=True)
        hit = row == idx
        vals.append(m)
        if payload is None:
            picks.append(idx)
        else:
            picks.append(jnp.max(jnp.where(hit, payload, -1.0), axis=0, keepdims=True))
        s = jnp.where(hit, -jnp.inf, s)
    return vals, picks


PAIR_COUNTS = tuple(PEER_TOPK // (a + 1) for a in range(PEER_TOPK))
N_PAIR = sum(PAIR_COUNTS)
N_PAIR_PAD = -(-N_PAIR // SUBLANES) * SUBLANES


def _peer_topk_kernel(x_ref, g_ref, wq_ref, keys_ref, h_ref, eidx_ref, gate_ref, sv_s, si_s, cand_s, eid_s):
    h = _rms(x_ref[...], g_ref[...])
    h_ref[...] = h
    q = jnp.dot(h.astype(BF16), wq_ref[...], preferred_element_type=F32).astype(BF16)
    for grp in range(2 * PEER_HEADS):
        s = lax.dot_general(keys_ref[grp], q[:, grp * PEER_DHALF:(grp + 1) * PEER_DHALF],
                            (((1,), (1,)), ((), ())), preferred_element_type=F32)
        vals, idxs = _top16(s)
        for j in range(PEER_TOPK):
            sv_s[grp, pl.ds(j, 1), :] = vals[j]
            si_s[grp, pl.ds(j, 1), :] = idxs[j]
    tm = cand_s.shape[1]
    cand_s[pl.ds(N_PAIR_PAD - SUBLANES, SUBLANES), :] = jnp.full((SUBLANES, tm), -jnp.inf, F32)
    eid_s[pl.ds(N_PAIR_PAD - SUBLANES, SUBLANES), :] = jnp.full((SUBLANES, tm), -1.0, F32)
    for hd in range(PEER_HEADS):
        sv0, sv1 = sv_s[2 * hd], sv_s[2 * hd + 1]
        si0, si1 = si_s[2 * hd], si_s[2 * hd + 1]
        off = 0
        for a, nb in enumerate(PAIR_COUNTS):
            cand_s[pl.ds(off, nb), :] = sv0[a:a + 1, :] + sv1[0:nb, :]
            eid_s[pl.ds(off, nb), :] = si0[a:a + 1, :] * float(PEER_NKEYS) + si1[0:nb, :]
            off += nb
        vals, picks = _top16(cand_s[...], eid_s[...])
        cv = jnp.concatenate(vals, axis=0)
        e = jnp.exp(cv - vals[0])
        gate_ref[pl.ds(hd * PEER_TOPK, PEER_TOPK), :] = e / jnp.sum(e, axis=0, keepdims=True)
        eidx_ref[pl.ds(hd * PEER_TOPK, PEER_TOPK), :] = jnp.concatenate(picks, axis=0).astype(jnp.int32)


def _peer_topk(x, g, wq, keys, *, tm=256):
    t, d = x.shape
    return pl.pallas_call(
        _peer_topk_kernel,
        grid=(t // tm,),
        in_specs=[pl.BlockSpec((tm, d), lambda i: (i, 0)),
                  pl.BlockSpec((1, d), lambda i: (0, 0)),
                  pl.BlockSpec((d, 2 * PEER_HEADS * PEER_DHALF), lambda i: (0, 0)),
                  pl.BlockSpec((2 * PEER_HEADS, PEER_NKEYS, PEER_DHALF), lambda i: (0, 0, 0))],
        out_specs=[pl.BlockSpec((tm, d), lambda i: (i, 0)),
                   pl.BlockSpec((PEER_SEL, tm), lambda i: (0, i)),
                   pl.BlockSpec((PEER_SEL, tm), lambda i: (0, i))],
        out_shape=[jax.ShapeDtypeStruct((t, d), F32),
                   jax.ShapeDtypeStruct((PEER_SEL, t), jnp.int32),
                   jax.ShapeDtypeStruct((PEER_SEL, t), F32)],
        scratch_shapes=[pltpu.VMEM((2 * PEER_HEADS, PEER_TOPK, tm), F32),
                        pltpu.VMEM((2 * PEER_HEADS, PEER_TOPK, tm), F32),
                        pltpu.VMEM((N_PAIR_PAD, tm), F32),
                        pltpu.VMEM((N_PAIR_PAD, tm), F32)],
        compiler_params=_cparams("arbitrary"),
        name="peer_topk",
    )(x, g.reshape(1, d), wq, keys)


PEER_TB = 256
N_SLOT = 8
LOOKAHEAD = N_SLOT - 1
SLAB = D_MODEL // LANES
EXPERT_GROUP = 16
N_GROUP = PEER_SEL // EXPERT_GROUP
SEGMENT = 8
N_SEGMENT = PEER_SEL // SEGMENT
ISSUE_PER_SEGMENT = PEER_SEL // (2 * N_SEGMENT)


def _gelu_tanh(x):
    return 0.5 * x * (1.0 + jnp.tanh(math.sqrt(2.0 / math.pi) * (x + 0.044715 * (x * x * x))))


def _peer_expert_kernel(idx_hbm, tab_hbm, sel_ref, h_ref, gate_ref, x_ref, o_ref,
                        idx_s, gbuf, pbuf, wsc, isem, gsem):
    i = pl.program_id(0)
    icp = pltpu.make_async_copy(idx_hbm.at[i], idx_s, isem)
    icp.start()
    icp.wait()

    def issue(t, slot, j0, n):
        row = idx_s.at[pl.ds(t * PEER_SEL, PEER_SEL)]
        for j in range(j0, j0 + n):
            pltpu.make_async_copy(tab_hbm.at[row[j]], gbuf.at[slot, j],
                                  gsem.at[slot]).start(priority=j % 2)

    def wait(slot):
        pltpu.make_async_copy(tab_hbm.at[pl.ds(0, PEER_SEL)], gbuf.at[slot], gsem.at[slot]).wait()

    lane_t = lax.broadcasted_iota(jnp.int32, (1, PEER_TB), 1)
    hi_mask = jnp.uint32(0xFFFF0000)

    def load_segment(slot, seg):
        return [gbuf[slot, seg * SEGMENT + k, half * SUBLANES:(half + 1) * SUBLANES, :]
                for k in range(SEGMENT) for half in range(2)]

    def key_side(t, slot, issue_args=None):
        x3 = h_ref[t]
        x0, x1 = x3[0:SUBLANES], x3[SUBLANES:SLAB]
        raw = load_segment(slot, 0)
        parts = []
        for seg in range(N_SEGMENT):
            nxt = load_segment(slot, seg + 1) if seg + 1 < N_SEGMENT else None
            if issue_args is not None:
                issue(*issue_args, seg * ISSUE_PER_SEGMENT, ISSUE_PER_SEGMENT)
            for k in range(SEGMENT):
                parts.append(pltpu.bitcast(raw[2 * k] << 16, F32) * x0
                             + pltpu.bitcast(raw[2 * k + 1] << 16, F32) * x1)
            if len(parts) == EXPERT_GROUP:
                grp = seg * SEGMENT // EXPERT_GROUP
                pbuf[pl.ds(grp * LANES, LANES), :] = jnp.concatenate(parts, axis=0).astype(BF16)
                parts = []
            raw = nxt

    def weight_inputs(t):
        q = jnp.dot(sel_ref[...], pbuf[...], preferred_element_type=F32)
        gate = jnp.sum(jnp.where(lane_t == t, gate_ref[...], 0.0), axis=1, keepdims=True)
        return q, gate

    def store_weights(q, gate):
        act = jnp.sum(q, axis=1, keepdims=True)
        wsc[...] = jnp.broadcast_to(gate * _gelu_tanh(act), (PEER_SEL, LANES))

    def value_side(t, slot, issue_args=None):
        acc = [jnp.zeros((SUBLANES, LANES), F32) for _ in range(4)]

        def load_weights(seg):
            return [wsc[pl.ds(seg * SEGMENT + k, 1), :] for k in range(SEGMENT)]

        raw, wts = load_segment(slot, 0), load_weights(0)
        for seg in range(N_SEGMENT):
            last = seg + 1 == N_SEGMENT
            nxt, wnxt = (None, None) if last else (load_segment(slot, seg + 1), load_weights(seg + 1))
            if issue_args is not None:
                issue(*issue_args, (N_SEGMENT + seg) * ISSUE_PER_SEGMENT, ISSUE_PER_SEGMENT)
            for k in range(SEGMENT):
                v0 = pltpu.bitcast(raw[2 * k] & hi_mask, F32)
                v1 = pltpu.bitcast(raw[2 * k + 1] & hi_mask, F32)
                acc[2 * (k % 2)] = acc[2 * (k % 2)] + wts[k] * v0
                acc[2 * (k % 2) + 1] = acc[2 * (k % 2) + 1] + wts[k] * v1
            raw, wts = nxt, wnxt
        o_ref[t] = x_ref[t] + jnp.concatenate([acc[0] + acc[2], acc[1] + acc[3]], axis=0)

    def prologue_body(n, carry):
        k = n // (PEER_SEL // 2)
        j = (n % (PEER_SEL // 2)) * 2
        for b in range(2):
            pltpu.make_async_copy(tab_hbm.at[idx_s[k * PEER_SEL + j + b]], gbuf.at[k, j + b],
                                  gsem.at[k]).start(priority=b)
        return carry

    lax.fori_loop(0, LOOKAHEAD * (PEER_SEL // 2), prologue_body, 0)
    wait(0)
    key_side(0, 0)
    store_weights(*weight_inputs(0))
    wait(1)
    key_side(1, 1)

    def token_body(t, carry):
        gather = (jnp.minimum(t + LOOKAHEAD, PEER_TB - 1), (t + LOOKAHEAD) % N_SLOT)
        wait((t + 2) % N_SLOT)
        q, gate = weight_inputs(t + 1)
        key_side(t + 2, (t + 2) % N_SLOT, gather)
        value_side(t, t % N_SLOT, gather)
        store_weights(q, gate)
        return carry

    n_here = jnp.minimum(PEER_TB, pl.num_programs(0) * PEER_TB - i * PEER_TB)
    lax.fori_loop(0, n_here - 2, token_body, 0)
    value_side(PEER_TB - 2, (PEER_TB - 2) % N_SLOT)
    store_weights(*weight_inputs(PEER_TB - 1))
    value_side(PEER_TB - 1, (PEER_TB - 1) % N_SLOT)
    for k in range(LOOKAHEAD - 2):
        wait(k)


def _peer_experts(idx, tab, h, gate_t, x):
    t, d = x.shape
    n_blk = t // PEER_TB
    slabs = pl.BlockSpec((PEER_TB, SLAB, LANES), lambda i: (i, 0, 0))
    sel = (jnp.arange(PEER_SEL)[:, None] == jnp.arange(PEER_SEL * SUBLANES)[None, :] // SUBLANES).astype(BF16)
    out = pl.pallas_call(
        _peer_expert_kernel,
        grid=(n_blk,),
        in_specs=[pl.BlockSpec(memory_space=pl.ANY),
                  pl.BlockSpec(memory_space=pl.ANY),
                  pl.BlockSpec((PEER_SEL, PEER_SEL * SUBLANES), lambda i: (0, 0)),
                  slabs,
                  pl.BlockSpec((PEER_SEL, PEER_TB), lambda i: (0, i)),
                  slabs],
        out_specs=slabs,
        out_shape=jax.ShapeDtypeStruct((t, SLAB, LANES), F32),
        scratch_shapes=[pltpu.SMEM((PEER_TB * PEER_SEL,), jnp.int32),
                        pltpu.VMEM((N_SLOT, PEER_SEL, SLAB, LANES), jnp.uint32),
                        pltpu.VMEM((PEER_SEL * SUBLANES, LANES), BF16),
                        pltpu.VMEM((PEER_SEL, LANES), F32),
                        pltpu.SemaphoreType.DMA,
                        pltpu.SemaphoreType.DMA((N_SLOT,))],
        compiler_params=_cparams("arbitrary"),
        name="peer_experts",
    )(idx, tab, sel, h.reshape(t, SLAB, LANES), gate_t, x.reshape(t, SLAB, LANES))
    return out.reshape(t, d)


def _pack_expert_table(u, v):
    ub = lax.bitcast_convert_type(u.astype(BF16), jnp.uint16).astype(jnp.uint32)
    vb = lax.bitcast_convert_type(v.astype(BF16), jnp.uint16).astype(jnp.uint32)
    return (ub | (vb << 16)).reshape(u.shape[0], SLAB, LANES)


def _final_norm_kernel(x_ref, g_ref, o_ref):
    o_ref[...] = _rms(x_ref[...], g_ref[...])


def _final_norm(x, g, *, tm=512):
    t, d = x.shape
    row = pl.BlockSpec((tm, d), lambda i: (i, 0))
    return pl.pallas_call(
        _final_norm_kernel,
        grid=(t // tm,),
        in_specs=[row, pl.BlockSpec((1, d), lambda i: (0, 0))],
        out_specs=row,
        out_shape=jax.ShapeDtypeStruct((t, d), F32),
        compiler_params=_cparams("arbitrary"),
        name="final_norm",
    )(x, g.reshape(1, d))


def _trunk(x, params):
    b, s, d = x.shape
    t = b * s
    x = x.reshape(t, d)
    cos2, sin2 = _rotary_tables(s)
    for l in range(DEPTH):
        p = params[l]
        proj = _inproj(x, p["norm_mix"], p["w_in"]).reshape(b, s, IN_COLS)
        att = _attention(proj, params["band"])
        conv = _short_conv(proj, p["conv_w"])
        ret = _retention(proj, p["log_g"], cos2, sin2)
        x = _outproj(att.reshape(t, ATT_WIDTH), conv.reshape(t, CONV_WIDTH),
                     ret.reshape(t, RET_WIDTH), x, p["w_out"])
        h, eidx_t, gate_t = _peer_topk(x, p["norm_ffn"], p["peer_wq"], p["peer_keys"])
        idx = jnp.transpose(eidx_t).reshape(t // PEER_TB, PEER_TB * PEER_SEL)
        x = _peer_experts(idx, p["peer_tab"], h, gate_t, x)
    return _final_norm(x, params["final_norm"]).reshape(b, s, d)


def kernel(x_prompt, x_sample, rel_bias, final_norm, norm_mix, norm_ffn, w_in, conv_w, ret_decay,
           w_out, peer_wq, peer_keys, peer_u, peer_v):
    params = {"band": _attention_band(rel_bias), "final_norm": final_norm}
    for l in range(DEPTH):
        params[l] = {
            "norm_mix": norm_mix[l],
            "norm_ffn": norm_ffn[l],
            "w_in": w_in[l].astype(BF16),
            "conv_w": conv_w[l],
            "log_g": jax.nn.log_sigmoid(ret_decay[l].astype(F32)),
            "w_out": w_out[l].astype(BF16),
            "peer_wq": peer_wq[l].astype(BF16),
            "peer_keys": peer_keys[l].reshape(2 * PEER_HEADS, PEER_NKEYS, PEER_DHALF).astype(BF16),
            "peer_tab": _pack_expert_table(peer_u[l], peer_v[l]),
        }
    return (_trunk(x_prompt, params), _trunk(x_sample, params))
```

```python
import functools
import math

import jax
import jax.numpy as jnp
import numpy as np
from jax import lax
from jax.experimental import pallas as pl
from jax.experimental.pallas import tpu as pltpu

F32 = jnp.float32
BF16 = jnp.bfloat16

D_MODEL = 2048
DEPTH = 2
N_ATT_HEADS = 12
ATT_HEAD_DIM = 64
ATT_WIDTH = N_ATT_HEADS * ATT_HEAD_DIM
DILATED_PATTERNS = ((128, 1), (512, 4), (2048, 16))
REL_BUCKETS = 32
REL_MAX_DIST = 1024
CONV_WIDTH = 512
N_RET_HEADS = 6
RET_HEAD_DIM = 128
RET_WIDTH = N_RET_HEADS * RET_HEAD_DIM
RET_CHUNK = 128
ROPE_BASE = 10000.0
MIX_WIDTH = ATT_WIDTH + CONV_WIDTH + RET_WIDTH
IN_COLS = 3 * ATT_WIDTH + 3 * CONV_WIDTH + 4 * RET_WIDTH
PEER_HEADS = 8
PEER_NKEYS = 128
PEER_EXPERTS = PEER_NKEYS * PEER_NKEYS
PEER_DHALF = 128
PEER_TOPK = 16
PEER_SEL = PEER_HEADS * PEER_TOPK
RMS_EPS = 1e-6
GN_EPS = 1e-6
NEG_INF = -1e30

LANES = 128
SUBLANES = 8
VMEM_LIMIT_BYTES = 56 * 1024 * 1024

ATT_W = 64
ATT_TQ = 128
ATT_TK = ATT_TQ + 2 * ATT_W
ATT_UNROLL = 8

COL_AQ, COL_AK, COL_AV = 0, ATT_WIDTH // LANES, 2 * ATT_WIDTH // LANES
COL_CB = 3 * ATT_WIDTH // 256
COL_CC = COL_CB + CONV_WIDTH // 256
COL_CH = COL_CC + CONV_WIDTH // 256
COL_RQ = (3 * ATT_WIDTH + 3 * CONV_WIDTH) // LANES
COL_RK = COL_RQ + N_RET_HEADS
COL_RV = COL_RK + N_RET_HEADS
COL_RG = COL_RV + N_RET_HEADS


def _cparams(*sem):
    return pltpu.CompilerParams(dimension_semantics=sem, vmem_limit_bytes=VMEM_LIMIT_BYTES)


def _rms(x, g):
    ms = jnp.mean(x * x, axis=-1, keepdims=True)
    return x * lax.rsqrt(ms + RMS_EPS) * g


def _inproj_kernel(x_ref, g_ref, w_ref, o_ref):
    h = _rms(x_ref[...], g_ref[...]).astype(BF16)
    o_ref[...] = jnp.dot(h, w_ref[...], preferred_element_type=F32)


def _inproj(x, g, w, *, tm=512, tn=2304):
    t, d = x.shape
    n = w.shape[1]
    return pl.pallas_call(
        _inproj_kernel,
        grid=(n // tn, t // tm),
        in_specs=[pl.BlockSpec((tm, d), lambda j, i: (i, 0)),
                  pl.BlockSpec((1, d), lambda j, i: (0, 0)),
                  pl.BlockSpec((d, tn), lambda j, i: (0, j))],
        out_specs=pl.BlockSpec((tm, tn), lambda j, i: (i, j)),
        out_shape=jax.ShapeDtypeStruct((t, n), F32),
        compiler_params=_cparams("arbitrary", "arbitrary"),
        name="inproj",
    )(x, g.reshape(1, d), w)


def _attn_kernel(q_ref, k_ref, v_ref, band_ref, o_ref, kbuf, vbuf0, vbuf1, acc_ref, m_ref, l_ref):
    s_len = q_ref.shape[1]
    n_flat = s_len // ATT_TQ
    lane = lax.broadcasted_iota(jnp.int32, (1, LANES), 1)
    head0 = lane < ATT_HEAD_DIM
    kcol = lax.broadcasted_iota(jnp.int32, (1, ATT_TK), 1)
    zpad = jnp.zeros((ATT_W, LANES), BF16)
    one_b = jnp.ones((), BF16)

    for p, (_, dil) in enumerate(DILATED_PATTERNS):
        sub_len = s_len // dil
        n_chunk = sub_len // ATT_TQ
        res_rows = sub_len + 2 * ATT_W

        def rows(start, size):
            if dil == 1:
                return pl.ds(start, size)
            return pl.ds(start, size, stride=dil)

        def split(f):
            return f // n_chunk, f % n_chunk

        def pad_body(r, carry):
            for off in (0, ATT_W + sub_len):
                dst = pl.ds(pl.multiple_of(r * res_rows + off, ATT_W), ATT_W)
                kbuf[dst, :] = zpad
                vbuf0[dst, :] = zpad
                vbuf1[dst, :] = zpad
            return carry

        lax.fori_loop(0, dil, pad_body, 0)

        def stage_body(f, carry):
            r, c = split(f)
            src = rows(r + c * ATT_TQ * dil, ATT_TQ)
            dst = pl.ds(pl.multiple_of(r * res_rows + ATT_W + c * ATT_TQ, ATT_W), ATT_TQ)
            kbuf[dst, :] = k_ref[0, src, :].astype(BF16)
            v = v_ref[0, src, :].astype(BF16)
            vbuf0[dst, :] = jnp.where(head0, v, one_b)
            vbuf1[dst, :] = jnp.where(head0, one_b, v)
            return carry

        lax.fori_loop(0, n_flat, stage_body, 0)

        def chunk(f):
            r, c = split(f)
            qrows = rows(r + c * ATT_TQ * dil, ATT_TQ)
            q = q_ref[0, qrows, :] * (ATT_HEAD_DIM ** -0.5)
            win = pl.ds(pl.multiple_of(r * res_rows + c * ATT_TQ, ATT_W), ATT_TK)
            kw = kbuf[win, :]
            kpos = c * ATT_TQ - ATT_W + kcol
            pos_ok = (kpos >= 0) & (kpos < sub_len)
            pv, mm = [], []
            for h, vbuf in enumerate((vbuf0, vbuf1)):
                sel = head0 if h == 0 else jnp.logical_not(head0)
                qh = jnp.where(sel, q, 0.0).astype(BF16)
                s = lax.dot_general(qh, kw, (((1,), (1,)), ((), ())), preferred_element_type=F32)
                s = jnp.where(pos_ok, s + band_ref[p, h], NEG_INF)
                m = jnp.max(s, axis=1, keepdims=True)
                mm.append(m)
                pv.append(jnp.dot(jnp.exp(s - m).astype(BF16), vbuf[win, :], preferred_element_type=F32))
            acc_n = jnp.where(head0, pv[0], pv[1])
            m_n = jnp.where(head0, mm[0], mm[1])
            l_n = pltpu.roll(jnp.where(head0, pv[1], pv[0]), ATT_HEAD_DIM, 1)
            if p == 0:
                acc_ref[qrows, :] = acc_n
                m_ref[qrows, :] = m_n
                l_ref[qrows, :] = l_n
            else:
                m_o = m_ref[qrows, :]
                m_t = jnp.maximum(m_o, m_n)
                a_o = jnp.exp(m_o - m_t)
                a_n = jnp.exp(m_n - m_t)
                acc_ref[qrows, :] = acc_ref[qrows, :] * a_o + acc_n * a_n
                l_ref[qrows, :] = l_ref[qrows, :] * a_o + l_n * a_n
                m_ref[qrows, :] = m_t

        def chunks_body(g, carry):
            for u in range(ATT_UNROLL):
                chunk(g * ATT_UNROLL + u)
            return carry

        lax.fori_loop(0, n_flat // ATT_UNROLL, chunks_body, 0)

    def out_body(c, carry):
        rws = pl.ds(pl.multiple_of(c * ATT_TQ, ATT_TQ), ATT_TQ)
        o_ref[0, rws, :] = (acc_ref[rws, :] / l_ref[rws, :]).astype(o_ref.dtype)
        return carry

    lax.fori_loop(0, s_len // ATT_TQ, out_body, 0)


def _attention(proj, band):
    b, s, _ = proj.shape
    n_pair = N_ATT_HEADS // 2
    staged_rows = s + 2 * ATT_W * max(dil for _, dil in DILATED_PATTERNS)
    blk = lambda off: pl.BlockSpec((1, s, LANES), lambda bi, hp: (bi, 0, off + hp))
    return pl.pallas_call(
        _attn_kernel,
        grid=(b, n_pair),
        in_specs=[blk(COL_AQ), blk(COL_AK), blk(COL_AV),
                  pl.BlockSpec((len(DILATED_PATTERNS), 2, ATT_TQ, ATT_TK),
                               lambda bi, hp: (0, hp, 0, 0))],
        out_specs=pl.BlockSpec((1, s, LANES), lambda bi, hp: (bi, 0, hp)),
        out_shape=jax.ShapeDtypeStruct((b, s, ATT_WIDTH), BF16),
        scratch_shapes=[pltpu.VMEM((staged_rows, LANES), BF16)] * 3 + [
                        pltpu.VMEM((s, LANES), F32),
                        pltpu.VMEM((s, LANES), F32),
                        pltpu.VMEM((s, LANES), F32)],
        compiler_params=_cparams("arbitrary", "arbitrary"),
        name="dilated_attention",
    )(proj, proj, proj, band)


def _t5_bucket(rel):
    nb = REL_BUCKETS // 2
    ret = (rel > 0).astype(np.int32) * nb
    n = np.abs(rel)
    max_exact = nb // 2
    nf = np.maximum(n, 1).astype(np.float32)
    large = max_exact + (np.log(nf / max_exact) / math.log(REL_MAX_DIST / max_exact)
                         * (nb - max_exact)).astype(np.int32)
    large = np.minimum(large, nb - 1)
    return ret + np.where(n < max_exact, n, large)


def _attention_band(rel_bias):
    span = ATT_TQ - 1 + ATT_W
    rel = np.arange(-span, ATT_TK - ATT_W)
    out = []
    for _, dil in DILATED_PATTERNS:
        diag = jnp.where((np.abs(rel) <= ATT_W)[:, None], rel_bias[_t5_bucket(dil * rel)].astype(F32), NEG_INF)
        diag = jnp.transpose(diag)
        out.append(jnp.stack([diag[:, ATT_TQ - 1 - i:ATT_TQ - 1 - i + ATT_TK] for i in range(ATT_TQ)], axis=1))
    return jnp.stack(out, axis=0)


CONV_ROWS = 512


def _conv_kernel(cb_ref, cc_ref, ch_ref, w_ref, o_ref, zbuf):
    s_len = cb_ref.shape[1]
    width = cb_ref.shape[2]
    zero = jnp.zeros((SUBLANES, width), F32)
    zbuf[pl.ds(0, SUBLANES), :] = zero
    zbuf[pl.ds(SUBLANES + s_len, SUBLANES), :] = zero
    n_blk = s_len // CONV_ROWS

    for c in range(n_blk):
        r0 = c * CONV_ROWS
        zbuf[pl.ds(SUBLANES + r0, CONV_ROWS), :] = (
            cc_ref[0, pl.ds(r0, CONV_ROWS), :] * ch_ref[0, pl.ds(r0, CONV_ROWS), :])
    w0 = w_ref[0:1, :]
    w1 = w_ref[1:2, :]
    w2 = w_ref[2:3, :]

    for c in range(n_blk):
        r0 = c * CONV_ROWS
        y = (w0 * zbuf[pl.ds(SUBLANES - 1 + r0, CONV_ROWS), :]
             + w1 * zbuf[pl.ds(SUBLANES + r0, CONV_ROWS), :]
             + w2 * zbuf[pl.ds(SUBLANES + 1 + r0, CONV_ROWS), :])
        o_ref[0, pl.ds(r0, CONV_ROWS), :] = (cb_ref[0, pl.ds(r0, CONV_ROWS), :] * y).astype(o_ref.dtype)


def _short_conv(proj, conv_w):
    b, s, _ = proj.shape
    cw = 256
    blk = lambda off: pl.BlockSpec((1, s, cw), lambda bi, j: (bi, 0, off + j))
    return pl.pallas_call(
        _conv_kernel,
        grid=(b, CONV_WIDTH // cw),
        in_specs=[blk(COL_CB), blk(COL_CC), blk(COL_CH),
                  pl.BlockSpec((3, cw), lambda bi, j: (0, j))],
        out_specs=pl.BlockSpec((1, s, cw), lambda bi, j: (bi, 0, j)),
        out_shape=jax.ShapeDtypeStruct((b, s, CONV_WIDTH), BF16),
        scratch_shapes=[pltpu.VMEM((s + 2 * SUBLANES, cw), F32)],
        compiler_params=_cparams("arbitrary", "arbitrary"),
        name="short_conv",
    )(proj, proj, proj, conv_w)


RET_UNROLL = 4


def _ret_kernel(lg_ref, q_ref, k_ref, v_ref, g_ref, cos_ref, sin_ref, o_ref, qs, ks, os_f, os_b):
    s_len = q_ref.shape[1]
    c_len = RET_CHUNK
    n_chunk = s_len // c_len
    h = pl.program_id(1)
    lg_f = lg_ref[0, h]
    lg_b = lg_ref[1, h]

    def rot_body(n, carry):
        rws = pl.ds(pl.multiple_of(n * c_len, c_len), c_len)
        cs = cos_ref[rws, :]
        sn = sin_ref[rws, :]
        q = q_ref[0, rws, :]
        k = k_ref[0, rws, :]
        qs[rws, :] = q * cs + pltpu.roll(q, RET_HEAD_DIM // 2, 1) * sn
        ks[rws, :] = (k * cs + pltpu.roll(k, RET_HEAD_DIM // 2, 1) * sn) * (RET_HEAD_DIM ** -0.5)
        return carry

    lax.fori_loop(0, n_chunk, rot_body, 0)

    ri = lax.broadcasted_iota(jnp.int32, (c_len, c_len), 0)
    ci = lax.broadcasted_iota(jnp.int32, (c_len, c_len), 1)
    diff = (ri - ci).astype(F32)
    col = lax.broadcasted_iota(jnp.int32, (c_len, 1), 0).astype(F32)
    one = jnp.ones((1, 1), F32)

    dec_f = jnp.where(diff >= 0, jnp.exp(lg_f * jnp.maximum(diff, 0.0)), 0.0)
    kw_f = jnp.exp(lg_f * (c_len - 1 - col))
    qw_f = jnp.exp(lg_f * (col + 1))
    gc_f = jnp.exp(lg_f * c_len * one)
    dec_b = jnp.where(diff < 0, jnp.exp(lg_b * jnp.maximum(-diff, 0.0)), 0.0)
    kw_b = jnp.exp(lg_b * col)
    qw_b = jnp.exp(lg_b * (c_len - col))
    gc_b = jnp.exp(lg_b * c_len * one)

    def direction(n, state, dec, kw, qw, gc):
        rws = pl.ds(pl.multiple_of(n * c_len, c_len), c_len)
        q = qs[rws, :]
        k = ks[rws, :]
        v = v_ref[0, rws, :].astype(BF16)
        qb = q.astype(BF16)
        sc = lax.dot_general(qb, k.astype(BF16), (((1,), (1,)), ((), ())),
                             preferred_element_type=F32)
        o_in = jnp.dot((sc * dec).astype(BF16), v, preferred_element_type=F32)
        o_x = jnp.dot(qb, state.astype(BF16), preferred_element_type=F32) * qw
        kv = jnp.dot(jnp.transpose(k * kw).astype(BF16), v, preferred_element_type=F32)
        return rws, o_in + o_x, gc * state + kv

    def sweep_body(g, states):
        s_f, s_b = states
        for u in range(RET_UNROLL):
            n = g * RET_UNROLL + u
            rws, o, s_f = direction(n, s_f, dec_f, kw_f, qw_f, gc_f)
            os_f[rws, :] = o
            rws, o, s_b = direction(n_chunk - 1 - n, s_b, dec_b, kw_b, qw_b, gc_b)
            os_b[rws, :] = o
        return s_f, s_b

    zero_state = jnp.zeros((RET_HEAD_DIM, RET_HEAD_DIM), F32)
    lax.fori_loop(0, n_chunk // RET_UNROLL, sweep_body, (zero_state, zero_state))

    def norm_body(g, carry):
        for u in range(RET_UNROLL):
            rws = pl.ds(pl.multiple_of((g * RET_UNROLL + u) * c_len, c_len), c_len)
            o = os_f[rws, :] + os_b[rws, :]
            mu = jnp.mean(o, axis=-1, keepdims=True)
            oc = o - mu
            var = jnp.mean(oc * oc, axis=-1, keepdims=True)
            on = oc * lax.rsqrt(var + GN_EPS)
            gt = g_ref[0, rws, :]
            o_ref[0, rws, :] = (gt * jax.nn.sigmoid(gt) * on).astype(o_ref.dtype)
        return carry

    lax.fori_loop(0, n_chunk // RET_UNROLL, norm_body, 0)


def _retention(proj, log_g, cos2, sin2):
    b, s, _ = proj.shape
    blk = lambda off: pl.BlockSpec((1, s, LANES), lambda bi, h: (bi, 0, off + h))
    tab = pl.BlockSpec((s, LANES), lambda bi, h: (0, 0))
    return pl.pallas_call(
        _ret_kernel,
        grid=(b, N_RET_HEADS),
        in_specs=[pl.BlockSpec(memory_space=pltpu.SMEM),
                  blk(COL_RQ), blk(COL_RK), blk(COL_RV), blk(COL_RG), tab, tab],
        out_specs=pl.BlockSpec((1, s, LANES), lambda bi, h: (bi, 0, h)),
        out_shape=jax.ShapeDtypeStruct((b, s, RET_WIDTH), BF16),
        scratch_shapes=[pltpu.VMEM((s, LANES), F32)] * 4,
        compiler_params=_cparams("arbitrary", "arbitrary"),
        name="retention",
    )(log_g, proj, proj, proj, proj, cos2, sin2)


def _rotary_tables(s):
    d = RET_HEAD_DIM
    inv = ROPE_BASE ** (-jnp.arange(0, d, 2, dtype=F32) / d)
    ang = jnp.arange(s, dtype=F32)[:, None] * inv[None, :]
    cos, sin = jnp.cos(ang), jnp.sin(ang)
    return jnp.concatenate([cos, cos], axis=-1), jnp.concatenate([-sin, sin], axis=-1)


def _outproj_kernel(att_ref, conv_ref, ret_ref, x_ref, w_ref, o_ref):
    acc = jnp.dot(att_ref[...], w_ref[pl.ds(0, ATT_WIDTH), :], preferred_element_type=F32)
    acc += jnp.dot(conv_ref[...], w_ref[pl.ds(ATT_WIDTH, CONV_WIDTH), :], preferred_element_type=F32)
    acc += jnp.dot(ret_ref[...], w_ref[pl.ds(ATT_WIDTH + CONV_WIDTH, RET_WIDTH), :],
                   preferred_element_type=F32)
    o_ref[...] = x_ref[...] + acc


def _outproj(att, conv, ret, x, w, *, tm=512):
    t, d = x.shape
    row = lambda width: pl.BlockSpec((tm, width), lambda i: (i, 0))
    return pl.pallas_call(
        _outproj_kernel,
        grid=(t // tm,),
        in_specs=[row(ATT_WIDTH), row(CONV_WIDTH), row(RET_WIDTH), row(d),
                  pl.BlockSpec((MIX_WIDTH, d), lambda i: (0, 0))],
        out_specs=row(d),
        out_shape=jax.ShapeDtypeStruct((t, d), F32),
        compiler_params=_cparams("arbitrary"),
        name="outproj",
    )(att, conv, ret, x, w)


def _top16(s, payload=None):
    n = s.shape[0]
    row = lax.broadcasted_iota(jnp.int32, s.shape, 0).astype(F32)
    vals, picks = [], []
    for _ in range(PEER_TOPK):
        m = jnp.max(s, axis=0, keepdims=True)
        idx = jnp.min(jnp.where(s == m, row, float(n)), axis=0, keepdims=True)
        hit = row == idx
        vals.append(m)
        if payload is None:
            picks.append(idx)
        else:
            picks.append(jnp.max(jnp.where(hit, payload, -1.0), axis=0, keepdims=True))
        s = jnp.where(hit, -jnp.inf, s)
    return vals, picks


PAIR_COUNTS = tuple(PEER_TOPK // (a + 1) for a in range(PEER_TOPK))
N_PAIR = sum(PAIR_COUNTS)
N_PAIR_PAD = -(-N_PAIR // SUBLANES) * SUBLANES


def _peer_topk_kernel(x_ref, g_ref, wq_ref, keys_ref, h_ref, eidx_ref, gate_ref, sv_s, si_s, cand_s, eid_s):
    h = _rms(x_ref[...], g_ref[...])
    h_ref[...] = h
    q = jnp.dot(h.astype(BF16), wq_ref[...], preferred_element_type=F32).astype(BF16)
    for grp in range(2 * PEER_HEADS):
        s = lax.dot_general(keys_ref[grp], q[:, grp * PEER_DHALF:(grp + 1) * PEER_DHALF],
                            (((1,), (1,)), ((), ())), preferred_element_type=F32)
        vals, idxs = _top16(s)
        for j in range(PEER_TOPK):
            sv_s[grp, pl.ds(j, 1), :] = vals[j]
            si_s[grp, pl.ds(j, 1), :] = idxs[j]
    tm = cand_s.shape[1]
    cand_s[pl.ds(N_PAIR_PAD - SUBLANES, SUBLANES), :] = jnp.full((SUBLANES, tm), -jnp.inf, F32)
    eid_s[pl.ds(N_PAIR_PAD - SUBLANES, SUBLANES), :] = jnp.full((SUBLANES, tm), -1.0, F32)
    for hd in range(PEER_HEADS):
        sv0, sv1 = sv_s[2 * hd], sv_s[2 * hd + 1]
        si0, si1 = si_s[2 * hd], si_s[2 * hd + 1]
        off = 0
        for a, nb in enumerate(PAIR_COUNTS):
            cand_s[pl.ds(off, nb), :] = sv0[a:a + 1, :] + sv1[0:nb, :]
            eid_s[pl.ds(off, nb), :] = si0[a:a + 1, :] * float(PEER_NKEYS) + si1[0:nb, :]
            off += nb
        vals, picks = _top16(cand_s[...], eid_s[...])
        cv = jnp.concatenate(vals, axis=0)
        e = jnp.exp(cv - vals[0])
        gate_ref[pl.ds(hd * PEER_TOPK, PEER_TOPK), :] = e / jnp.sum(e, axis=0, keepdims=True)
        eidx_ref[pl.ds(hd * PEER_TOPK, PEER_TOPK), :] = jnp.concatenate(picks, axis=0).astype(jnp.int32)


def _peer_topk(x, g, wq, keys, *, tm=256):
    t, d = x.shape
    return pl.pallas_call(
        _peer_topk_kernel,
        grid=(t // tm,),
        in_specs=[pl.BlockSpec((tm, d), lambda i: (i, 0)),
                  pl.BlockSpec((1, d), lambda i: (0, 0)),
                  pl.BlockSpec((d, 2 * PEER_HEADS * PEER_DHALF), lambda i: (0, 0)),
                  pl.BlockSpec((2 * PEER_HEADS, PEER_NKEYS, PEER_DHALF), lambda i: (0, 0, 0))],
        out_specs=[pl.BlockSpec((tm, d), lambda i: (i, 0)),
                   pl.BlockSpec((PEER_SEL, tm), lambda i: (0, i)),
                   pl.BlockSpec((PEER_SEL, tm), lambda i: (0, i))],
        out_shape=[jax.ShapeDtypeStruct((t, d), F32),
                   jax.ShapeDtypeStruct((PEER_SEL, t), jnp.int32),
                   jax.ShapeDtypeStruct((PEER_SEL, t), F32)],
        scratch_shapes=[pltpu.VMEM((2 * PEER_HEADS, PEER_TOPK, tm), F32),
                        pltpu.VMEM((2 * PEER_HEADS, PEER_TOPK, tm), F32),
                        pltpu.VMEM((N_PAIR_PAD, tm), F32),
                        pltpu.VMEM((N_PAIR_PAD, tm), F32)],
        compiler_params=_cparams("arbitrary"),
        name="peer_topk",
    )(x, g.reshape(1, d), wq, keys)


PEER_TB = 512
N_SLOT = 8
LOOKAHEAD = N_SLOT - 1
SLAB = D_MODEL // LANES
EXPERT_GROUP = 16
N_GROUP = PEER_SEL // EXPERT_GROUP
SEGMENT = 8
N_SEGMENT = PEER_SEL // SEGMENT
ISSUE_PER_SEGMENT = PEER_SEL // (2 * N_SEGMENT)


def _gelu_tanh(x):
    return 0.5 * x * (1.0 + jnp.tanh(math.sqrt(2.0 / math.pi) * (x + 0.044715 * (x * x * x))))


def _peer_expert_kernel(idx_hbm, tab_hbm, sel_ref, h_ref, gate_ref, x_ref, o_ref,
                        idx_s, gbuf, pbuf, wsc, isem, gsem):
    i = pl.program_id(0)
    icp = pltpu.make_async_copy(idx_hbm.at[i], idx_s, isem)
    icp.start()
    icp.wait()

    def issue(t, slot, j0, n):
        row = idx_s.at[pl.ds(t * PEER_SEL, PEER_SEL)]
        for j in range(j0, j0 + n):
            pltpu.make_async_copy(tab_hbm.at[row[j]], gbuf.at[slot, j],
                                  gsem.at[slot]).start(priority=j % 2)

    def wait(slot):
        pltpu.make_async_copy(tab_hbm.at[pl.ds(0, PEER_SEL)], gbuf.at[slot], gsem.at[slot]).wait()

    lane_t = lax.broadcasted_iota(jnp.int32, (1, PEER_TB), 1)
    hi_mask = jnp.uint32(0xFFFF0000)

    def load_segment(slot, seg):
        return [gbuf[slot, seg * SEGMENT + k, half * SUBLANES:(half + 1) * SUBLANES, :]
                for k in range(SEGMENT) for half in range(2)]

    def key_side(t, slot, issue_args=None):
        x3 = h_ref[t]
        x0, x1 = x3[0:SUBLANES], x3[SUBLANES:SLAB]
        raw = load_segment(slot, 0)
        parts = []
        for seg in range(N_SEGMENT):
            nxt = load_segment(slot, seg + 1) if seg + 1 < N_SEGMENT else None
            if issue_args is not None:
                issue(*issue_args, seg * ISSUE_PER_SEGMENT, ISSUE_PER_SEGMENT)
            for k in range(SEGMENT):
                parts.append(pltpu.bitcast(raw[2 * k] << 16, F32) * x0
                             + pltpu.bitcast(raw[2 * k + 1] << 16, F32) * x1)
            if len(parts) == EXPERT_GROUP:
                grp = seg * SEGMENT // EXPERT_GROUP
                pbuf[pl.ds(grp * LANES, LANES), :] = jnp.concatenate(parts, axis=0).astype(BF16)
                parts = []
            raw = nxt

    def weight_inputs(t):
        q = jnp.dot(sel_ref[...], pbuf[...], preferred_element_type=F32)
        gate = jnp.sum(jnp.where(lane_t == t, gate_ref[...], 0.0), axis=1, keepdims=True)
        return q, gate

    def store_weights(q, gate):
        act = jnp.sum(q, axis=1, keepdims=True)
        wsc[...] = jnp.broadcast_to(gate * _gelu_tanh(act), (PEER_SEL, LANES))

    def value_side(t, slot, issue_args=None):
        acc = [jnp.zeros((SUBLANES, LANES), F32) for _ in range(4)]

        def load_weights(seg):
            return [wsc[pl.ds(seg * SEGMENT + k, 1), :] for k in range(SEGMENT)]

        raw, wts = load_segment(slot, 0), load_weights(0)
        for seg in range(N_SEGMENT):
            last = seg + 1 == N_SEGMENT
            nxt, wnxt = (None, None) if last else (load_segment(slot, seg + 1), load_weights(seg + 1))
            if issue_args is not None:
                issue(*issue_args, (N_SEGMENT + seg) * ISSUE_PER_SEGMENT, ISSUE_PER_SEGMENT)
            for k in range(SEGMENT):
                v0 = pltpu.bitcast(raw[2 * k] & hi_mask, F32)
                v1 = pltpu.bitcast(raw[2 * k + 1] & hi_mask, F32)
                acc[2 * (k % 2)] = acc[2 * (k % 2)] + wts[k] * v0
                acc[2 * (k % 2) + 1] = acc[2 * (k % 2) + 1] + wts[k] * v1
            raw, wts = nxt, wnxt
        o_ref[t] = x_ref[t] + jnp.concatenate([acc[0] + acc[2], acc[1] + acc[3]], axis=0)

    def prologue_body(n, carry):
        k = n // (PEER_SEL // SEGMENT)
        j = (n % (PEER_SEL // SEGMENT)) * SEGMENT
        for b in range(SEGMENT):
            pltpu.make_async_copy(tab_hbm.at[idx_s[k * PEER_SEL + j + b]], gbuf.at[k, j + b],
                                  gsem.at[k]).start(priority=b % 2)
        return carry

    lax.fori_loop(0, LOOKAHEAD * (PEER_SEL // SEGMENT), prologue_body, 0)
    wait(0)
    key_side(0, 0)
    store_weights(*weight_inputs(0))
    wait(1)
    key_side(1, 1)

    def token_body(t, carry):
        gather = (jnp.minimum(t + LOOKAHEAD, PEER_TB - 1), (t + LOOKAHEAD) % N_SLOT)
        wait((t + 2) % N_SLOT)
        q, gate = weight_inputs(t + 1)
        key_side(t + 2, (t + 2) % N_SLOT, gather)
        value_side(t, t % N_SLOT, gather)
        store_weights(q, gate)
        return carry

    n_here = jnp.minimum(PEER_TB, pl.num_programs(0) * PEER_TB - i * PEER_TB)
    lax.fori_loop(0, n_here - 2, token_body, 0)
    value_side(PEER_TB - 2, (PEER_TB - 2) % N_SLOT)
    store_weights(*weight_inputs(PEER_TB - 1))
    value_side(PEER_TB - 1, (PEER_TB - 1) % N_SLOT)
    for k in range(LOOKAHEAD - 2):
        wait(k)


def _peer_experts(idx, tab, h, gate_t, x):
    t, d = x.shape
    n_blk = t // PEER_TB
    slabs = pl.BlockSpec((PEER_TB, SLAB, LANES), lambda i: (i, 0, 0))
    sel = (jnp.arange(PEER_SEL)[:, None] == jnp.arange(PEER_SEL * SUBLANES)[None, :] // SUBLANES).astype(BF16)
    out = pl.pallas_call(
        _peer_expert_kernel,
        grid=(n_blk,),
        in_specs=[pl.BlockSpec(memory_space=pl.ANY),
                  pl.BlockSpec(memory_space=pl.ANY),
                  pl.BlockSpec((PEER_SEL, PEER_SEL * SUBLANES), lambda i: (0, 0)),
                  slabs,
                  pl.BlockSpec((PEER_SEL, PEER_TB), lambda i: (0, i)),
                  slabs],
        out_specs=slabs,
        out_shape=jax.ShapeDtypeStruct((t, SLAB, LANES), F32),
        scratch_shapes=[pltpu.SMEM((PEER_TB * PEER_SEL,), jnp.int32),
                        pltpu.VMEM((N_SLOT, PEER_SEL, SLAB, LANES), jnp.uint32),
                        pltpu.VMEM((PEER_SEL * SUBLANES, LANES), BF16),
                        pltpu.VMEM((PEER_SEL, LANES), F32),
                        pltpu.SemaphoreType.DMA,
                        pltpu.SemaphoreType.DMA((N_SLOT,))],
        compiler_params=_cparams("arbitrary"),
        name="peer_experts",
    )(idx, tab, sel, h.reshape(t, SLAB, LANES), gate_t, x.reshape(t, SLAB, LANES))
    return out.reshape(t, d)


def _pack_expert_table(u, v):
    ub = lax.bitcast_convert_type(u.astype(BF16), jnp.uint16).astype(jnp.uint32)
    vb = lax.bitcast_convert_type(v.astype(BF16), jnp.uint16).astype(jnp.uint32)
    return (ub | (vb << 16)).reshape(u.shape[0], SLAB, LANES)


def _final_norm_kernel(x_ref, g_ref, o_ref):
    o_ref[...] = _rms(x_ref[...], g_ref[...])


def _final_norm(x, g, *, tm=512):
    t, d = x.shape
    row = pl.BlockSpec((tm, d), lambda i: (i, 0))
    return pl.pallas_call(
        _final_norm_kernel,
        grid=(t // tm,),
        in_specs=[row, pl.BlockSpec((1, d), lambda i: (0, 0))],
        out_specs=row,
        out_shape=jax.ShapeDtypeStruct((t, d), F32),
        compiler_params=_cparams("arbitrary"),
        name="final_norm",
    )(x, g.reshape(1, d))


def _trunk(x, params):
    b, s, d = x.shape
    t = b * s
    x = x.reshape(t, d)
    cos2, sin2 = _rotary_tables(s)
    for l in range(DEPTH):
        p = params[l]
        proj = _inproj(x, p["norm_mix"], p["w_in"]).reshape(b, s, IN_COLS)
        att = _attention(proj, params["band"])
        conv = _short_conv(proj, p["conv_w"])
        ret = _retention(proj, p["log_g"], cos2, sin2)
        x = _outproj(att.reshape(t, ATT_WIDTH), conv.reshape(t, CONV_WIDTH),
                     ret.reshape(t, RET_WIDTH), x, p["w_out"])
        h, eidx_t, gate_t = _peer_topk(x, p["norm_ffn"], p["peer_wq"], p["peer_keys"])
        idx = jnp.transpose(eidx_t).reshape(t // PEER_TB, PEER_TB * PEER_SEL)
        x = _peer_experts(idx, p["peer_tab"], h, gate_t, x)
    return _final_norm(x, params["final_norm"]).reshape(b, s, d)


def kernel(x_prompt, x_sample, rel_bias, final_norm, norm_mix, norm_ffn, w_in, conv_w, ret_decay,
           w_out, peer_wq, peer_keys, peer_u, peer_v):
    params = {"band": _attention_band(rel_bias), "final_norm": final_norm}
    for l in range(DEPTH):
        params[l] = {
            "norm_mix": norm_mix[l],
            "norm_ffn": norm_ffn[l],
            "w_in": w_in[l].astype(BF16),
            "conv_w": conv_w[l],
            "log_g": jax.nn.log_sigmoid(ret_decay[l].astype(F32)),
            "w_out": w_out[l].astype(BF16),
            "peer_wq": peer_wq[l].astype(BF16),
            "peer_keys": peer_keys[l].reshape(2 * PEER_HEADS, PEER_NKEYS, PEER_DHALF).astype(BF16),
            "peer_tab": _pack_expert_table(peer_u[l], peer_v[l]),
        }
    return (_trunk(x_prompt, params), _trunk(x_sample, params))
```

```python
import functools
import math

import jax
import jax.numpy as jnp
import numpy as np
from jax import lax
from jax.experimental import pallas as pl
from jax.experimental.pallas import tpu as pltpu
from jax.experimental.pallas import tpu_sc as plsc

F32 = jnp.float32
BF16 = jnp.bfloat16

D_MODEL = 2048
DEPTH = 2
N_ATT_HEADS = 12
ATT_HEAD_DIM = 64
ATT_WIDTH = N_ATT_HEADS * ATT_HEAD_DIM
DILATED_PATTERNS = ((128, 1), (512, 4), (2048, 16))
REL_BUCKETS = 32
REL_MAX_DIST = 1024
CONV_WIDTH = 512
N_RET_HEADS = 6
RET_HEAD_DIM = 128
RET_WIDTH = N_RET_HEADS * RET_HEAD_DIM
RET_CHUNK = 128
ROPE_BASE = 10000.0
MIX_WIDTH = ATT_WIDTH + CONV_WIDTH + RET_WIDTH
IN_COLS = 3 * ATT_WIDTH + 3 * CONV_WIDTH + 4 * RET_WIDTH
PEER_HEADS = 8
PEER_NKEYS = 128
PEER_EXPERTS = PEER_NKEYS * PEER_NKEYS
PEER_DHALF = 128
PEER_TOPK = 16
PEER_SEL = PEER_HEADS * PEER_TOPK
RMS_EPS = 1e-6
GN_EPS = 1e-6
NEG_INF = -1e30

LANES = 128
SUBLANES = 8
VMEM_LIMIT_BYTES = 56 * 1024 * 1024

ATT_W = 64
ATT_TQ = 128
ATT_TK = ATT_TQ + 2 * ATT_W
ATT_UNROLL = 8

COL_AQ, COL_AK, COL_AV = 0, ATT_WIDTH // LANES, 2 * ATT_WIDTH // LANES
COL_CB = 3 * ATT_WIDTH // 256
COL_CC = COL_CB + CONV_WIDTH // 256
COL_CH = COL_CC + CONV_WIDTH // 256
COL_RQ = (3 * ATT_WIDTH + 3 * CONV_WIDTH) // LANES
COL_RK = COL_RQ + N_RET_HEADS
COL_RV = COL_RK + N_RET_HEADS
COL_RG = COL_RV + N_RET_HEADS


def _cparams(*sem):
    return pltpu.CompilerParams(dimension_semantics=sem, vmem_limit_bytes=VMEM_LIMIT_BYTES)


def _rms(x, g):
    ms = jnp.mean(x * x, axis=-1, keepdims=True)
    return x * lax.rsqrt(ms + RMS_EPS) * g


def _inproj_kernel(x_ref, g_ref, w_ref, o_ref):
    h = _rms(x_ref[...], g_ref[...]).astype(BF16)
    o_ref[...] = jnp.dot(h, w_ref[...], preferred_element_type=F32)


def _inproj(x, g, w, *, tm=512, tn=2304):
    t, d = x.shape
    n = w.shape[1]
    return pl.pallas_call(
        _inproj_kernel,
        grid=(n // tn, t // tm),
        in_specs=[pl.BlockSpec((tm, d), lambda j, i: (i, 0)),
                  pl.BlockSpec((1, d), lambda j, i: (0, 0)),
                  pl.BlockSpec((d, tn), lambda j, i: (0, j))],
        out_specs=pl.BlockSpec((tm, tn), lambda j, i: (i, j)),
        out_shape=jax.ShapeDtypeStruct((t, n), F32),
        compiler_params=_cparams("arbitrary", "arbitrary"),
        name="inproj",
    )(x, g.reshape(1, d), w)


def _attn_kernel(q_ref, k_ref, v_ref, band_ref, o_ref, kbuf, vbuf0, vbuf1, acc_ref, m_ref, l_ref):
    s_len = q_ref.shape[1]
    n_flat = s_len // ATT_TQ
    lane = lax.broadcasted_iota(jnp.int32, (1, LANES), 1)
    head0 = lane < ATT_HEAD_DIM
    kcol = lax.broadcasted_iota(jnp.int32, (1, ATT_TK), 1)
    zpad = jnp.zeros((ATT_W, LANES), BF16)
    one_b = jnp.ones((), BF16)

    for p, (_, dil) in enumerate(DILATED_PATTERNS):
        sub_len = s_len // dil
        n_chunk = sub_len // ATT_TQ
        res_rows = sub_len + 2 * ATT_W

        def rows(start, size):
            if dil == 1:
                return pl.ds(start, size)
            return pl.ds(start, size, stride=dil)

        def split(f):
            return f // n_chunk, f % n_chunk

        def pad_body(r, carry):
            for off in (0, ATT_W + sub_len):
                dst = pl.ds(pl.multiple_of(r * res_rows + off, ATT_W), ATT_W)
                kbuf[dst, :] = zpad
                vbuf0[dst, :] = zpad
                vbuf1[dst, :] = zpad
            return carry

        lax.fori_loop(0, dil, pad_body, 0)

        def stage_body(f, carry):
            r, c = split(f)
            src = rows(r + c * ATT_TQ * dil, ATT_TQ)
            dst = pl.ds(pl.multiple_of(r * res_rows + ATT_W + c * ATT_TQ, ATT_W), ATT_TQ)
            kbuf[dst, :] = k_ref[0, src, :].astype(BF16)
            v = v_ref[0, src, :].astype(BF16)
            vbuf0[dst, :] = jnp.where(head0, v, one_b)
            vbuf1[dst, :] = jnp.where(head0, one_b, v)
            return carry

        lax.fori_loop(0, n_flat, stage_body, 0)

        def chunk(f):
            r, c = split(f)
            qrows = rows(r + c * ATT_TQ * dil, ATT_TQ)
            q = q_ref[0, qrows, :] * (ATT_HEAD_DIM ** -0.5)
            win = pl.ds(pl.multiple_of(r * res_rows + c * ATT_TQ, ATT_W), ATT_TK)
            kw = kbuf[win, :]
            kpos = c * ATT_TQ - ATT_W + kcol
            pos_ok = (kpos >= 0) & (kpos < sub_len)
            pv, mm = [], []
            for h, vbuf in enumerate((vbuf0, vbuf1)):
                sel = head0 if h == 0 else jnp.logical_not(head0)
                qh = jnp.where(sel, q, 0.0).astype(BF16)
                s = lax.dot_general(qh, kw, (((1,), (1,)), ((), ())), preferred_element_type=F32)
                s = jnp.where(pos_ok, s + band_ref[p, h], NEG_INF)
                m = jnp.max(s, axis=1, keepdims=True)
                mm.append(m)
                pv.append(jnp.dot(jnp.exp(s - m).astype(BF16), vbuf[win, :], preferred_element_type=F32))
            acc_n = jnp.where(head0, pv[0], pv[1])
            m_n = jnp.where(head0, mm[0], mm[1])
            l_n = pltpu.roll(jnp.where(head0, pv[1], pv[0]), ATT_HEAD_DIM, 1)
            if p == 0:
                acc_ref[qrows, :] = acc_n
                m_ref[qrows, :] = m_n
                l_ref[qrows, :] = l_n
            else:
                m_o = m_ref[qrows, :]
                m_t = jnp.maximum(m_o, m_n)
                a_o = jnp.exp(m_o - m_t)
                a_n = jnp.exp(m_n - m_t)
                acc_ref[qrows, :] = acc_ref[qrows, :] * a_o + acc_n * a_n
                l_ref[qrows, :] = l_ref[qrows, :] * a_o + l_n * a_n
                m_ref[qrows, :] = m_t

        def chunks_body(g, carry):
            for u in range(ATT_UNROLL):
                chunk(g * ATT_UNROLL + u)
            return carry

        lax.fori_loop(0, n_flat // ATT_UNROLL, chunks_body, 0)

    def out_body(c, carry):
        rws = pl.ds(pl.multiple_of(c * ATT_TQ, ATT_TQ), ATT_TQ)
        o_ref[0, rws, :] = (acc_ref[rws, :] / l_ref[rws, :]).astype(o_ref.dtype)
        return carry

    lax.fori_loop(0, s_len // ATT_TQ, out_body, 0)


def _attention(proj, band):
    b, s, _ = proj.shape
    n_pair = N_ATT_HEADS // 2
    staged_rows = s + 2 * ATT_W * max(dil for _, dil in DILATED_PATTERNS)
    blk = lambda off: pl.BlockSpec((1, s, LANES), lambda bi, hp: (bi, 0, off + hp))
    return pl.pallas_call(
        _attn_kernel,
        grid=(b, n_pair),
        in_specs=[blk(COL_AQ), blk(COL_AK), blk(COL_AV),
                  pl.BlockSpec((len(DILATED_PATTERNS), 2, ATT_TQ, ATT_TK),
                               lambda bi, hp: (0, hp, 0, 0))],
        out_specs=pl.BlockSpec((1, s, LANES), lambda bi, hp: (bi, 0, hp)),
        out_shape=jax.ShapeDtypeStruct((b, s, ATT_WIDTH), BF16),
        scratch_shapes=[pltpu.VMEM((staged_rows, LANES), BF16)] * 3 + [
                        pltpu.VMEM((s, LANES), F32),
                        pltpu.VMEM((s, LANES), F32),
                        pltpu.VMEM((s, LANES), F32)],
        compiler_params=_cparams("arbitrary", "arbitrary"),
        name="dilated_attention",
    )(proj, proj, proj, band)


def _t5_bucket(rel):
    nb = REL_BUCKETS // 2
    ret = (rel > 0).astype(np.int32) * nb
    n = np.abs(rel)
    max_exact = nb // 2
    nf = np.maximum(n, 1).astype(np.float32)
    large = max_exact + (np.log(nf / max_exact) / math.log(REL_MAX_DIST / max_exact)
                         * (nb - max_exact)).astype(np.int32)
    large = np.minimum(large, nb - 1)
    return ret + np.where(n < max_exact, n, large)


def _attention_band(rel_bias):
    span = ATT_TQ - 1 + ATT_W
    rel = np.arange(-span, ATT_TK - ATT_W)
    out = []
    for _, dil in DILATED_PATTERNS:
        diag = jnp.where((np.abs(rel) <= ATT_W)[:, None], rel_bias[_t5_bucket(dil * rel)].astype(F32), NEG_INF)
        diag = jnp.transpose(diag)
        out.append(jnp.stack([diag[:, ATT_TQ - 1 - i:ATT_TQ - 1 - i + ATT_TK] for i in range(ATT_TQ)], axis=1))
    return jnp.stack(out, axis=0)


CONV_ROWS = 512


def _conv_kernel(cb_ref, cc_ref, ch_ref, w_ref, o_ref, zbuf):
    s_len = cb_ref.shape[1]
    width = cb_ref.shape[2]
    zero = jnp.zeros((SUBLANES, width), F32)
    zbuf[pl.ds(0, SUBLANES), :] = zero
    zbuf[pl.ds(SUBLANES + s_len, SUBLANES), :] = zero
    n_blk = s_len // CONV_ROWS

    for c in range(n_blk):
        r0 = c * CONV_ROWS
        zbuf[pl.ds(SUBLANES + r0, CONV_ROWS), :] = (
            cc_ref[0, pl.ds(r0, CONV_ROWS), :] * ch_ref[0, pl.ds(r0, CONV_ROWS), :])
    w0 = w_ref[0:1, :]
    w1 = w_ref[1:2, :]
    w2 = w_ref[2:3, :]

    for c in range(n_blk):
        r0 = c * CONV_ROWS
        y = (w0 * zbuf[pl.ds(SUBLANES - 1 + r0, CONV_ROWS), :]
             + w1 * zbuf[pl.ds(SUBLANES + r0, CONV_ROWS), :]
             + w2 * zbuf[pl.ds(SUBLANES + 1 + r0, CONV_ROWS), :])
        o_ref[0, pl.ds(r0, CONV_ROWS), :] = (cb_ref[0, pl.ds(r0, CONV_ROWS), :] * y).astype(o_ref.dtype)


def _short_conv(proj, conv_w):
    b, s, _ = proj.shape
    cw = 256
    blk = lambda off: pl.BlockSpec((1, s, cw), lambda bi, j: (bi, 0, off + j))
    return pl.pallas_call(
        _conv_kernel,
        grid=(b, CONV_WIDTH // cw),
        in_specs=[blk(COL_CB), blk(COL_CC), blk(COL_CH),
                  pl.BlockSpec((3, cw), lambda bi, j: (0, j))],
        out_specs=pl.BlockSpec((1, s, cw), lambda bi, j: (bi, 0, j)),
        out_shape=jax.ShapeDtypeStruct((b, s, CONV_WIDTH), BF16),
        scratch_shapes=[pltpu.VMEM((s + 2 * SUBLANES, cw), F32)],
        compiler_params=_cparams("arbitrary", "arbitrary"),
        name="short_conv",
    )(proj, proj, proj, conv_w)


RET_UNROLL = 4


def _ret_kernel(lg_ref, q_ref, k_ref, v_ref, g_ref, cos_ref, sin_ref, o_ref, qs, ks, os_f, os_b):
    s_len = q_ref.shape[1]
    c_len = RET_CHUNK
    n_chunk = s_len // c_len
    h = pl.program_id(1)
    lg_f = lg_ref[0, h]
    lg_b = lg_ref[1, h]

    def rot_body(n, carry):
        rws = pl.ds(pl.multiple_of(n * c_len, c_len), c_len)
        cs = cos_ref[rws, :]
        sn = sin_ref[rws, :]
        q = q_ref[0, rws, :]
        k = k_ref[0, rws, :]
        qs[rws, :] = q * cs + pltpu.roll(q, RET_HEAD_DIM // 2, 1) * sn
        ks[rws, :] = (k * cs + pltpu.roll(k, RET_HEAD_DIM // 2, 1) * sn) * (RET_HEAD_DIM ** -0.5)
        return carry

    lax.fori_loop(0, n_chunk, rot_body, 0)

    ri = lax.broadcasted_iota(jnp.int32, (c_len, c_len), 0)
    ci = lax.broadcasted_iota(jnp.int32, (c_len, c_len), 1)
    diff = (ri - ci).astype(F32)
    col = lax.broadcasted_iota(jnp.int32, (c_len, 1), 0).astype(F32)
    one = jnp.ones((1, 1), F32)

    dec_f = jnp.where(diff >= 0, jnp.exp(lg_f * jnp.maximum(diff, 0.0)), 0.0)
    kw_f = jnp.exp(lg_f * (c_len - 1 - col))
    qw_f = jnp.exp(lg_f * (col + 1))
    gc_f = jnp.exp(lg_f * c_len * one)
    dec_b = jnp.where(diff < 0, jnp.exp(lg_b * jnp.maximum(-diff, 0.0)), 0.0)
    kw_b = jnp.exp(lg_b * col)
    qw_b = jnp.exp(lg_b * (c_len - col))
    gc_b = jnp.exp(lg_b * c_len * one)

    def direction(n, state, dec, kw, qw, gc):
        rws = pl.ds(pl.multiple_of(n * c_len, c_len), c_len)
        q = qs[rws, :]
        k = ks[rws, :]
        v = v_ref[0, rws, :].astype(BF16)
        qb = q.astype(BF16)
        sc = lax.dot_general(qb, k.astype(BF16), (((1,), (1,)), ((), ())),
                             preferred_element_type=F32)
        o_in = jnp.dot((sc * dec).astype(BF16), v, preferred_element_type=F32)
        o_x = jnp.dot(qb, state.astype(BF16), preferred_element_type=F32) * qw
        kv = jnp.dot(jnp.transpose(k * kw).astype(BF16), v, preferred_element_type=F32)
        return rws, o_in + o_x, gc * state + kv

    def sweep_body(g, states):
        s_f, s_b = states
        for u in range(RET_UNROLL):
            n = g * RET_UNROLL + u
            rws, o, s_f = direction(n, s_f, dec_f, kw_f, qw_f, gc_f)
            os_f[rws, :] = o
            rws, o, s_b = direction(n_chunk - 1 - n, s_b, dec_b, kw_b, qw_b, gc_b)
            os_b[rws, :] = o
        return s_f, s_b

    zero_state = jnp.zeros((RET_HEAD_DIM, RET_HEAD_DIM), F32)
    lax.fori_loop(0, n_chunk // RET_UNROLL, sweep_body, (zero_state, zero_state))

    def norm_body(g, carry):
        for u in range(RET_UNROLL):
            rws = pl.ds(pl.multiple_of((g * RET_UNROLL + u) * c_len, c_len), c_len)
            o = os_f[rws, :] + os_b[rws, :]
            mu = jnp.mean(o, axis=-1, keepdims=True)
            oc = o - mu
            var = jnp.mean(oc * oc, axis=-1, keepdims=True)
            on = oc * lax.rsqrt(var + GN_EPS)
            gt = g_ref[0, rws, :]
            o_ref[0, rws, :] = (gt * jax.nn.sigmoid(gt) * on).astype(o_ref.dtype)
        return carry

    lax.fori_loop(0, n_chunk // RET_UNROLL, norm_body, 0)


def _retention(proj, log_g, cos2, sin2):
    b, s, _ = proj.shape
    blk = lambda off: pl.BlockSpec((1, s, LANES), lambda bi, h: (bi, 0, off + h))
    tab = pl.BlockSpec((s, LANES), lambda bi, h: (0, 0))
    return pl.pallas_call(
        _ret_kernel,
        grid=(b, N_RET_HEADS),
        in_specs=[pl.BlockSpec(memory_space=pltpu.SMEM),
                  blk(COL_RQ), blk(COL_RK), blk(COL_RV), blk(COL_RG), tab, tab],
        out_specs=pl.BlockSpec((1, s, LANES), lambda bi, h: (bi, 0, h)),
        out_shape=jax.ShapeDtypeStruct((b, s, RET_WIDTH), BF16),
        scratch_shapes=[pltpu.VMEM((s, LANES), F32)] * 4,
        compiler_params=_cparams("arbitrary", "arbitrary"),
        name="retention",
    )(log_g, proj, proj, proj, proj, cos2, sin2)


def _rotary_tables(s):
    d = RET_HEAD_DIM
    inv = ROPE_BASE ** (-jnp.arange(0, d, 2, dtype=F32) / d)
    ang = jnp.arange(s, dtype=F32)[:, None] * inv[None, :]
    cos, sin = jnp.cos(ang), jnp.sin(ang)
    return jnp.concatenate([cos, cos], axis=-1), jnp.concatenate([-sin, sin], axis=-1)


def _outproj_kernel(att_ref, conv_ref, ret_ref, x_ref, w_ref, o_ref):
    acc = jnp.dot(att_ref[...], w_ref[pl.ds(0, ATT_WIDTH), :], preferred_element_type=F32)
    acc += jnp.dot(conv_ref[...], w_ref[pl.ds(ATT_WIDTH, CONV_WIDTH), :], preferred_element_type=F32)
    acc += jnp.dot(ret_ref[...], w_ref[pl.ds(ATT_WIDTH + CONV_WIDTH, RET_WIDTH), :],
                   preferred_element_type=F32)
    o_ref[...] = x_ref[...] + acc


def _outproj(att, conv, ret, x, w, *, tm=512):
    t, d = x.shape
    row = lambda width: pl.BlockSpec((tm, width), lambda i: (i, 0))
    return pl.pallas_call(
        _outproj_kernel,
        grid=(t // tm,),
        in_specs=[row(ATT_WIDTH), row(CONV_WIDTH), row(RET_WIDTH), row(d),
                  pl.BlockSpec((MIX_WIDTH, d), lambda i: (0, 0))],
        out_specs=row(d),
        out_shape=jax.ShapeDtypeStruct((t, d), F32),
        compiler_params=_cparams("arbitrary"),
        name="outproj",
    )(att, conv, ret, x, w)


def _top16(s, payload=None):
    n = s.shape[0]
    row = lax.broadcasted_iota(jnp.int32, s.shape, 0).astype(F32)
    vals, picks = [], []
    for _ in range(PEER_TOPK):
        m = jnp.max(s, axis=0, keepdims=True)
        idx = jnp.min(jnp.where(s == m, row, float(n)), axis=0, keepdims=True)
        hit = row == idx
        vals.append(m)
        if payload is None:
            picks.append(idx)
        else:
            picks.append(jnp.max(jnp.where(hit, payload, -1.0), axis=0, keepdims=True))
        s = jnp.where(hit, -jnp.inf, s)
    return vals, picks


PAIR_COUNTS = tuple(PEER_TOPK // (a + 1) for a in range(PEER_TOPK))
N_PAIR = sum(PAIR_COUNTS)
N_PAIR_PAD = -(-N_PAIR // SUBLANES) * SUBLANES


def _peer_topk_kernel(x_ref, g_ref, wq_ref, keys_ref, h_ref, eidx_ref, gate_ref, sv_s, si_s, cand_s, eid_s):
    h = _rms(x_ref[...], g_ref[...])
    h_ref[...] = h
    q = jnp.dot(h.astype(BF16), wq_ref[...], preferred_element_type=F32).astype(BF16)
    for grp in range(2 * PEER_HEADS):
        s = lax.dot_general(keys_ref[grp], q[:, grp * PEER_DHALF:(grp + 1) * PEER_DHALF],
                            (((1,), (1,)), ((), ())), preferred_element_type=F32)
        vals, idxs = _top16(s)
        for j in range(PEER_TOPK):
            sv_s[grp, pl.ds(j, 1), :] = vals[j]
            si_s[grp, pl.ds(j, 1), :] = idxs[j]
    tm = cand_s.shape[1]
    cand_s[pl.ds(N_PAIR_PAD - SUBLANES, SUBLANES), :] = jnp.full((SUBLANES, tm), -jnp.inf, F32)
    eid_s[pl.ds(N_PAIR_PAD - SUBLANES, SUBLANES), :] = jnp.full((SUBLANES, tm), -1.0, F32)
    for hd in range(PEER_HEADS):
        sv0, sv1 = sv_s[2 * hd], sv_s[2 * hd + 1]
        si0, si1 = si_s[2 * hd], si_s[2 * hd + 1]
        off = 0
        for a, nb in enumerate(PAIR_COUNTS):
            cand_s[pl.ds(off, nb), :] = sv0[a:a + 1, :] + sv1[0:nb, :]
            eid_s[pl.ds(off, nb), :] = si0[a:a + 1, :] * float(PEER_NKEYS) + si1[0:nb, :]
            off += nb
        vals, picks = _top16(cand_s[...], eid_s[...])
        cv = jnp.concatenate(vals, axis=0)
        e = jnp.exp(cv - vals[0])
        gate_ref[pl.ds(hd * PEER_TOPK, PEER_TOPK), :] = e / jnp.sum(e, axis=0, keepdims=True)
        eidx_ref[pl.ds(hd * PEER_TOPK, PEER_TOPK), :] = jnp.concatenate(picks, axis=0).astype(jnp.int32)


def _peer_topk(x, g, wq, keys, *, tm=256):
    t, d = x.shape
    return pl.pallas_call(
        _peer_topk_kernel,
        grid=(t // tm,),
        in_specs=[pl.BlockSpec((tm, d), lambda i: (i, 0)),
                  pl.BlockSpec((1, d), lambda i: (0, 0)),
                  pl.BlockSpec((d, 2 * PEER_HEADS * PEER_DHALF), lambda i: (0, 0)),
                  pl.BlockSpec((2 * PEER_HEADS, PEER_NKEYS, PEER_DHALF), lambda i: (0, 0, 0))],
        out_specs=[pl.BlockSpec((tm, d), lambda i: (i, 0)),
                   pl.BlockSpec((PEER_SEL, tm), lambda i: (0, i)),
                   pl.BlockSpec((PEER_SEL, tm), lambda i: (0, i))],
        out_shape=[jax.ShapeDtypeStruct((t, d), F32),
                   jax.ShapeDtypeStruct((PEER_SEL, t), jnp.int32),
                   jax.ShapeDtypeStruct((PEER_SEL, t), F32)],
        scratch_shapes=[pltpu.VMEM((2 * PEER_HEADS, PEER_TOPK, tm), F32),
                        pltpu.VMEM((2 * PEER_HEADS, PEER_TOPK, tm), F32),
                        pltpu.VMEM((N_PAIR_PAD, tm), F32),
                        pltpu.VMEM((N_PAIR_PAD, tm), F32)],
        compiler_params=_cparams("arbitrary"),
        name="peer_topk",
    )(x, g.reshape(1, d), wq, keys)


PEER_TB = 512
N_SLOT = 8
LOOKAHEAD = N_SLOT - 1
SLAB = D_MODEL // LANES
EXPERT_GROUP = 16
N_GROUP = PEER_SEL // EXPERT_GROUP
SEGMENT = 8
N_SEGMENT = PEER_SEL // SEGMENT
ISSUE_PER_SEGMENT = PEER_SEL // (2 * N_SEGMENT)


def _gelu_tanh(x):
    return 0.5 * x * (1.0 + jnp.tanh(math.sqrt(2.0 / math.pi) * (x + 0.044715 * (x * x * x))))


def _peer_expert_kernel(idx_hbm, tab_hbm, sel_ref, h_ref, gate_ref, x_ref, o_ref,
                        idx_s, gbuf, pbuf, wsc, isem, gsem):
    i = pl.program_id(0)
    icp = pltpu.make_async_copy(idx_hbm.at[i], idx_s, isem)
    icp.start()
    icp.wait()

    def issue(t, slot, j0, n):
        row = idx_s.at[pl.ds(t * PEER_SEL, PEER_SEL)]
        for j in range(j0, j0 + n):
            pltpu.make_async_copy(tab_hbm.at[row[j]], gbuf.at[slot, j],
                                  gsem.at[slot]).start(priority=j % 2)

    def wait(slot):
        pltpu.make_async_copy(tab_hbm.at[pl.ds(0, PEER_SEL)], gbuf.at[slot], gsem.at[slot]).wait()

    lane_t = lax.broadcasted_iota(jnp.int32, (1, PEER_TB), 1)
    hi_mask = jnp.uint32(0xFFFF0000)

    def load_segment(slot, seg):
        return [gbuf[slot, seg * SEGMENT + k, half * SUBLANES:(half + 1) * SUBLANES, :]
                for k in range(SEGMENT) for half in range(2)]

    def key_side(t, slot, issue_args=None):
        x3 = h_ref[t]
        x0, x1 = x3[0:SUBLANES], x3[SUBLANES:SLAB]
        raw = load_segment(slot, 0)
        parts = []
        for seg in range(N_SEGMENT):
            nxt = load_segment(slot, seg + 1) if seg + 1 < N_SEGMENT else None
            if issue_args is not None:
                issue(*issue_args, seg * ISSUE_PER_SEGMENT, ISSUE_PER_SEGMENT)
            for k in range(SEGMENT):
                parts.append(pltpu.bitcast(raw[2 * k] << 16, F32) * x0
                             + pltpu.bitcast(raw[2 * k + 1] << 16, F32) * x1)
            if len(parts) == EXPERT_GROUP:
                grp = seg * SEGMENT // EXPERT_GROUP
                pbuf[pl.ds(grp * LANES, LANES), :] = jnp.concatenate(parts, axis=0).astype(BF16)
                parts = []
            raw = nxt

    def weight_inputs(t):
        q = jnp.dot(sel_ref[...], pbuf[...], preferred_element_type=F32)
        gate = jnp.sum(jnp.where(lane_t == t, gate_ref[...], 0.0), axis=1, keepdims=True)
        return q, gate

    def store_weights(q, gate):
        act = jnp.sum(q, axis=1, keepdims=True)
        wsc[...] = jnp.broadcast_to(gate * _gelu_tanh(act), (PEER_SEL, LANES))

    def value_side(t, slot, issue_args=None):
        acc = [jnp.zeros((SUBLANES, LANES), F32) for _ in range(4)]

        def load_weights(seg):
            return [wsc[pl.ds(seg * SEGMENT + k, 1), :] for k in range(SEGMENT)]

        raw, wts = load_segment(slot, 0), load_weights(0)
        for seg in range(N_SEGMENT):
            last = seg + 1 == N_SEGMENT
            nxt, wnxt = (None, None) if last else (load_segment(slot, seg + 1), load_weights(seg + 1))
            if issue_args is not None:
                issue(*issue_args, (N_SEGMENT + seg) * ISSUE_PER_SEGMENT, ISSUE_PER_SEGMENT)
            for k in range(SEGMENT):
                v0 = pltpu.bitcast(raw[2 * k] & hi_mask, F32)
                v1 = pltpu.bitcast(raw[2 * k + 1] & hi_mask, F32)
                acc[2 * (k % 2)] = acc[2 * (k % 2)] + wts[k] * v0
                acc[2 * (k % 2) + 1] = acc[2 * (k % 2) + 1] + wts[k] * v1
            raw, wts = nxt, wnxt
        o_ref[t] = x_ref[t] + jnp.concatenate([acc[0] + acc[2], acc[1] + acc[3]], axis=0)

    def prologue_body(n, carry):
        k = n // (PEER_SEL // SEGMENT)
        j = (n % (PEER_SEL // SEGMENT)) * SEGMENT
        for b in range(SEGMENT):
            pltpu.make_async_copy(tab_hbm.at[idx_s[k * PEER_SEL + j + b]], gbuf.at[k, j + b],
                                  gsem.at[k]).start(priority=b % 2)
        return carry

    lax.fori_loop(0, LOOKAHEAD * (PEER_SEL // SEGMENT), prologue_body, 0)
    wait(0)
    key_side(0, 0)
    store_weights(*weight_inputs(0))
    wait(1)
    key_side(1, 1)

    def token_body(t, carry):
        gather = (jnp.minimum(t + LOOKAHEAD, PEER_TB - 1), (t + LOOKAHEAD) % N_SLOT)
        wait((t + 2) % N_SLOT)
        q, gate = weight_inputs(t + 1)
        key_side(t + 2, (t + 2) % N_SLOT, gather)
        value_side(t, t % N_SLOT, gather)
        store_weights(q, gate)
        return carry

    n_here = jnp.minimum(PEER_TB, pl.num_programs(0) * PEER_TB - i * PEER_TB)
    lax.fori_loop(0, n_here - 2, token_body, 0)
    value_side(PEER_TB - 2, (PEER_TB - 2) % N_SLOT)
    store_weights(*weight_inputs(PEER_TB - 1))
    value_side(PEER_TB - 1, (PEER_TB - 1) % N_SLOT)
    for k in range(LOOKAHEAD - 2):
        wait(k)


def _peer_experts(idx, tab, h, gate_t, x):
    t, d = x.shape
    n_blk = t // PEER_TB
    slabs = pl.BlockSpec((PEER_TB, SLAB, LANES), lambda i: (i, 0, 0))
    sel = (jnp.arange(PEER_SEL)[:, None] == jnp.arange(PEER_SEL * SUBLANES)[None, :] // SUBLANES).astype(BF16)
    out = pl.pallas_call(
        _peer_expert_kernel,
        grid=(n_blk,),
        in_specs=[pl.BlockSpec(memory_space=pl.ANY),
                  pl.BlockSpec(memory_space=pl.ANY),
                  pl.BlockSpec((PEER_SEL, PEER_SEL * SUBLANES), lambda i: (0, 0)),
                  slabs,
                  pl.BlockSpec((PEER_SEL, PEER_TB), lambda i: (0, i)),
                  slabs],
        out_specs=slabs,
        out_shape=jax.ShapeDtypeStruct((t, SLAB, LANES), F32),
        scratch_shapes=[pltpu.SMEM((PEER_TB * PEER_SEL,), jnp.int32),
                        pltpu.VMEM((N_SLOT, PEER_SEL, SLAB, LANES), jnp.uint32),
                        pltpu.VMEM((PEER_SEL * SUBLANES, LANES), BF16),
                        pltpu.VMEM((PEER_SEL, LANES), F32),
                        pltpu.SemaphoreType.DMA,
                        pltpu.SemaphoreType.DMA((N_SLOT,))],
        compiler_params=_cparams("arbitrary"),
        name="peer_experts",
    )(idx, tab, sel, h.reshape(t, SLAB, LANES), gate_t, x.reshape(t, SLAB, LANES))
    return out.reshape(t, d)


def _pack_expert_table(u, v):
    ub = lax.bitcast_convert_type(u.astype(BF16), jnp.uint16).astype(jnp.uint32)
    vb = lax.bitcast_convert_type(v.astype(BF16), jnp.uint16).astype(jnp.uint32)
    return ub | (vb << 16)


SC_LANES = 16
SC_CHUNKS = D_MODEL // SC_LANES
SC_EXPERTS = 16
SC_BLOCK = 32
PEER_SC_TOKENS = 1024


def _gelu_tanh_exp(x):
    z = math.sqrt(2.0 / math.pi) * (x + 0.044715 * (x * x * x))
    return 0.5 * x * (2.0 - 2.0 / (jnp.exp(2.0 * z) + 1.0))


def _peer_experts_sc(idx, tab, h, gate, x):
    n, d = x.shape
    info = plsc.get_sparse_core_info()
    n_core, n_worker = info.num_cores, info.num_cores * info.num_subcores
    per_worker = n // n_worker
    n_chunk = PEER_SEL // SC_EXPERTS
    hi_mask = jnp.uint32(0xFFFF0000)

    def body(idx_hbm, tab_hbm, h_hbm, gate_hbm, x_hbm, o_hbm, idx_v, gate_v, h_v, out_v, rows_v, w_v, sem):
        worker = lax.axis_index("subcore") * n_core + lax.axis_index("core")

        def gather(cb):
            return pltpu.make_async_copy(tab_hbm.at[idx_v.at[pl.ds(cb * SC_EXPERTS, SC_EXPERTS)]],
                                         rows_v.at[cb % 2], sem.at[cb % 2])

        def mix_chunk(cb):
            rows = rows_v.at[cb % 2]
            for e in range(SC_EXPERTS):
                w_v[e, :] = jnp.zeros((SC_LANES,), F32)

            def key_block(blk, carry):
                base = blk * (SC_BLOCK * SC_LANES)
                hs = [h_v[pl.ds(base + j * SC_LANES, SC_LANES)] for j in range(SC_BLOCK)]
                for e in range(SC_EXPERTS):
                    a = w_v[e, :]
                    for j in range(SC_BLOCK):
                        word = rows[e, pl.ds(base + j * SC_LANES, SC_LANES)]
                        a = a + plsc.bitcast(word << 16, F32) * hs[j]
                    w_v[e, :] = a
                return carry

            lax.fori_loop(0, SC_CHUNKS // SC_BLOCK, key_block, 0)
            for e in range(SC_EXPERTS):
                act = jnp.full((SC_LANES,), jnp.sum(w_v[e, :]), F32)
                g = plsc.load_gather(gate_v, [jnp.full((SC_LANES,), cb * SC_EXPERTS + e, jnp.int32)])
                w_v[e, :] = g * _gelu_tanh_exp(act)

            def value_block(blk, carry):
                base = blk * (SC_BLOCK * SC_LANES)
                outs = [out_v[pl.ds(base + j * SC_LANES, SC_LANES)] for j in range(SC_BLOCK)]
                for e in range(SC_EXPERTS):
                    w = w_v[e, :]
                    for j in range(SC_BLOCK):
                        word = rows[e, pl.ds(base + j * SC_LANES, SC_LANES)]
                        outs[j] = outs[j] + w * plsc.bitcast(word & hi_mask, F32)
                for j in range(SC_BLOCK):
                    out_v[pl.ds(base + j * SC_LANES, SC_LANES)] = outs[j]
                return carry

            lax.fori_loop(0, SC_CHUNKS // SC_BLOCK, value_block, 0)

        @pl.loop(0, per_worker)
        def _(i):
            t = worker * per_worker + i
            pltpu.sync_copy(idx_hbm.at[t], idx_v)
            gather(0).start()
            pltpu.sync_copy(gate_hbm.at[t], gate_v)
            pltpu.sync_copy(h_hbm.at[t], h_v)
            pltpu.sync_copy(x_hbm.at[t], out_v)

            @pl.loop(0, n_chunk)
            def _(cb):
                @pl.when(cb + 1 < n_chunk)
                def _():
                    gather(cb + 1).start()

                gather(cb).wait()
                mix_chunk(cb)

            pltpu.sync_copy(out_v, o_hbm.at[t])

    return pl.kernel(
        body,
        out_type=jax.ShapeDtypeStruct((n, d), F32),
        mesh=plsc.VectorSubcoreMesh(core_axis_name="core", subcore_axis_name="subcore"),
        scratch_types=[pltpu.VMEM((PEER_SEL,), jnp.int32),
                       pltpu.VMEM((PEER_SEL,), F32),
                       pltpu.VMEM((d,), F32),
                       pltpu.VMEM((d,), F32),
                       pltpu.VMEM((2, SC_EXPERTS, d), jnp.uint32),
                       pltpu.VMEM((SC_EXPERTS, SC_LANES), F32),
                       pltpu.SemaphoreType.DMA((2,))],
        compiler_params=pltpu.CompilerParams(needs_layout_passes=False),
        name="peer_experts_sc",
    )(idx, tab, h, gate, x)


def _final_norm_kernel(x_ref, g_ref, o_ref):
    o_ref[...] = _rms(x_ref[...], g_ref[...])


def _final_norm(x, g, *, tm=512):
    t, d = x.shape
    row = pl.BlockSpec((tm, d), lambda i: (i, 0))
    return pl.pallas_call(
        _final_norm_kernel,
        grid=(t // tm,),
        in_specs=[row, pl.BlockSpec((1, d), lambda i: (0, 0))],
        out_specs=row,
        out_shape=jax.ShapeDtypeStruct((t, d), F32),
        compiler_params=_cparams("arbitrary"),
        name="final_norm",
    )(x, g.reshape(1, d))


def _trunk(x, params):
    b, s, d = x.shape
    t = b * s
    x = x.reshape(t, d)
    cos2, sin2 = _rotary_tables(s)
    for l in range(DEPTH):
        p = params[l]
        proj = _inproj(x, p["norm_mix"], p["w_in"]).reshape(b, s, IN_COLS)
        att = _attention(proj, params["band"])
        conv = _short_conv(proj, p["conv_w"])
        ret = _retention(proj, p["log_g"], cos2, sin2)
        x = _outproj(att.reshape(t, ATT_WIDTH), conv.reshape(t, CONV_WIDTH),
                     ret.reshape(t, RET_WIDTH), x, p["w_out"])
        h, eidx_t, gate_t = _peer_topk(x, p["norm_ffn"], p["peer_wq"], p["peer_keys"])
        t_tc = t - PEER_SC_TOKENS
        idx = jnp.transpose(eidx_t)
        x_tc = _peer_experts(idx[:t_tc].reshape(t_tc // PEER_TB, PEER_TB * PEER_SEL),
                             p["peer_tab"].reshape(PEER_EXPERTS, SLAB, LANES),
                             h[:t_tc], gate_t[:, :t_tc], x[:t_tc])
        x_sc = _peer_experts_sc(idx[t_tc:], p["peer_tab"], h[t_tc:], jnp.transpose(gate_t[:, t_tc:]),
                                x[t_tc:])
        x = jnp.concatenate([x_tc, x_sc], axis=0)
    return _final_norm(x, params["final_norm"]).reshape(b, s, d)


def kernel(x_prompt, x_sample, rel_bias, final_norm, norm_mix, norm_ffn, w_in, conv_w, ret_decay,
           w_out, peer_wq, peer_keys, peer_u, peer_v):
    params = {"band": _attention_band(rel_bias), "final_norm": final_norm}
    for l in range(DEPTH):
        params[l] = {
            "norm_mix": norm_mix[l],
            "norm_ffn": norm_ffn[l],
            "w_in": w_in[l].astype(BF16),
            "conv_w": conv_w[l],
            "log_g": jax.nn.log_sigmoid(ret_decay[l].astype(F32)),
            "w_out": w_out[l].astype(BF16),
            "peer_wq": peer_wq[l].astype(BF16),
            "peer_keys": peer_keys[l].reshape(2 * PEER_HEADS, PEER_NKEYS, PEER_DHALF).astype(BF16),
            "peer_tab": _pack_expert_table(peer_u[l], peer_v[l]),
        }
    return (_trunk(x_prompt, params), _trunk(x_sample, params))
```

```python
import functools
import math

import jax
import jax.numpy as jnp
import numpy as np
from jax import lax
from jax.experimental import pallas as pl
from jax.experimental.pallas import tpu as pltpu
from jax.experimental.pallas import tpu_sc as plsc

F32 = jnp.float32
BF16 = jnp.bfloat16

D_MODEL = 2048
DEPTH = 2
N_ATT_HEADS = 12
ATT_HEAD_DIM = 64
ATT_WIDTH = N_ATT_HEADS * ATT_HEAD_DIM
DILATED_PATTERNS = ((128, 1), (512, 4), (2048, 16))
REL_BUCKETS = 32
REL_MAX_DIST = 1024
CONV_WIDTH = 512
N_RET_HEADS = 6
RET_HEAD_DIM = 128
RET_WIDTH = N_RET_HEADS * RET_HEAD_DIM
RET_CHUNK = 128
ROPE_BASE = 10000.0
MIX_WIDTH = ATT_WIDTH + CONV_WIDTH + RET_WIDTH
IN_COLS = 3 * ATT_WIDTH + 3 * CONV_WIDTH + 4 * RET_WIDTH
PEER_HEADS = 8
PEER_NKEYS = 128
PEER_EXPERTS = PEER_NKEYS * PEER_NKEYS
PEER_DHALF = 128
PEER_TOPK = 16
PEER_SEL = PEER_HEADS * PEER_TOPK
RMS_EPS = 1e-6
GN_EPS = 1e-6
NEG_INF = -1e30

LANES = 128
SUBLANES = 8
VMEM_LIMIT_BYTES = 56 * 1024 * 1024

ATT_W = 64
ATT_TQ = 128
ATT_TK = ATT_TQ + 2 * ATT_W
ATT_UNROLL = 8

COL_AQ, COL_AK, COL_AV = 0, ATT_WIDTH // LANES, 2 * ATT_WIDTH // LANES
COL_CB = 3 * ATT_WIDTH // 256
COL_CC = COL_CB + CONV_WIDTH // 256
COL_CH = COL_CC + CONV_WIDTH // 256
COL_RQ = (3 * ATT_WIDTH + 3 * CONV_WIDTH) // LANES
COL_RK = COL_RQ + N_RET_HEADS
COL_RV = COL_RK + N_RET_HEADS
COL_RG = COL_RV + N_RET_HEADS


def _cparams(*sem):
    return pltpu.CompilerParams(dimension_semantics=sem, vmem_limit_bytes=VMEM_LIMIT_BYTES)


def _rms(x, g):
    ms = jnp.mean(x * x, axis=-1, keepdims=True)
    return x * lax.rsqrt(ms + RMS_EPS) * g


def _inproj_kernel(x_ref, g_ref, w_ref, o_ref):
    h = _rms(x_ref[...], g_ref[...]).astype(BF16)
    o_ref[...] = jnp.dot(h, w_ref[...], preferred_element_type=F32)


def _inproj(x, g, w, *, tm=512, tn=2304):
    t, d = x.shape
    n = w.shape[1]
    return pl.pallas_call(
        _inproj_kernel,
        grid=(n // tn, t // tm),
        in_specs=[pl.BlockSpec((tm, d), lambda j, i: (i, 0)),
                  pl.BlockSpec((1, d), lambda j, i: (0, 0)),
                  pl.BlockSpec((d, tn), lambda j, i: (0, j))],
        out_specs=pl.BlockSpec((tm, tn), lambda j, i: (i, j)),
        out_shape=jax.ShapeDtypeStruct((t, n), F32),
        compiler_params=_cparams("arbitrary", "arbitrary"),
        name="inproj",
    )(x, g.reshape(1, d), w)


def _attn_kernel(q_ref, k_ref, v_ref, band_ref, o_ref, kbuf, vbuf0, vbuf1, acc_ref, m_ref, l_ref):
    s_len = q_ref.shape[1]
    n_flat = s_len // ATT_TQ
    lane = lax.broadcasted_iota(jnp.int32, (1, LANES), 1)
    head0 = lane < ATT_HEAD_DIM
    kcol = lax.broadcasted_iota(jnp.int32, (1, ATT_TK), 1)
    zpad = jnp.zeros((ATT_W, LANES), BF16)
    one_b = jnp.ones((), BF16)

    for p, (_, dil) in enumerate(DILATED_PATTERNS):
        sub_len = s_len // dil
        n_chunk = sub_len // ATT_TQ
        res_rows = sub_len + 2 * ATT_W

        def rows(start, size):
            if dil == 1:
                return pl.ds(start, size)
            return pl.ds(start, size, stride=dil)

        def split(f):
            return f // n_chunk, f % n_chunk

        def pad_body(r, carry):
            for off in (0, ATT_W + sub_len):
                dst = pl.ds(pl.multiple_of(r * res_rows + off, ATT_W), ATT_W)
                kbuf[dst, :] = zpad
                vbuf0[dst, :] = zpad
                vbuf1[dst, :] = zpad
            return carry

        lax.fori_loop(0, dil, pad_body, 0)

        def stage_body(f, carry):
            r, c = split(f)
            src = rows(r + c * ATT_TQ * dil, ATT_TQ)
            dst = pl.ds(pl.multiple_of(r * res_rows + ATT_W + c * ATT_TQ, ATT_W), ATT_TQ)
            kbuf[dst, :] = k_ref[0, src, :].astype(BF16)
            v = v_ref[0, src, :].astype(BF16)
            vbuf0[dst, :] = jnp.where(head0, v, one_b)
            vbuf1[dst, :] = jnp.where(head0, one_b, v)
            return carry

        lax.fori_loop(0, n_flat, stage_body, 0)

        def chunk(f):
            r, c = split(f)
            qrows = rows(r + c * ATT_TQ * dil, ATT_TQ)
            q = q_ref[0, qrows, :] * (ATT_HEAD_DIM ** -0.5)
            win = pl.ds(pl.multiple_of(r * res_rows + c * ATT_TQ, ATT_W), ATT_TK)
            kw = kbuf[win, :]
            kpos = c * ATT_TQ - ATT_W + kcol
            pos_ok = (kpos >= 0) & (kpos < sub_len)
            pv, mm = [], []
            for h, vbuf in enumerate((vbuf0, vbuf1)):
                sel = head0 if h == 0 else jnp.logical_not(head0)
                qh = jnp.where(sel, q, 0.0).astype(BF16)
                s = lax.dot_general(qh, kw, (((1,), (1,)), ((), ())), preferred_element_type=F32)
                s = jnp.where(pos_ok, s + band_ref[p, h], NEG_INF)
                m = jnp.max(s, axis=1, keepdims=True)
                mm.append(m)
                pv.append(jnp.dot(jnp.exp(s - m).astype(BF16), vbuf[win, :], preferred_element_type=F32))
            acc_n = jnp.where(head0, pv[0], pv[1])
            m_n = jnp.where(head0, mm[0], mm[1])
            l_n = pltpu.roll(jnp.where(head0, pv[1], pv[0]), ATT_HEAD_DIM, 1)
            if p == 0:
                acc_ref[qrows, :] = acc_n
                m_ref[qrows, :] = m_n
                l_ref[qrows, :] = l_n
            else:
                m_o = m_ref[qrows, :]
                m_t = jnp.maximum(m_o, m_n)
                a_o = jnp.exp(m_o - m_t)
                a_n = jnp.exp(m_n - m_t)
                acc_ref[qrows, :] = acc_ref[qrows, :] * a_o + acc_n * a_n
                l_ref[qrows, :] = l_ref[qrows, :] * a_o + l_n * a_n
                m_ref[qrows, :] = m_t

        def chunks_body(g, carry):
            for u in range(ATT_UNROLL):
                chunk(g * ATT_UNROLL + u)
            return carry

        lax.fori_loop(0, n_flat // ATT_UNROLL, chunks_body, 0)

    def out_body(c, carry):
        rws = pl.ds(pl.multiple_of(c * ATT_TQ, ATT_TQ), ATT_TQ)
        o_ref[0, rws, :] = (acc_ref[rws, :] / l_ref[rws, :]).astype(o_ref.dtype)
        return carry

    lax.fori_loop(0, s_len // ATT_TQ, out_body, 0)


def _attention(proj, band):
    b, s, _ = proj.shape
    n_pair = N_ATT_HEADS // 2
    staged_rows = s + 2 * ATT_W * max(dil for _, dil in DILATED_PATTERNS)
    blk = lambda off: pl.BlockSpec((1, s, LANES), lambda bi, hp: (bi, 0, off + hp))
    return pl.pallas_call(
        _attn_kernel,
        grid=(b, n_pair),
        in_specs=[blk(COL_AQ), blk(COL_AK), blk(COL_AV),
                  pl.BlockSpec((len(DILATED_PATTERNS), 2, ATT_TQ, ATT_TK),
                               lambda bi, hp: (0, hp, 0, 0))],
        out_specs=pl.BlockSpec((1, s, LANES), lambda bi, hp: (bi, 0, hp)),
        out_shape=jax.ShapeDtypeStruct((b, s, ATT_WIDTH), BF16),
        scratch_shapes=[pltpu.VMEM((staged_rows, LANES), BF16)] * 3 + [
                        pltpu.VMEM((s, LANES), F32),
                        pltpu.VMEM((s, LANES), F32),
                        pltpu.VMEM((s, LANES), F32)],
        compiler_params=_cparams("arbitrary", "arbitrary"),
        name="dilated_attention",
    )(proj, proj, proj, band)


def _t5_bucket(rel):
    nb = REL_BUCKETS // 2
    ret = (rel > 0).astype(np.int32) * nb
    n = np.abs(rel)
    max_exact = nb // 2
    nf = np.maximum(n, 1).astype(np.float32)
    large = max_exact + (np.log(nf / max_exact) / math.log(REL_MAX_DIST / max_exact)
                         * (nb - max_exact)).astype(np.int32)
    large = np.minimum(large, nb - 1)
    return ret + np.where(n < max_exact, n, large)


def _attention_band(rel_bias):
    span = ATT_TQ - 1 + ATT_W
    rel = np.arange(-span, ATT_TK - ATT_W)
    out = []
    for _, dil in DILATED_PATTERNS:
        diag = jnp.where((np.abs(rel) <= ATT_W)[:, None], rel_bias[_t5_bucket(dil * rel)].astype(F32), NEG_INF)
        diag = jnp.transpose(diag)
        out.append(jnp.stack([diag[:, ATT_TQ - 1 - i:ATT_TQ - 1 - i + ATT_TK] for i in range(ATT_TQ)], axis=1))
    return jnp.stack(out, axis=0)


CONV_ROWS = 512


def _conv_kernel(cb_ref, cc_ref, ch_ref, w_ref, o_ref, zbuf):
    s_len = cb_ref.shape[1]
    width = cb_ref.shape[2]
    zero = jnp.zeros((SUBLANES, width), F32)
    zbuf[pl.ds(0, SUBLANES), :] = zero
    zbuf[pl.ds(SUBLANES + s_len, SUBLANES), :] = zero
    n_blk = s_len // CONV_ROWS

    for c in range(n_blk):
        r0 = c * CONV_ROWS
        zbuf[pl.ds(SUBLANES + r0, CONV_ROWS), :] = (
            cc_ref[0, pl.ds(r0, CONV_ROWS), :] * ch_ref[0, pl.ds(r0, CONV_ROWS), :])
    w0 = w_ref[0:1, :]
    w1 = w_ref[1:2, :]
    w2 = w_ref[2:3, :]

    for c in range(n_blk):
        r0 = c * CONV_ROWS
        y = (w0 * zbuf[pl.ds(SUBLANES - 1 + r0, CONV_ROWS), :]
             + w1 * zbuf[pl.ds(SUBLANES + r0, CONV_ROWS), :]
             + w2 * zbuf[pl.ds(SUBLANES + 1 + r0, CONV_ROWS), :])
        o_ref[0, pl.ds(r0, CONV_ROWS), :] = (cb_ref[0, pl.ds(r0, CONV_ROWS), :] * y).astype(o_ref.dtype)


def _short_conv(proj, conv_w):
    b, s, _ = proj.shape
    cw = 256
    blk = lambda off: pl.BlockSpec((1, s, cw), lambda bi, j: (bi, 0, off + j))
    return pl.pallas_call(
        _conv_kernel,
        grid=(b, CONV_WIDTH // cw),
        in_specs=[blk(COL_CB), blk(COL_CC), blk(COL_CH),
                  pl.BlockSpec((3, cw), lambda bi, j: (0, j))],
        out_specs=pl.BlockSpec((1, s, cw), lambda bi, j: (bi, 0, j)),
        out_shape=jax.ShapeDtypeStruct((b, s, CONV_WIDTH), BF16),
        scratch_shapes=[pltpu.VMEM((s + 2 * SUBLANES, cw), F32)],
        compiler_params=_cparams("arbitrary", "arbitrary"),
        name="short_conv",
    )(proj, proj, proj, conv_w)


RET_UNROLL = 4


def _ret_kernel(lg_ref, q_ref, k_ref, v_ref, g_ref, cos_ref, sin_ref, o_ref, qs, ks, os_f, os_b):
    s_len = q_ref.shape[1]
    c_len = RET_CHUNK
    n_chunk = s_len // c_len
    h = pl.program_id(1)
    lg_f = lg_ref[0, h]
    lg_b = lg_ref[1, h]

    def rot_body(n, carry):
        rws = pl.ds(pl.multiple_of(n * c_len, c_len), c_len)
        cs = cos_ref[rws, :]
        sn = sin_ref[rws, :]
        q = q_ref[0, rws, :]
        k = k_ref[0, rws, :]
        qs[rws, :] = q * cs + pltpu.roll(q, RET_HEAD_DIM // 2, 1) * sn
        ks[rws, :] = (k * cs + pltpu.roll(k, RET_HEAD_DIM // 2, 1) * sn) * (RET_HEAD_DIM ** -0.5)
        return carry

    lax.fori_loop(0, n_chunk, rot_body, 0)

    ri = lax.broadcasted_iota(jnp.int32, (c_len, c_len), 0)
    ci = lax.broadcasted_iota(jnp.int32, (c_len, c_len), 1)
    diff = (ri - ci).astype(F32)
    col = lax.broadcasted_iota(jnp.int32, (c_len, 1), 0).astype(F32)
    one = jnp.ones((1, 1), F32)

    dec_f = jnp.where(diff >= 0, jnp.exp(lg_f * jnp.maximum(diff, 0.0)), 0.0)
    kw_f = jnp.exp(lg_f * (c_len - 1 - col))
    qw_f = jnp.exp(lg_f * (col + 1))
    gc_f = jnp.exp(lg_f * c_len * one)
    dec_b = jnp.where(diff < 0, jnp.exp(lg_b * jnp.maximum(-diff, 0.0)), 0.0)
    kw_b = jnp.exp(lg_b * col)
    qw_b = jnp.exp(lg_b * (c_len - col))
    gc_b = jnp.exp(lg_b * c_len * one)

    def direction(n, state, dec, kw, qw, gc):
        rws = pl.ds(pl.multiple_of(n * c_len, c_len), c_len)
        q = qs[rws, :]
        k = ks[rws, :]
        v = v_ref[0, rws, :].astype(BF16)
        qb = q.astype(BF16)
        sc = lax.dot_general(qb, k.astype(BF16), (((1,), (1,)), ((), ())),
                             preferred_element_type=F32)
        o_in = jnp.dot((sc * dec).astype(BF16), v, preferred_element_type=F32)
        o_x = jnp.dot(qb, state.astype(BF16), preferred_element_type=F32) * qw
        kv = jnp.dot(jnp.transpose(k * kw).astype(BF16), v, preferred_element_type=F32)
        return rws, o_in + o_x, gc * state + kv

    def sweep_body(g, states):
        s_f, s_b = states
        for u in range(RET_UNROLL):
            n = g * RET_UNROLL + u
            rws, o, s_f = direction(n, s_f, dec_f, kw_f, qw_f, gc_f)
            os_f[rws, :] = o
            rws, o, s_b = direction(n_chunk - 1 - n, s_b, dec_b, kw_b, qw_b, gc_b)
            os_b[rws, :] = o
        return s_f, s_b

    zero_state = jnp.zeros((RET_HEAD_DIM, RET_HEAD_DIM), F32)
    lax.fori_loop(0, n_chunk // RET_UNROLL, sweep_body, (zero_state, zero_state))

    def norm_body(g, carry):
        for u in range(RET_UNROLL):
            rws = pl.ds(pl.multiple_of((g * RET_UNROLL + u) * c_len, c_len), c_len)
            o = os_f[rws, :] + os_b[rws, :]
            mu = jnp.mean(o, axis=-1, keepdims=True)
            oc = o - mu
            var = jnp.mean(oc * oc, axis=-1, keepdims=True)
            on = oc * lax.rsqrt(var + GN_EPS)
            gt = g_ref[0, rws, :]
            o_ref[0, rws, :] = (gt * jax.nn.sigmoid(gt) * on).astype(o_ref.dtype)
        return carry

    lax.fori_loop(0, n_chunk // RET_UNROLL, norm_body, 0)


def _retention(proj, log_g, cos2, sin2):
    b, s, _ = proj.shape
    blk = lambda off: pl.BlockSpec((1, s, LANES), lambda bi, h: (bi, 0, off + h))
    tab = pl.BlockSpec((s, LANES), lambda bi, h: (0, 0))
    return pl.pallas_call(
        _ret_kernel,
        grid=(b, N_RET_HEADS),
        in_specs=[pl.BlockSpec(memory_space=pltpu.SMEM),
                  blk(COL_RQ), blk(COL_RK), blk(COL_RV), blk(COL_RG), tab, tab],
        out_specs=pl.BlockSpec((1, s, LANES), lambda bi, h: (bi, 0, h)),
        out_shape=jax.ShapeDtypeStruct((b, s, RET_WIDTH), BF16),
        scratch_shapes=[pltpu.VMEM((s, LANES), F32)] * 4,
        compiler_params=_cparams("arbitrary", "arbitrary"),
        name="retention",
    )(log_g, proj, proj, proj, proj, cos2, sin2)


def _rotary_tables(s):
    d = RET_HEAD_DIM
    inv = ROPE_BASE ** (-jnp.arange(0, d, 2, dtype=F32) / d)
    ang = jnp.arange(s, dtype=F32)[:, None] * inv[None, :]
    cos, sin = jnp.cos(ang), jnp.sin(ang)
    return jnp.concatenate([cos, cos], axis=-1), jnp.concatenate([-sin, sin], axis=-1)


def _outproj_kernel(att_ref, conv_ref, ret_ref, x_ref, w_ref, o_ref):
    acc = jnp.dot(att_ref[...], w_ref[pl.ds(0, ATT_WIDTH), :], preferred_element_type=F32)
    acc += jnp.dot(conv_ref[...], w_ref[pl.ds(ATT_WIDTH, CONV_WIDTH), :], preferred_element_type=F32)
    acc += jnp.dot(ret_ref[...], w_ref[pl.ds(ATT_WIDTH + CONV_WIDTH, RET_WIDTH), :],
                   preferred_element_type=F32)
    o_ref[...] = x_ref[...] + acc


def _outproj(att, conv, ret, x, w, *, tm=512):
    t, d = x.shape
    row = lambda width: pl.BlockSpec((tm, width), lambda i: (i, 0))
    return pl.pallas_call(
        _outproj_kernel,
        grid=(t // tm,),
        in_specs=[row(ATT_WIDTH), row(CONV_WIDTH), row(RET_WIDTH), row(d),
                  pl.BlockSpec((MIX_WIDTH, d), lambda i: (0, 0))],
        out_specs=row(d),
        out_shape=jax.ShapeDtypeStruct((t, d), F32),
        compiler_params=_cparams("arbitrary"),
        name="outproj",
    )(att, conv, ret, x, w)


def _top16(s, payload=None):
    n = s.shape[0]
    row = lax.broadcasted_iota(jnp.int32, s.shape, 0).astype(F32)
    vals, picks = [], []
    for _ in range(PEER_TOPK):
        m = jnp.max(s, axis=0, keepdims=True)
        idx = jnp.min(jnp.where(s == m, row, float(n)), axis=0, keepdims=True)
        hit = row == idx
        vals.append(m)
        if payload is None:
            picks.append(idx)
        else:
            picks.append(jnp.max(jnp.where(hit, payload, -1.0), axis=0, keepdims=True))
        s = jnp.where(hit, -jnp.inf, s)
    return vals, picks


PAIR_COUNTS = tuple(PEER_TOPK // (a + 1) for a in range(PEER_TOPK))
N_PAIR = sum(PAIR_COUNTS)
N_PAIR_PAD = -(-N_PAIR // SUBLANES) * SUBLANES


def _peer_topk_kernel(x_ref, g_ref, wq_ref, keys_ref, h_ref, eidx_ref, gate_ref, sv_s, si_s, cand_s, eid_s):
    h = _rms(x_ref[...], g_ref[...])
    h_ref[...] = h
    q = jnp.dot(h.astype(BF16), wq_ref[...], preferred_element_type=F32).astype(BF16)
    for grp in range(2 * PEER_HEADS):
        s = lax.dot_general(keys_ref[grp], q[:, grp * PEER_DHALF:(grp + 1) * PEER_DHALF],
                            (((1,), (1,)), ((), ())), preferred_element_type=F32)
        vals, idxs = _top16(s)
        for j in range(PEER_TOPK):
            sv_s[grp, pl.ds(j, 1), :] = vals[j]
            si_s[grp, pl.ds(j, 1), :] = idxs[j]
    tm = cand_s.shape[1]
    cand_s[pl.ds(N_PAIR_PAD - SUBLANES, SUBLANES), :] = jnp.full((SUBLANES, tm), -jnp.inf, F32)
    eid_s[pl.ds(N_PAIR_PAD - SUBLANES, SUBLANES), :] = jnp.full((SUBLANES, tm), -1.0, F32)
    for hd in range(PEER_HEADS):
        sv0, sv1 = sv_s[2 * hd], sv_s[2 * hd + 1]
        si0, si1 = si_s[2 * hd], si_s[2 * hd + 1]
        off = 0
        for a, nb in enumerate(PAIR_COUNTS):
            cand_s[pl.ds(off, nb), :] = sv0[a:a + 1, :] + sv1[0:nb, :]
            eid_s[pl.ds(off, nb), :] = si0[a:a + 1, :] * float(PEER_NKEYS) + si1[0:nb, :]
            off += nb
        vals, picks = _top16(cand_s[...], eid_s[...])
        cv = jnp.concatenate(vals, axis=0)
        e = jnp.exp(cv - vals[0])
        gate_ref[pl.ds(hd * PEER_TOPK, PEER_TOPK), :] = e / jnp.sum(e, axis=0, keepdims=True)
        eidx_ref[pl.ds(hd * PEER_TOPK, PEER_TOPK), :] = jnp.concatenate(picks, axis=0).astype(jnp.int32)


def _peer_topk(x, g, wq, keys, *, tm=256):
    t, d = x.shape
    return pl.pallas_call(
        _peer_topk_kernel,
        grid=(t // tm,),
        in_specs=[pl.BlockSpec((tm, d), lambda i: (i, 0)),
                  pl.BlockSpec((1, d), lambda i: (0, 0)),
                  pl.BlockSpec((d, 2 * PEER_HEADS * PEER_DHALF), lambda i: (0, 0)),
                  pl.BlockSpec((2 * PEER_HEADS, PEER_NKEYS, PEER_DHALF), lambda i: (0, 0, 0))],
        out_specs=[pl.BlockSpec((tm, d), lambda i: (i, 0)),
                   pl.BlockSpec((PEER_SEL, tm), lambda i: (0, i)),
                   pl.BlockSpec((PEER_SEL, tm), lambda i: (0, i))],
        out_shape=[jax.ShapeDtypeStruct((t, d), F32),
                   jax.ShapeDtypeStruct((PEER_SEL, t), jnp.int32),
                   jax.ShapeDtypeStruct((PEER_SEL, t), F32)],
        scratch_shapes=[pltpu.VMEM((2 * PEER_HEADS, PEER_TOPK, tm), F32),
                        pltpu.VMEM((2 * PEER_HEADS, PEER_TOPK, tm), F32),
                        pltpu.VMEM((N_PAIR_PAD, tm), F32),
                        pltpu.VMEM((N_PAIR_PAD, tm), F32)],
        compiler_params=_cparams("arbitrary"),
        name="peer_topk",
    )(x, g.reshape(1, d), wq, keys)


PEER_TB = 512
N_SLOT = 8
LOOKAHEAD = N_SLOT - 1
SLAB = D_MODEL // LANES
EXPERT_GROUP = 16
N_GROUP = PEER_SEL // EXPERT_GROUP
SEGMENT = 8
N_SEGMENT = PEER_SEL // SEGMENT
ISSUE_PER_SEGMENT = PEER_SEL // (2 * N_SEGMENT)


def _gelu_tanh(x):
    return 0.5 * x * (1.0 + jnp.tanh(math.sqrt(2.0 / math.pi) * (x + 0.044715 * (x * x * x))))


def _peer_expert_kernel(idx_hbm, tab_hbm, sel_ref, h_ref, gate_ref, x_ref, o_ref,
                        idx_s, gbuf, pbuf, wsc, isem, gsem):
    i = pl.program_id(0)
    icp = pltpu.make_async_copy(idx_hbm.at[i], idx_s, isem)
    icp.start()
    icp.wait()

    def issue(t, slot, j0, n):
        row = idx_s.at[pl.ds(t * PEER_SEL, PEER_SEL)]
        for j in range(j0, j0 + n):
            pltpu.make_async_copy(tab_hbm.at[row[j]], gbuf.at[slot, j],
                                  gsem.at[slot]).start(priority=j % 2)

    def wait(slot):
        pltpu.make_async_copy(tab_hbm.at[pl.ds(0, PEER_SEL)], gbuf.at[slot], gsem.at[slot]).wait()

    lane_t = lax.broadcasted_iota(jnp.int32, (1, PEER_TB), 1)
    hi_mask = jnp.uint32(0xFFFF0000)

    def load_segment(slot, seg):
        return [gbuf[slot, seg * SEGMENT + k, half * SUBLANES:(half + 1) * SUBLANES, :]
                for k in range(SEGMENT) for half in range(2)]

    def key_side(t, slot, issue_args=None):
        x3 = h_ref[t]
        x0, x1 = x3[0:SUBLANES], x3[SUBLANES:SLAB]
        raw = load_segment(slot, 0)
        parts = []
        for seg in range(N_SEGMENT):
            nxt = load_segment(slot, seg + 1) if seg + 1 < N_SEGMENT else None
            if issue_args is not None:
                issue(*issue_args, seg * ISSUE_PER_SEGMENT, ISSUE_PER_SEGMENT)
            for k in range(SEGMENT):
                parts.append(pltpu.bitcast(raw[2 * k] << 16, F32) * x0
                             + pltpu.bitcast(raw[2 * k + 1] << 16, F32) * x1)
            if len(parts) == EXPERT_GROUP:
                grp = seg * SEGMENT // EXPERT_GROUP
                pbuf[pl.ds(grp * LANES, LANES), :] = jnp.concatenate(parts, axis=0).astype(BF16)
                parts = []
            raw = nxt

    def weight_inputs(t):
        q = jnp.dot(sel_ref[...], pbuf[...], preferred_element_type=F32)
        gate = jnp.sum(jnp.where(lane_t == t, gate_ref[...], 0.0), axis=1, keepdims=True)
        return q, gate

    def store_weights(q, gate):
        act = jnp.sum(q, axis=1, keepdims=True)
        wsc[...] = jnp.broadcast_to(gate * _gelu_tanh(act), (PEER_SEL, LANES))

    def value_side(t, slot, issue_args=None):
        acc = [jnp.zeros((SUBLANES, LANES), F32) for _ in range(4)]

        def load_weights(seg):
            return [wsc[pl.ds(seg * SEGMENT + k, 1), :] for k in range(SEGMENT)]

        raw, wts = load_segment(slot, 0), load_weights(0)
        for seg in range(N_SEGMENT):
            last = seg + 1 == N_SEGMENT
            nxt, wnxt = (None, None) if last else (load_segment(slot, seg + 1), load_weights(seg + 1))
            if issue_args is not None:
                issue(*issue_args, (N_SEGMENT + seg) * ISSUE_PER_SEGMENT, ISSUE_PER_SEGMENT)
            for k in range(SEGMENT):
                v0 = pltpu.bitcast(raw[2 * k] & hi_mask, F32)
                v1 = pltpu.bitcast(raw[2 * k + 1] & hi_mask, F32)
                acc[2 * (k % 2)] = acc[2 * (k % 2)] + wts[k] * v0
                acc[2 * (k % 2) + 1] = acc[2 * (k % 2) + 1] + wts[k] * v1
            raw, wts = nxt, wnxt
        o_ref[t] = x_ref[t] + jnp.concatenate([acc[0] + acc[2], acc[1] + acc[3]], axis=0)

    def prologue_body(n, carry):
        k = n // (PEER_SEL // SEGMENT)
        j = (n % (PEER_SEL // SEGMENT)) * SEGMENT
        for b in range(SEGMENT):
            pltpu.make_async_copy(tab_hbm.at[idx_s[k * PEER_SEL + j + b]], gbuf.at[k, j + b],
                                  gsem.at[k]).start(priority=b % 2)
        return carry

    lax.fori_loop(0, LOOKAHEAD * (PEER_SEL // SEGMENT), prologue_body, 0)
    wait(0)
    key_side(0, 0)
    store_weights(*weight_inputs(0))
    wait(1)
    key_side(1, 1)

    def token_body(t, carry):
        gather = (jnp.minimum(t + LOOKAHEAD, PEER_TB - 1), (t + LOOKAHEAD) % N_SLOT)
        wait((t + 2) % N_SLOT)
        q, gate = weight_inputs(t + 1)
        key_side(t + 2, (t + 2) % N_SLOT, gather)
        value_side(t, t % N_SLOT, gather)
        store_weights(q, gate)
        return carry

    n_here = jnp.minimum(PEER_TB, pl.num_programs(0) * PEER_TB - i * PEER_TB)
    lax.fori_loop(0, n_here - 2, token_body, 0)
    value_side(PEER_TB - 2, (PEER_TB - 2) % N_SLOT)
    store_weights(*weight_inputs(PEER_TB - 1))
    value_side(PEER_TB - 1, (PEER_TB - 1) % N_SLOT)
    for k in range(LOOKAHEAD - 2):
        wait(k)


def _peer_experts(idx, tab, h, gate_t, x, n_tokens):
    t, d = x.shape
    n_blk = n_tokens // PEER_TB
    slabs = pl.BlockSpec((PEER_TB, SLAB, LANES), lambda i: (i, 0, 0))
    sel = (jnp.arange(PEER_SEL)[:, None] == jnp.arange(PEER_SEL * SUBLANES)[None, :] // SUBLANES).astype(BF16)
    out = pl.pallas_call(
        _peer_expert_kernel,
        grid=(n_blk,),
        in_specs=[pl.BlockSpec(memory_space=pl.ANY),
                  pl.BlockSpec(memory_space=pl.ANY),
                  pl.BlockSpec((PEER_SEL, PEER_SEL * SUBLANES), lambda i: (0, 0)),
                  slabs,
                  pl.BlockSpec((PEER_SEL, PEER_TB), lambda i: (0, i)),
                  slabs],
        out_specs=slabs,
        out_shape=jax.ShapeDtypeStruct((n_tokens, SLAB, LANES), F32),
        scratch_shapes=[pltpu.SMEM((PEER_TB * PEER_SEL,), jnp.int32),
                        pltpu.VMEM((N_SLOT, PEER_SEL, SLAB, LANES), jnp.uint32),
                        pltpu.VMEM((PEER_SEL * SUBLANES, LANES), BF16),
                        pltpu.VMEM((PEER_SEL, LANES), F32),
                        pltpu.SemaphoreType.DMA,
                        pltpu.SemaphoreType.DMA((N_SLOT,))],
        compiler_params=_cparams("arbitrary"),
        name="peer_experts",
    )(idx, tab, sel, h.reshape(t, SLAB, LANES), gate_t, x.reshape(t, SLAB, LANES))
    return out.reshape(n_tokens, d)


def _pack_expert_table(u, v):
    ub = lax.bitcast_convert_type(u.astype(BF16), jnp.uint16).astype(jnp.uint32)
    vb = lax.bitcast_convert_type(v.astype(BF16), jnp.uint16).astype(jnp.uint32)
    return ub | (vb << 16)


SC_LANES = 16
SC_CHUNKS = D_MODEL // SC_LANES
SC_EXPERTS = 16
SC_BLOCK = 32
PEER_SC_TOKENS = 1024


def _gelu_tanh_exp(x):
    z = math.sqrt(2.0 / math.pi) * (x + 0.044715 * (x * x * x))
    return 0.5 * x * (2.0 - 2.0 / (jnp.exp(2.0 * z) + 1.0))


def _peer_experts_sc(idx, tab, h, gate, x, first, n):
    d = x.shape[1]
    info = plsc.get_sparse_core_info()
    n_core, n_worker = info.num_cores, info.num_cores * info.num_subcores
    per_worker = n // n_worker
    n_chunk = PEER_SEL // SC_EXPERTS
    hi_mask = jnp.uint32(0xFFFF0000)

    def body(idx_hbm, tab_hbm, h_hbm, gate_hbm, x_hbm, o_hbm, idx_v, gate_v, h_v, out_v, rows_v, w_v, sem):
        worker = lax.axis_index("subcore") * n_core + lax.axis_index("core")

        def gather(cb):
            return pltpu.make_async_copy(tab_hbm.at[idx_v.at[pl.ds(cb * SC_EXPERTS, SC_EXPERTS)]],
                                         rows_v.at[cb % 2], sem.at[cb % 2])

        def mix_chunk(cb):
            rows = rows_v.at[cb % 2]
            for e in range(SC_EXPERTS):
                w_v[e, :] = jnp.zeros((SC_LANES,), F32)

            def key_block(blk, carry):
                base = blk * (SC_BLOCK * SC_LANES)
                hs = [h_v[pl.ds(base + j * SC_LANES, SC_LANES)] for j in range(SC_BLOCK)]
                for e in range(SC_EXPERTS):
                    a = w_v[e, :]
                    for j in range(SC_BLOCK):
                        word = rows[e, pl.ds(base + j * SC_LANES, SC_LANES)]
                        a = a + plsc.bitcast(word << 16, F32) * hs[j]
                    w_v[e, :] = a
                return carry

            lax.fori_loop(0, SC_CHUNKS // SC_BLOCK, key_block, 0)
            for e in range(SC_EXPERTS):
                act = jnp.full((SC_LANES,), jnp.sum(w_v[e, :]), F32)
                g = plsc.load_gather(gate_v, [jnp.full((SC_LANES,), cb * SC_EXPERTS + e, jnp.int32)])
                w_v[e, :] = g * _gelu_tanh_exp(act)

            def value_block(blk, carry):
                base = blk * (SC_BLOCK * SC_LANES)
                outs = [out_v[pl.ds(base + j * SC_LANES, SC_LANES)] for j in range(SC_BLOCK)]
                for e in range(SC_EXPERTS):
                    w = w_v[e, :]
                    for j in range(SC_BLOCK):
                        word = rows[e, pl.ds(base + j * SC_LANES, SC_LANES)]
                        outs[j] = outs[j] + w * plsc.bitcast(word & hi_mask, F32)
                for j in range(SC_BLOCK):
                    out_v[pl.ds(base + j * SC_LANES, SC_LANES)] = outs[j]
                return carry

            lax.fori_loop(0, SC_CHUNKS // SC_BLOCK, value_block, 0)

        @pl.loop(0, per_worker)
        def _(i):
            row = worker * per_worker + i
            t = first + row
            pltpu.sync_copy(idx_hbm.at[t], idx_v)
            gather(0).start()
            pltpu.sync_copy(gate_hbm.at[t], gate_v)
            pltpu.sync_copy(h_hbm.at[t], h_v)
            pltpu.sync_copy(x_hbm.at[t], out_v)

            @pl.loop(0, n_chunk)
            def _(cb):
                @pl.when(cb + 1 < n_chunk)
                def _():
                    gather(cb + 1).start()

                gather(cb).wait()
                mix_chunk(cb)

            pltpu.sync_copy(out_v, o_hbm.at[row])

    return pl.kernel(
        body,
        out_type=jax.ShapeDtypeStruct((n, d), F32),
        mesh=plsc.VectorSubcoreMesh(core_axis_name="core", subcore_axis_name="subcore"),
        scratch_types=[pltpu.VMEM((PEER_SEL,), jnp.int32),
                       pltpu.VMEM((PEER_SEL,), F32),
                       pltpu.VMEM((d,), F32),
                       pltpu.VMEM((d,), F32),
                       pltpu.VMEM((2, SC_EXPERTS, d), jnp.uint32),
                       pltpu.VMEM((SC_EXPERTS, SC_LANES), F32),
                       pltpu.SemaphoreType.DMA((2,))],
        compiler_params=pltpu.CompilerParams(needs_layout_passes=False),
        name="peer_experts_sc",
    )(idx, tab, h, gate, x)


def _final_norm_kernel(x_ref, g_ref, o_ref):
    o_ref[...] = _rms(x_ref[...], g_ref[...])


def _final_norm(x, g, *, tm=512):
    t, d = x.shape
    row = pl.BlockSpec((tm, d), lambda i: (i, 0))
    return pl.pallas_call(
        _final_norm_kernel,
        grid=(t // tm,),
        in_specs=[row, pl.BlockSpec((1, d), lambda i: (0, 0))],
        out_specs=row,
        out_shape=jax.ShapeDtypeStruct((t, d), F32),
        compiler_params=_cparams("arbitrary"),
        name="final_norm",
    )(x, g.reshape(1, d))


def _trunk(x, params):
    b, s, d = x.shape
    t = b * s
    x = x.reshape(t, d)
    cos2, sin2 = _rotary_tables(s)
    for l in range(DEPTH):
        p = params[l]
        proj = _inproj(x, p["norm_mix"], p["w_in"]).reshape(b, s, IN_COLS)
        att = _attention(proj, params["band"])
        conv = _short_conv(proj, p["conv_w"])
        ret = _retention(proj, p["log_g"], cos2, sin2)
        x = _outproj(att.reshape(t, ATT_WIDTH), conv.reshape(t, CONV_WIDTH),
                     ret.reshape(t, RET_WIDTH), x, p["w_out"])
        h, eidx_t, gate_t = _peer_topk(x, p["norm_ffn"], p["peer_wq"], p["peer_keys"])
        t_tc = t - PEER_SC_TOKENS
        idx = jnp.transpose(eidx_t)
        x_tc = _peer_experts(idx.reshape(t // PEER_TB, PEER_TB * PEER_SEL),
                             p["peer_tab"].reshape(PEER_EXPERTS, SLAB, LANES), h, gate_t, x, t_tc)
        x_sc = _peer_experts_sc(idx, p["peer_tab"], h, jnp.transpose(gate_t), x, t_tc, PEER_SC_TOKENS)
        x = jnp.concatenate([x_tc, x_sc], axis=0)
    return _final_norm(x, params["final_norm"]).reshape(b, s, d)


def kernel(x_prompt, x_sample, rel_bias, final_norm, norm_mix, norm_ffn, w_in, conv_w, ret_decay,
           w_out, peer_wq, peer_keys, peer_u, peer_v):
    params = {"band": _attention_band(rel_bias), "final_norm": final_norm}
    for l in range(DEPTH):
        params[l] = {
            "norm_mix": norm_mix[l],
            "norm_ffn": norm_ffn[l],
            "w_in": w_in[l].astype(BF16),
            "conv_w": conv_w[l],
            "log_g": jax.nn.log_sigmoid(ret_decay[l].astype(F32)),
            "w_out": w_out[l].astype(BF16),
            "peer_wq": peer_wq[l].astype(BF16),
            "peer_keys": peer_keys[l].reshape(2 * PEER_HEADS, PEER_NKEYS, PEER_DHALF).astype(BF16),
            "peer_tab": _pack_expert_table(peer_u[l], peer_v[l]),
        }
    return (_trunk(x_prompt, params), _trunk(x_sample, params))
```

```python
import functools
import math

import jax
import jax.numpy as jnp
import numpy as np
from jax import lax
from jax.experimental import pallas as pl
from jax.experimental.pallas import tpu as pltpu
from jax.experimental.pallas import tpu_sc as plsc

F32 = jnp.float32
BF16 = jnp.bfloat16

D_MODEL = 2048
DEPTH = 2
N_ATT_HEADS = 12
ATT_HEAD_DIM = 64
ATT_WIDTH = N_ATT_HEADS * ATT_HEAD_DIM
DILATED_PATTERNS = ((128, 1), (512, 4), (2048, 16))
REL_BUCKETS = 32
REL_MAX_DIST = 1024
CONV_WIDTH = 512
N_RET_HEADS = 6
RET_HEAD_DIM = 128
RET_WIDTH = N_RET_HEADS * RET_HEAD_DIM
RET_CHUNK = 128
ROPE_BASE = 10000.0
MIX_WIDTH = ATT_WIDTH + CONV_WIDTH + RET_WIDTH
IN_COLS = 3 * ATT_WIDTH + 3 * CONV_WIDTH + 4 * RET_WIDTH
PEER_HEADS = 8
PEER_NKEYS = 128
PEER_EXPERTS = PEER_NKEYS * PEER_NKEYS
PEER_DHALF = 128
PEER_TOPK = 16
PEER_SEL = PEER_HEADS * PEER_TOPK
RMS_EPS = 1e-6
GN_EPS = 1e-6
NEG_INF = -1e30

LANES = 128
SUBLANES = 8
VMEM_LIMIT_BYTES = 56 * 1024 * 1024

ATT_W = 64
ATT_TQ = 128
ATT_TK = ATT_TQ + 2 * ATT_W
ATT_UNROLL = 8

COL_AQ, COL_AK, COL_AV = 0, ATT_WIDTH // LANES, 2 * ATT_WIDTH // LANES
COL_CB = 3 * ATT_WIDTH // 256
COL_CC = COL_CB + CONV_WIDTH // 256
COL_CH = COL_CC + CONV_WIDTH // 256
COL_RQ = (3 * ATT_WIDTH + 3 * CONV_WIDTH) // LANES
COL_RK = COL_RQ + N_RET_HEADS
COL_RV = COL_RK + N_RET_HEADS
COL_RG = COL_RV + N_RET_HEADS


def _cparams(*sem):
    return pltpu.CompilerParams(dimension_semantics=sem, vmem_limit_bytes=VMEM_LIMIT_BYTES)


def _rms(x, g):
    ms = jnp.mean(x * x, axis=-1, keepdims=True)
    return x * lax.rsqrt(ms + RMS_EPS) * g


def _inproj_kernel(x_ref, g_ref, w_ref, o_ref):
    h = _rms(x_ref[...], g_ref[...]).astype(BF16)
    o_ref[...] = jnp.dot(h, w_ref[...], preferred_element_type=F32)


def _inproj(x, g, w, *, tm=512, tn=2304):
    t, d = x.shape
    n = w.shape[1]
    return pl.pallas_call(
        _inproj_kernel,
        grid=(n // tn, t // tm),
        in_specs=[pl.BlockSpec((tm, d), lambda j, i: (i, 0)),
                  pl.BlockSpec((1, d), lambda j, i: (0, 0)),
                  pl.BlockSpec((d, tn), lambda j, i: (0, j))],
        out_specs=pl.BlockSpec((tm, tn), lambda j, i: (i, j)),
        out_shape=jax.ShapeDtypeStruct((t, n), F32),
        compiler_params=_cparams("arbitrary", "arbitrary"),
        name="inproj",
    )(x, g.reshape(1, d), w)


def _attn_kernel(q_ref, k_ref, v_ref, band_ref, o_ref, kbuf, vbuf0, vbuf1, acc_ref, m_ref, l_ref):
    s_len = q_ref.shape[1]
    n_flat = s_len // ATT_TQ
    lane = lax.broadcasted_iota(jnp.int32, (1, LANES), 1)
    head0 = lane < ATT_HEAD_DIM
    kcol = lax.broadcasted_iota(jnp.int32, (1, ATT_TK), 1)
    zpad = jnp.zeros((ATT_W, LANES), BF16)
    one_b = jnp.ones((), BF16)

    for p, (_, dil) in enumerate(DILATED_PATTERNS):
        sub_len = s_len // dil
        n_chunk = sub_len // ATT_TQ
        res_rows = sub_len + 2 * ATT_W

        def rows(start, size):
            if dil == 1:
                return pl.ds(start, size)
            return pl.ds(start, size, stride=dil)

        def split(f):
            return f // n_chunk, f % n_chunk

        def pad_body(r, carry):
            for off in (0, ATT_W + sub_len):
                dst = pl.ds(pl.multiple_of(r * res_rows + off, ATT_W), ATT_W)
                kbuf[dst, :] = zpad
                vbuf0[dst, :] = zpad
                vbuf1[dst, :] = zpad
            return carry

        lax.fori_loop(0, dil, pad_body, 0)

        def stage_body(f, carry):
            r, c = split(f)
            src = rows(r + c * ATT_TQ * dil, ATT_TQ)
            dst = pl.ds(pl.multiple_of(r * res_rows + ATT_W + c * ATT_TQ, ATT_W), ATT_TQ)
            kbuf[dst, :] = k_ref[0, src, :].astype(BF16)
            v = v_ref[0, src, :].astype(BF16)
            vbuf0[dst, :] = jnp.where(head0, v, one_b)
            vbuf1[dst, :] = jnp.where(head0, one_b, v)
            return carry

        lax.fori_loop(0, n_flat, stage_body, 0)

        def chunk(f):
            r, c = split(f)
            qrows = rows(r + c * ATT_TQ * dil, ATT_TQ)
            q = q_ref[0, qrows, :] * (ATT_HEAD_DIM ** -0.5)
            win = pl.ds(pl.multiple_of(r * res_rows + c * ATT_TQ, ATT_W), ATT_TK)
            kw = kbuf[win, :]
            kpos = c * ATT_TQ - ATT_W + kcol
            pos_ok = (kpos >= 0) & (kpos < sub_len)
            pv, mm = [], []
            for h, vbuf in enumerate((vbuf0, vbuf1)):
                sel = head0 if h == 0 else jnp.logical_not(head0)
                qh = jnp.where(sel, q, 0.0).astype(BF16)
                s = lax.dot_general(qh, kw, (((1,), (1,)), ((), ())), preferred_element_type=F32)
                s = jnp.where(pos_ok, s + band_ref[p, h], NEG_INF)
                m = jnp.max(s, axis=1, keepdims=True)
                mm.append(m)
                pv.append(jnp.dot(jnp.exp(s - m).astype(BF16), vbuf[win, :], preferred_element_type=F32))
            acc_n = jnp.where(head0, pv[0], pv[1])
            m_n = jnp.where(head0, mm[0], mm[1])
            l_n = pltpu.roll(jnp.where(head0, pv[1], pv[0]), ATT_HEAD_DIM, 1)
            if p == 0:
                acc_ref[qrows, :] = acc_n
                m_ref[qrows, :] = m_n
                l_ref[qrows, :] = l_n
            else:
                m_o = m_ref[qrows, :]
                m_t = jnp.maximum(m_o, m_n)
                a_o = jnp.exp(m_o - m_t)
                a_n = jnp.exp(m_n - m_t)
                acc_ref[qrows, :] = acc_ref[qrows, :] * a_o + acc_n * a_n
                l_ref[qrows, :] = l_ref[qrows, :] * a_o + l_n * a_n
                m_ref[qrows, :] = m_t

        def chunks_body(g, carry):
            for u in range(ATT_UNROLL):
                chunk(g * ATT_UNROLL + u)
            return carry

        lax.fori_loop(0, n_flat // ATT_UNROLL, chunks_body, 0)

    def out_body(c, carry):
        rws = pl.ds(pl.multiple_of(c * ATT_TQ, ATT_TQ), ATT_TQ)
        o_ref[0, rws, :] = (acc_ref[rws, :] / l_ref[rws, :]).astype(o_ref.dtype)
        return carry

    lax.fori_loop(0, s_len // ATT_TQ, out_body, 0)


def _attention(proj, band):
    b, s, _ = proj.shape
    n_pair = N_ATT_HEADS // 2
    staged_rows = s + 2 * ATT_W * max(dil for _, dil in DILATED_PATTERNS)
    blk = lambda off: pl.BlockSpec((1, s, LANES), lambda bi, hp: (bi, 0, off + hp))
    return pl.pallas_call(
        _attn_kernel,
        grid=(b, n_pair),
        in_specs=[blk(COL_AQ), blk(COL_AK), blk(COL_AV),
                  pl.BlockSpec((len(DILATED_PATTERNS), 2, ATT_TQ, ATT_TK),
                               lambda bi, hp: (0, hp, 0, 0))],
        out_specs=pl.BlockSpec((1, s, LANES), lambda bi, hp: (bi, 0, hp)),
        out_shape=jax.ShapeDtypeStruct((b, s, ATT_WIDTH), BF16),
        scratch_shapes=[pltpu.VMEM((staged_rows, LANES), BF16)] * 3 + [
                        pltpu.VMEM((s, LANES), F32),
                        pltpu.VMEM((s, LANES), F32),
                        pltpu.VMEM((s, LANES), F32)],
        compiler_params=_cparams("arbitrary", "arbitrary"),
        name="dilated_attention",
    )(proj, proj, proj, band)


def _t5_bucket(rel):
    nb = REL_BUCKETS // 2
    ret = (rel > 0).astype(np.int32) * nb
    n = np.abs(rel)
    max_exact = nb // 2
    nf = np.maximum(n, 1).astype(np.float32)
    large = max_exact + (np.log(nf / max_exact) / math.log(REL_MAX_DIST / max_exact)
                         * (nb - max_exact)).astype(np.int32)
    large = np.minimum(large, nb - 1)
    return ret + np.where(n < max_exact, n, large)


def _attention_band(rel_bias):
    span = ATT_TQ - 1 + ATT_W
    rel = np.arange(-span, ATT_TK - ATT_W)
    out = []
    for _, dil in DILATED_PATTERNS:
        diag = jnp.where((np.abs(rel) <= ATT_W)[:, None], rel_bias[_t5_bucket(dil * rel)].astype(F32), NEG_INF)
        diag = jnp.transpose(diag)
        out.append(jnp.stack([diag[:, ATT_TQ - 1 - i:ATT_TQ - 1 - i + ATT_TK] for i in range(ATT_TQ)], axis=1))
    return jnp.stack(out, axis=0)


CONV_ROWS = 512


def _conv_kernel(cb_ref, cc_ref, ch_ref, w_ref, o_ref, zbuf):
    s_len = cb_ref.shape[1]
    width = cb_ref.shape[2]
    zero = jnp.zeros((SUBLANES, width), F32)
    zbuf[pl.ds(0, SUBLANES), :] = zero
    zbuf[pl.ds(SUBLANES + s_len, SUBLANES), :] = zero
    n_blk = s_len // CONV_ROWS

    for c in range(n_blk):
        r0 = c * CONV_ROWS
        zbuf[pl.ds(SUBLANES + r0, CONV_ROWS), :] = (
            cc_ref[0, pl.ds(r0, CONV_ROWS), :] * ch_ref[0, pl.ds(r0, CONV_ROWS), :])
    w0 = w_ref[0:1, :]
    w1 = w_ref[1:2, :]
    w2 = w_ref[2:3, :]

    for c in range(n_blk):
        r0 = c * CONV_ROWS
        y = (w0 * zbuf[pl.ds(SUBLANES - 1 + r0, CONV_ROWS), :]
             + w1 * zbuf[pl.ds(SUBLANES + r0, CONV_ROWS), :]
             + w2 * zbuf[pl.ds(SUBLANES + 1 + r0, CONV_ROWS), :])
        o_ref[0, pl.ds(r0, CONV_ROWS), :] = (cb_ref[0, pl.ds(r0, CONV_ROWS), :] * y).astype(o_ref.dtype)


def _short_conv(proj, conv_w):
    b, s, _ = proj.shape
    cw = 256
    blk = lambda off: pl.BlockSpec((1, s, cw), lambda bi, j: (bi, 0, off + j))
    return pl.pallas_call(
        _conv_kernel,
        grid=(b, CONV_WIDTH // cw),
        in_specs=[blk(COL_CB), blk(COL_CC), blk(COL_CH),
                  pl.BlockSpec((3, cw), lambda bi, j: (0, j))],
        out_specs=pl.BlockSpec((1, s, cw), lambda bi, j: (bi, 0, j)),
        out_shape=jax.ShapeDtypeStruct((b, s, CONV_WIDTH), BF16),
        scratch_shapes=[pltpu.VMEM((s + 2 * SUBLANES, cw), F32)],
        compiler_params=_cparams("arbitrary", "arbitrary"),
        name="short_conv",
    )(proj, proj, proj, conv_w)


RET_UNROLL = 4


def _ret_kernel(lg_ref, q_ref, k_ref, v_ref, g_ref, cos_ref, sin_ref, o_ref, qs, ks, os_f, os_b):
    s_len = q_ref.shape[1]
    c_len = RET_CHUNK
    n_chunk = s_len // c_len
    h = pl.program_id(1)
    lg_f = lg_ref[0, h]
    lg_b = lg_ref[1, h]

    def rot_body(n, carry):
        rws = pl.ds(pl.multiple_of(n * c_len, c_len), c_len)
        cs = cos_ref[rws, :]
        sn = sin_ref[rws, :]
        q = q_ref[0, rws, :]
        k = k_ref[0, rws, :]
        qs[rws, :] = q * cs + pltpu.roll(q, RET_HEAD_DIM // 2, 1) * sn
        ks[rws, :] = (k * cs + pltpu.roll(k, RET_HEAD_DIM // 2, 1) * sn) * (RET_HEAD_DIM ** -0.5)
        return carry

    lax.fori_loop(0, n_chunk, rot_body, 0)

    ri = lax.broadcasted_iota(jnp.int32, (c_len, c_len), 0)
    ci = lax.broadcasted_iota(jnp.int32, (c_len, c_len), 1)
    diff = (ri - ci).astype(F32)
    col = lax.broadcasted_iota(jnp.int32, (c_len, 1), 0).astype(F32)
    one = jnp.ones((1, 1), F32)

    dec_f = jnp.where(diff >= 0, jnp.exp(lg_f * jnp.maximum(diff, 0.0)), 0.0)
    kw_f = jnp.exp(lg_f * (c_len - 1 - col))
    qw_f = jnp.exp(lg_f * (col + 1))
    gc_f = jnp.exp(lg_f * c_len * one)
    dec_b = jnp.where(diff < 0, jnp.exp(lg_b * jnp.maximum(-diff, 0.0)), 0.0)
    kw_b = jnp.exp(lg_b * col)
    qw_b = jnp.exp(lg_b * (c_len - col))
    gc_b = jnp.exp(lg_b * c_len * one)

    def direction(n, state, dec, kw, qw, gc):
        rws = pl.ds(pl.multiple_of(n * c_len, c_len), c_len)
        q = qs[rws, :]
        k = ks[rws, :]
        v = v_ref[0, rws, :].astype(BF16)
        qb = q.astype(BF16)
        sc = lax.dot_general(qb, k.astype(BF16), (((1,), (1,)), ((), ())),
                             preferred_element_type=F32)
        o_in = jnp.dot((sc * dec).astype(BF16), v, preferred_element_type=F32)
        o_x = jnp.dot(qb, state.astype(BF16), preferred_element_type=F32) * qw
        kv = jnp.dot(jnp.transpose(k * kw).astype(BF16), v, preferred_element_type=F32)
        return rws, o_in + o_x, gc * state + kv

    def sweep_body(g, states):
        s_f, s_b = states
        for u in range(RET_UNROLL):
            n = g * RET_UNROLL + u
            rws, o, s_f = direction(n, s_f, dec_f, kw_f, qw_f, gc_f)
            os_f[rws, :] = o
            rws, o, s_b = direction(n_chunk - 1 - n, s_b, dec_b, kw_b, qw_b, gc_b)
            os_b[rws, :] = o
        return s_f, s_b

    zero_state = jnp.zeros((RET_HEAD_DIM, RET_HEAD_DIM), F32)
    lax.fori_loop(0, n_chunk // RET_UNROLL, sweep_body, (zero_state, zero_state))

    def norm_body(g, carry):
        for u in range(RET_UNROLL):
            rws = pl.ds(pl.multiple_of((g * RET_UNROLL + u) * c_len, c_len), c_len)
            o = os_f[rws, :] + os_b[rws, :]
            mu = jnp.mean(o, axis=-1, keepdims=True)
            oc = o - mu
            var = jnp.mean(oc * oc, axis=-1, keepdims=True)
            on = oc * lax.rsqrt(var + GN_EPS)
            gt = g_ref[0, rws, :]
            o_ref[0, rws, :] = (gt * jax.nn.sigmoid(gt) * on).astype(o_ref.dtype)
        return carry

    lax.fori_loop(0, n_chunk // RET_UNROLL, norm_body, 0)


def _retention(proj, log_g, cos2, sin2):
    b, s, _ = proj.shape
    blk = lambda off: pl.BlockSpec((1, s, LANES), lambda bi, h: (bi, 0, off + h))
    tab = pl.BlockSpec((s, LANES), lambda bi, h: (0, 0))
    return pl.pallas_call(
        _ret_kernel,
        grid=(b, N_RET_HEADS),
        in_specs=[pl.BlockSpec(memory_space=pltpu.SMEM),
                  blk(COL_RQ), blk(COL_RK), blk(COL_RV), blk(COL_RG), tab, tab],
        out_specs=pl.BlockSpec((1, s, LANES), lambda bi, h: (bi, 0, h)),
        out_shape=jax.ShapeDtypeStruct((b, s, RET_WIDTH), BF16),
        scratch_shapes=[pltpu.VMEM((s, LANES), F32)] * 4,
        compiler_params=_cparams("arbitrary", "arbitrary"),
        name="retention",
    )(log_g, proj, proj, proj, proj, cos2, sin2)


def _rotary_tables(s):
    d = RET_HEAD_DIM
    inv = ROPE_BASE ** (-jnp.arange(0, d, 2, dtype=F32) / d)
    ang = jnp.arange(s, dtype=F32)[:, None] * inv[None, :]
    cos, sin = jnp.cos(ang), jnp.sin(ang)
    return jnp.concatenate([cos, cos], axis=-1), jnp.concatenate([-sin, sin], axis=-1)


def _outproj_kernel(att_ref, conv_ref, ret_ref, x_ref, w_ref, o_ref):
    acc = jnp.dot(att_ref[...], w_ref[pl.ds(0, ATT_WIDTH), :], preferred_element_type=F32)
    acc += jnp.dot(conv_ref[...], w_ref[pl.ds(ATT_WIDTH, CONV_WIDTH), :], preferred_element_type=F32)
    acc += jnp.dot(ret_ref[...], w_ref[pl.ds(ATT_WIDTH + CONV_WIDTH, RET_WIDTH), :],
                   preferred_element_type=F32)
    o_ref[...] = x_ref[...] + acc


def _outproj(att, conv, ret, x, w, *, tm=512):
    t, d = x.shape
    row = lambda width: pl.BlockSpec((tm, width), lambda i: (i, 0))
    return pl.pallas_call(
        _outproj_kernel,
        grid=(t // tm,),
        in_specs=[row(ATT_WIDTH), row(CONV_WIDTH), row(RET_WIDTH), row(d),
                  pl.BlockSpec((MIX_WIDTH, d), lambda i: (0, 0))],
        out_specs=row(d),
        out_shape=jax.ShapeDtypeStruct((t, d), F32),
        compiler_params=_cparams("arbitrary"),
        name="outproj",
    )(att, conv, ret, x, w)


def _top16(s, payload=None):
    n = s.shape[0]
    row = lax.broadcasted_iota(jnp.int32, s.shape, 0).astype(F32)
    vals, picks = [], []
    for _ in range(PEER_TOPK):
        m = jnp.max(s, axis=0, keepdims=True)
        idx = jnp.min(jnp.where(s == m, row, float(n)), axis=0, keepdims=True)
        hit = row == idx
        vals.append(m)
        if payload is None:
            picks.append(idx)
        else:
            picks.append(jnp.max(jnp.where(hit, payload, -1.0), axis=0, keepdims=True))
        s = jnp.where(hit, -jnp.inf, s)
    return vals, picks


PAIR_COUNTS = tuple(PEER_TOPK // (a + 1) for a in range(PEER_TOPK))
N_PAIR = sum(PAIR_COUNTS)
N_PAIR_PAD = -(-N_PAIR // SUBLANES) * SUBLANES


def _peer_topk_kernel(x_ref, g_ref, wq_ref, keys_ref, h_ref, eidx_ref, gate_ref, sv_s, si_s, cand_s, eid_s):
    h = _rms(x_ref[...], g_ref[...])
    h_ref[...] = h
    q = jnp.dot(h.astype(BF16), wq_ref[...], preferred_element_type=F32).astype(BF16)
    for grp in range(2 * PEER_HEADS):
        s = lax.dot_general(keys_ref[grp], q[:, grp * PEER_DHALF:(grp + 1) * PEER_DHALF],
                            (((1,), (1,)), ((), ())), preferred_element_type=F32)
        vals, idxs = _top16(s)
        for j in range(PEER_TOPK):
            sv_s[grp, pl.ds(j, 1), :] = vals[j]
            si_s[grp, pl.ds(j, 1), :] = idxs[j]
    tm = cand_s.shape[1]
    cand_s[pl.ds(N_PAIR_PAD - SUBLANES, SUBLANES), :] = jnp.full((SUBLANES, tm), -jnp.inf, F32)
    eid_s[pl.ds(N_PAIR_PAD - SUBLANES, SUBLANES), :] = jnp.full((SUBLANES, tm), -1.0, F32)
    for hd in range(PEER_HEADS):
        sv0, sv1 = sv_s[2 * hd], sv_s[2 * hd + 1]
        si0, si1 = si_s[2 * hd], si_s[2 * hd + 1]
        off = 0
        for a, nb in enumerate(PAIR_COUNTS):
            cand_s[pl.ds(off, nb), :] = sv0[a:a + 1, :] + sv1[0:nb, :]
            eid_s[pl.ds(off, nb), :] = si0[a:a + 1, :] * float(PEER_NKEYS) + si1[0:nb, :]
            off += nb
        vals, picks = _top16(cand_s[...], eid_s[...])
        cv = jnp.concatenate(vals, axis=0)
        e = jnp.exp(cv - vals[0])
        gate_ref[pl.ds(hd * PEER_TOPK, PEER_TOPK), :] = e / jnp.sum(e, axis=0, keepdims=True)
        eidx_ref[pl.ds(hd * PEER_TOPK, PEER_TOPK), :] = jnp.concatenate(picks, axis=0).astype(jnp.int32)


def _peer_topk(x, g, wq, keys, *, tm=256):
    t, d = x.shape
    return pl.pallas_call(
        _peer_topk_kernel,
        grid=(t // tm,),
        in_specs=[pl.BlockSpec((tm, d), lambda i: (i, 0)),
                  pl.BlockSpec((1, d), lambda i: (0, 0)),
                  pl.BlockSpec((d, 2 * PEER_HEADS * PEER_DHALF), lambda i: (0, 0)),
                  pl.BlockSpec((2 * PEER_HEADS, PEER_NKEYS, PEER_DHALF), lambda i: (0, 0, 0))],
        out_specs=[pl.BlockSpec((tm, d), lambda i: (i, 0)),
                   pl.BlockSpec((PEER_SEL, tm), lambda i: (0, i)),
                   pl.BlockSpec((PEER_SEL, tm), lambda i: (0, i))],
        out_shape=[jax.ShapeDtypeStruct((t, d), F32),
                   jax.ShapeDtypeStruct((PEER_SEL, t), jnp.int32),
                   jax.ShapeDtypeStruct((PEER_SEL, t), F32)],
        scratch_shapes=[pltpu.VMEM((2 * PEER_HEADS, PEER_TOPK, tm), F32),
                        pltpu.VMEM((2 * PEER_HEADS, PEER_TOPK, tm), F32),
                        pltpu.VMEM((N_PAIR_PAD, tm), F32),
                        pltpu.VMEM((N_PAIR_PAD, tm), F32)],
        compiler_params=_cparams("arbitrary"),
        name="peer_topk",
    )(x, g.reshape(1, d), wq, keys)


PEER_TB = 512
N_SLOT = 8
LOOKAHEAD = N_SLOT - 1
SLAB = D_MODEL // LANES
EXPERT_GROUP = 16
N_GROUP = PEER_SEL // EXPERT_GROUP
SEGMENT = 8
N_SEGMENT = PEER_SEL // SEGMENT
ISSUE_PER_SEGMENT = PEER_SEL // (2 * N_SEGMENT)


def _gelu_tanh(x):
    return 0.5 * x * (1.0 + jnp.tanh(math.sqrt(2.0 / math.pi) * (x + 0.044715 * (x * x * x))))


def _peer_expert_kernel(idx_hbm, tab_hbm, sel_ref, h_ref, gate_ref, x_ref, o_ref,
                        idx_s, gbuf, pbuf, wsc, isem, gsem):
    i = pl.program_id(0)
    icp = pltpu.make_async_copy(idx_hbm.at[i], idx_s, isem)
    icp.start()
    icp.wait()

    def issue(t, slot, j0, n):
        row = idx_s.at[pl.ds(t * PEER_SEL, PEER_SEL)]
        for j in range(j0, j0 + n):
            pltpu.make_async_copy(tab_hbm.at[row[j]], gbuf.at[slot, j],
                                  gsem.at[slot]).start(priority=j % 2)

    def wait(slot):
        pltpu.make_async_copy(tab_hbm.at[pl.ds(0, PEER_SEL)], gbuf.at[slot], gsem.at[slot]).wait()

    lane_t = lax.broadcasted_iota(jnp.int32, (1, PEER_TB), 1)
    hi_mask = jnp.uint32(0xFFFF0000)

    def load_segment(slot, seg):
        return [gbuf[slot, seg * SEGMENT + k, half * SUBLANES:(half + 1) * SUBLANES, :]
                for k in range(SEGMENT) for half in range(2)]

    def key_side(t, slot, issue_args=None):
        x3 = h_ref[t]
        x0, x1 = x3[0:SUBLANES], x3[SUBLANES:SLAB]
        raw = load_segment(slot, 0)
        parts = []
        for seg in range(N_SEGMENT):
            nxt = load_segment(slot, seg + 1) if seg + 1 < N_SEGMENT else None
            if issue_args is not None:
                issue(*issue_args, seg * ISSUE_PER_SEGMENT, ISSUE_PER_SEGMENT)
            for k in range(SEGMENT):
                parts.append(pltpu.bitcast(raw[2 * k] << 16, F32) * x0
                             + pltpu.bitcast(raw[2 * k + 1] << 16, F32) * x1)
            if len(parts) == EXPERT_GROUP:
                grp = seg * SEGMENT // EXPERT_GROUP
                pbuf[pl.ds(grp * LANES, LANES), :] = jnp.concatenate(parts, axis=0).astype(BF16)
                parts = []
            raw = nxt

    def weight_inputs(t):
        q = jnp.dot(sel_ref[...], pbuf[...], preferred_element_type=F32)
        gate = jnp.sum(jnp.where(lane_t == t, gate_ref[...], 0.0), axis=1, keepdims=True)
        return q, gate

    def store_weights(q, gate):
        act = jnp.sum(q, axis=1, keepdims=True)
        wsc[...] = jnp.broadcast_to(gate * _gelu_tanh(act), (PEER_SEL, LANES))

    def value_side(t, slot, issue_args=None):
        acc = [jnp.zeros((SUBLANES, LANES), F32) for _ in range(4)]

        def load_weights(seg):
            return [wsc[pl.ds(seg * SEGMENT + k, 1), :] for k in range(SEGMENT)]

        raw, wts = load_segment(slot, 0), load_weights(0)
        for seg in range(N_SEGMENT):
            last = seg + 1 == N_SEGMENT
            nxt, wnxt = (None, None) if last else (load_segment(slot, seg + 1), load_weights(seg + 1))
            if issue_args is not None:
                issue(*issue_args, (N_SEGMENT + seg) * ISSUE_PER_SEGMENT, ISSUE_PER_SEGMENT)
            for k in range(SEGMENT):
                v0 = pltpu.bitcast(raw[2 * k] & hi_mask, F32)
                v1 = pltpu.bitcast(raw[2 * k + 1] & hi_mask, F32)
                acc[2 * (k % 2)] = acc[2 * (k % 2)] + wts[k] * v0
                acc[2 * (k % 2) + 1] = acc[2 * (k % 2) + 1] + wts[k] * v1
            raw, wts = nxt, wnxt
        o_ref[t] = x_ref[t] + jnp.concatenate([acc[0] + acc[2], acc[1] + acc[3]], axis=0)

    def prologue_body(n, carry):
        k = n // (PEER_SEL // SEGMENT)
        j = (n % (PEER_SEL // SEGMENT)) * SEGMENT
        for b in range(SEGMENT):
            pltpu.make_async_copy(tab_hbm.at[idx_s[k * PEER_SEL + j + b]], gbuf.at[k, j + b],
                                  gsem.at[k]).start(priority=b % 2)
        return carry

    lax.fori_loop(0, LOOKAHEAD * (PEER_SEL // SEGMENT), prologue_body, 0)
    wait(0)
    key_side(0, 0)
    store_weights(*weight_inputs(0))
    wait(1)
    key_side(1, 1)

    def token_body(t, carry):
        gather = (jnp.minimum(t + LOOKAHEAD, PEER_TB - 1), (t + LOOKAHEAD) % N_SLOT)
        wait((t + 2) % N_SLOT)
        q, gate = weight_inputs(t + 1)
        key_side(t + 2, (t + 2) % N_SLOT, gather)
        value_side(t, t % N_SLOT, gather)
        store_weights(q, gate)
        return carry

    n_here = jnp.minimum(PEER_TB, pl.num_programs(0) * PEER_TB - i * PEER_TB)
    lax.fori_loop(0, n_here - 2, token_body, 0)
    value_side(PEER_TB - 2, (PEER_TB - 2) % N_SLOT)
    store_weights(*weight_inputs(PEER_TB - 1))
    value_side(PEER_TB - 1, (PEER_TB - 1) % N_SLOT)
    for k in range(LOOKAHEAD - 2):
        wait(k)


def _peer_experts(idx, tab, h, gate_t, x, n_tokens):
    t, d = x.shape
    n_blk = n_tokens // PEER_TB
    slabs = pl.BlockSpec((PEER_TB, SLAB, LANES), lambda i: (i, 0, 0))
    sel = (jnp.arange(PEER_SEL)[:, None] == jnp.arange(PEER_SEL * SUBLANES)[None, :] // SUBLANES).astype(BF16)
    out = pl.pallas_call(
        _peer_expert_kernel,
        grid=(n_blk,),
        in_specs=[pl.BlockSpec(memory_space=pl.ANY),
                  pl.BlockSpec(memory_space=pl.ANY),
                  pl.BlockSpec((PEER_SEL, PEER_SEL * SUBLANES), lambda i: (0, 0)),
                  slabs,
                  pl.BlockSpec((PEER_SEL, PEER_TB), lambda i: (0, i)),
                  slabs],
        out_specs=slabs,
        out_shape=jax.ShapeDtypeStruct((n_tokens, SLAB, LANES), F32),
        scratch_shapes=[pltpu.SMEM((PEER_TB * PEER_SEL,), jnp.int32),
                        pltpu.VMEM((N_SLOT, PEER_SEL, SLAB, LANES), jnp.uint32),
                        pltpu.VMEM((PEER_SEL * SUBLANES, LANES), BF16),
                        pltpu.VMEM((PEER_SEL, LANES), F32),
                        pltpu.SemaphoreType.DMA,
                        pltpu.SemaphoreType.DMA((N_SLOT,))],
        compiler_params=_cparams("arbitrary"),
        name="peer_experts",
    )(idx, tab, sel, h.reshape(t, SLAB, LANES), gate_t, x.reshape(t, SLAB, LANES))
    return out.reshape(n_tokens, d)


def _pack_expert_table(u, v):
    ub = lax.bitcast_convert_type(u.astype(BF16), jnp.uint16).astype(jnp.uint32)
    vb = lax.bitcast_convert_type(v.astype(BF16), jnp.uint16).astype(jnp.uint32)
    return ub | (vb << 16)


SC_LANES = 16
SC_CHUNKS = D_MODEL // SC_LANES
SC_EXPERTS = 16
SC_BLOCK = 32
PEER_SC_TOKENS = 2048


def _gelu_tanh_exp(x):
    z = math.sqrt(2.0 / math.pi) * (x + 0.044715 * (x * x * x))
    return 0.5 * x * (2.0 - 2.0 / (jnp.exp(2.0 * z) + 1.0))


def _peer_experts_sc(idx, tab, h, gate, x, first, n):
    d = x.shape[1]
    info = plsc.get_sparse_core_info()
    n_core, n_worker = info.num_cores, info.num_cores * info.num_subcores
    per_worker = n // n_worker
    n_chunk = PEER_SEL // SC_EXPERTS
    hi_mask = jnp.uint32(0xFFFF0000)

    def body(idx_hbm, tab_hbm, h_hbm, gate_hbm, x_hbm, o_hbm, idx_v, gate_v, h_v, out_v, rows_v, w_v, sem):
        worker = lax.axis_index("subcore") * n_core + lax.axis_index("core")

        def gather(cb):
            return pltpu.make_async_copy(tab_hbm.at[idx_v.at[pl.ds(cb * SC_EXPERTS, SC_EXPERTS)]],
                                         rows_v.at[cb % 2], sem.at[cb % 2])

        def mix_chunk(cb):
            rows = rows_v.at[cb % 2]
            for e in range(SC_EXPERTS):
                w_v[e, :] = jnp.zeros((SC_LANES,), F32)

            def key_block(blk, carry):
                base = blk * (SC_BLOCK * SC_LANES)
                hs = [h_v[pl.ds(base + j * SC_LANES, SC_LANES)] for j in range(SC_BLOCK)]
                for e in range(SC_EXPERTS):
                    a = w_v[e, :]
                    for j in range(SC_BLOCK):
                        word = rows[e, pl.ds(base + j * SC_LANES, SC_LANES)]
                        a = a + plsc.bitcast(word << 16, F32) * hs[j]
                    w_v[e, :] = a
                return carry

            lax.fori_loop(0, SC_CHUNKS // SC_BLOCK, key_block, 0)
            for e in range(SC_EXPERTS):
                act = jnp.full((SC_LANES,), jnp.sum(w_v[e, :]), F32)
                g = plsc.load_gather(gate_v, [jnp.full((SC_LANES,), cb * SC_EXPERTS + e, jnp.int32)])
                w_v[e, :] = g * _gelu_tanh_exp(act)

            def value_block(blk, carry):
                base = blk * (SC_BLOCK * SC_LANES)
                outs = [out_v[pl.ds(base + j * SC_LANES, SC_LANES)] for j in range(SC_BLOCK)]
                for e in range(SC_EXPERTS):
                    w = w_v[e, :]
                    for j in range(SC_BLOCK):
                        word = rows[e, pl.ds(base + j * SC_LANES, SC_LANES)]
                        outs[j] = outs[j] + w * plsc.bitcast(word & hi_mask, F32)
                for j in range(SC_BLOCK):
                    out_v[pl.ds(base + j * SC_LANES, SC_LANES)] = outs[j]
                return carry

            lax.fori_loop(0, SC_CHUNKS // SC_BLOCK, value_block, 0)

        @pl.loop(0, per_worker)
        def _(i):
            row = worker * per_worker + i
            t = first + row
            pltpu.sync_copy(idx_hbm.at[t], idx_v)
            gather(0).start()
            pltpu.sync_copy(gate_hbm.at[t], gate_v)
            pltpu.sync_copy(h_hbm.at[t], h_v)
            pltpu.sync_copy(x_hbm.at[t], out_v)

            @pl.loop(0, n_chunk)
            def _(cb):
                @pl.when(cb + 1 < n_chunk)
                def _():
                    gather(cb + 1).start()

                gather(cb).wait()
                mix_chunk(cb)

            pltpu.sync_copy(out_v, o_hbm.at[row])

    return pl.kernel(
        body,
        out_type=jax.ShapeDtypeStruct((n, d), F32),
        mesh=plsc.VectorSubcoreMesh(core_axis_name="core", subcore_axis_name="subcore"),
        scratch_types=[pltpu.VMEM((PEER_SEL,), jnp.int32),
                       pltpu.VMEM((PEER_SEL,), F32),
                       pltpu.VMEM((d,), F32),
                       pltpu.VMEM((d,), F32),
                       pltpu.VMEM((2, SC_EXPERTS, d), jnp.uint32),
                       pltpu.VMEM((SC_EXPERTS, SC_LANES), F32),
                       pltpu.SemaphoreType.DMA((2,))],
        compiler_params=pltpu.CompilerParams(needs_layout_passes=False),
        name="peer_experts_sc",
    )(idx, tab, h, gate, x)


def _final_norm_kernel(x_ref, g_ref, o_ref):
    o_ref[...] = _rms(x_ref[...], g_ref[...])


def _final_norm(x, g, *, tm=512):
    t, d = x.shape
    row = pl.BlockSpec((tm, d), lambda i: (i, 0))
    return pl.pallas_call(
        _final_norm_kernel,
        grid=(t // tm,),
        in_specs=[row, pl.BlockSpec((1, d), lambda i: (0, 0))],
        out_specs=row,
        out_shape=jax.ShapeDtypeStruct((t, d), F32),
        compiler_params=_cparams("arbitrary"),
        name="final_norm",
    )(x, g.reshape(1, d))


def _mixer_and_retrieval(x, shape, p, band, rot):
    b, s, _ = shape
    t = b * s
    proj = _inproj(x, p["norm_mix"], p["w_in"]).reshape(b, s, IN_COLS)
    att = _attention(proj, band)
    conv = _short_conv(proj, p["conv_w"])
    ret = _retention(proj, p["log_g"], *rot)
    x = _outproj(att.reshape(t, ATT_WIDTH), conv.reshape(t, CONV_WIDTH),
                 ret.reshape(t, RET_WIDTH), x, p["w_out"])
    return (x,) + tuple(_peer_topk(x, p["norm_ffn"], p["peer_wq"], p["peer_keys"]))


def _expert_mix(x, h, eidx_t, gate_t, p):
    t = x.shape[0]
    t_tc = t - PEER_SC_TOKENS
    idx = jnp.transpose(eidx_t)
    x_tc = _peer_experts(idx.reshape(t // PEER_TB, PEER_TB * PEER_SEL),
                         p["peer_tab"].reshape(PEER_EXPERTS, SLAB, LANES), h, gate_t, x, t_tc)
    x_sc = _peer_experts_sc(idx, p["peer_tab"], h, jnp.transpose(gate_t), x, t_tc, PEER_SC_TOKENS)
    return jnp.concatenate([x_tc, x_sc], axis=0)


def _trunks(xs, params):
    shapes = [x.shape for x in xs]
    rots = [_rotary_tables(shape[1]) for shape in shapes]
    xs = [x.reshape(-1, D_MODEL) for x in xs]
    for l in range(DEPTH):
        p = params[l]
        staged = [_mixer_and_retrieval(x, shape, p, params["band"], rot)
                  for x, shape, rot in zip(xs, shapes, rots)]
        xs = [_expert_mix(*st, p) for st in staged]
    return tuple(_final_norm(x, params["final_norm"]).reshape(shape) for x, shape in zip(xs, shapes))


def kernel(x_prompt, x_sample, rel_bias, final_norm, norm_mix, norm_ffn, w_in, conv_w, ret_decay,
           w_out, peer_wq, peer_keys, peer_u, peer_v):
    params = {"band": _attention_band(rel_bias), "final_norm": final_norm}
    for l in range(DEPTH):
        params[l] = {
            "norm_mix": norm_mix[l],
            "norm_ffn": norm_ffn[l],
            "w_in": w_in[l].astype(BF16),
            "conv_w": conv_w[l],
            "log_g": jax.nn.log_sigmoid(ret_decay[l].astype(F32)),
            "w_out": w_out[l].astype(BF16),
            "peer_wq": peer_wq[l].astype(BF16),
            "peer_keys": peer_keys[l].reshape(2 * PEER_HEADS, PEER_NKEYS, PEER_DHALF).astype(BF16),
            "peer_tab": _pack_expert_table(peer_u[l], peer_v[l]),
        }
    return _trunks([x_prompt, x_sample], params)
```

```python
import functools
import math

import jax
import jax.numpy as jnp
import numpy as np
from jax import lax
from jax.experimental import pallas as pl
from jax.experimental.pallas import tpu as pltpu
from jax.experimental.pallas import tpu_sc as plsc

F32 = jnp.float32
BF16 = jnp.bfloat16

D_MODEL = 2048
DEPTH = 2
N_ATT_HEADS = 12
ATT_HEAD_DIM = 64
ATT_WIDTH = N_ATT_HEADS * ATT_HEAD_DIM
DILATED_PATTERNS = ((128, 1), (512, 4), (2048, 16))
REL_BUCKETS = 32
REL_MAX_DIST = 1024
CONV_WIDTH = 512
N_RET_HEADS = 6
RET_HEAD_DIM = 128
RET_WIDTH = N_RET_HEADS * RET_HEAD_DIM
RET_CHUNK = 128
ROPE_BASE = 10000.0
MIX_WIDTH = ATT_WIDTH + CONV_WIDTH + RET_WIDTH
IN_COLS = 3 * ATT_WIDTH + 3 * CONV_WIDTH + 4 * RET_WIDTH
PEER_HEADS = 8
PEER_NKEYS = 128
PEER_EXPERTS = PEER_NKEYS * PEER_NKEYS
PEER_DHALF = 128
PEER_TOPK = 16
PEER_SEL = PEER_HEADS * PEER_TOPK
RMS_EPS = 1e-6
GN_EPS = 1e-6
NEG_INF = -1e30

LANES = 128
SUBLANES = 8
VMEM_LIMIT_BYTES = 56 * 1024 * 1024

ATT_W = 64
ATT_TQ = 128
ATT_TK = ATT_TQ + 2 * ATT_W
ATT_UNROLL = 8

COL_AQ, COL_AK, COL_AV = 0, ATT_WIDTH // LANES, 2 * ATT_WIDTH // LANES
COL_CB = 3 * ATT_WIDTH // 256
COL_CC = COL_CB + CONV_WIDTH // 256
COL_CH = COL_CC + CONV_WIDTH // 256
COL_RQ = (3 * ATT_WIDTH + 3 * CONV_WIDTH) // LANES
COL_RK = COL_RQ + N_RET_HEADS
COL_RV = COL_RK + N_RET_HEADS
COL_RG = COL_RV + N_RET_HEADS


def _cparams(*sem):
    return pltpu.CompilerParams(dimension_semantics=sem, vmem_limit_bytes=VMEM_LIMIT_BYTES)


def _rms(x, g):
    ms = jnp.mean(x * x, axis=-1, keepdims=True)
    return x * lax.rsqrt(ms + RMS_EPS) * g


def _inproj_kernel(x_ref, g_ref, w_ref, o_ref):
    h = _rms(x_ref[...], g_ref[...]).astype(BF16)
    o_ref[...] = jnp.dot(h, w_ref[...], preferred_element_type=F32)


def _inproj(x, g, w, *, tm=512, tn=2304):
    t, d = x.shape
    n = w.shape[1]
    return pl.pallas_call(
        _inproj_kernel,
        grid=(n // tn, t // tm),
        in_specs=[pl.BlockSpec((tm, d), lambda j, i: (i, 0)),
                  pl.BlockSpec((1, d), lambda j, i: (0, 0)),
                  pl.BlockSpec((d, tn), lambda j, i: (0, j))],
        out_specs=pl.BlockSpec((tm, tn), lambda j, i: (i, j)),
        out_shape=jax.ShapeDtypeStruct((t, n), F32),
        compiler_params=_cparams("arbitrary", "arbitrary"),
        name="inproj",
    )(x, g.reshape(1, d), w)


def _attn_kernel(q_ref, k_ref, v_ref, band_ref, o_ref, kbuf, vbuf0, vbuf1, acc_ref, m_ref, l_ref):
    s_len = q_ref.shape[1]
    n_flat = s_len // ATT_TQ
    lane = lax.broadcasted_iota(jnp.int32, (1, LANES), 1)
    head0 = lane < ATT_HEAD_DIM
    kcol = lax.broadcasted_iota(jnp.int32, (1, ATT_TK), 1)
    zpad = jnp.zeros((ATT_W, LANES), BF16)
    one_b = jnp.ones((), BF16)

    for p, (_, dil) in enumerate(DILATED_PATTERNS):
        sub_len = s_len // dil
        n_chunk = sub_len // ATT_TQ
        res_rows = sub_len + 2 * ATT_W

        def rows(start, size):
            if dil == 1:
                return pl.ds(start, size)
            return pl.ds(start, size, stride=dil)

        def split(f):
            return f // n_chunk, f % n_chunk

        def pad_body(r, carry):
            for off in (0, ATT_W + sub_len):
                dst = pl.ds(pl.multiple_of(r * res_rows + off, ATT_W), ATT_W)
                kbuf[dst, :] = zpad
                vbuf0[dst, :] = zpad
                vbuf1[dst, :] = zpad
            return carry

        lax.fori_loop(0, dil, pad_body, 0)

        def stage_body(f, carry):
            r, c = split(f)
            src = rows(r + c * ATT_TQ * dil, ATT_TQ)
            dst = pl.ds(pl.multiple_of(r * res_rows + ATT_W + c * ATT_TQ, ATT_W), ATT_TQ)
            kbuf[dst, :] = k_ref[0, src, :].astype(BF16)
            v = v_ref[0, src, :].astype(BF16)
            vbuf0[dst, :] = jnp.where(head0, v, one_b)
            vbuf1[dst, :] = jnp.where(head0, one_b, v)
            return carry

        lax.fori_loop(0, n_flat, stage_body, 0)

        def chunk(f):
            r, c = split(f)
            qrows = rows(r + c * ATT_TQ * dil, ATT_TQ)
            q = q_ref[0, qrows, :] * (ATT_HEAD_DIM ** -0.5)
            win = pl.ds(pl.multiple_of(r * res_rows + c * ATT_TQ, ATT_W), ATT_TK)
            kw = kbuf[win, :]
            kpos = c * ATT_TQ - ATT_W + kcol
            pos_ok = (kpos >= 0) & (kpos < sub_len)
            pv, mm = [], []
            for h, vbuf in enumerate((vbuf0, vbuf1)):
                sel = head0 if h == 0 else jnp.logical_not(head0)
                qh = jnp.where(sel, q, 0.0).astype(BF16)
                s = lax.dot_general(qh, kw, (((1,), (1,)), ((), ())), preferred_element_type=F32)
                s = jnp.where(pos_ok, s + band_ref[p, h], NEG_INF)
                m = jnp.max(s, axis=1, keepdims=True)
                mm.append(m)
                pv.append(jnp.dot(jnp.exp(s - m).astype(BF16), vbuf[win, :], preferred_element_type=F32))
            acc_n = jnp.where(head0, pv[0], pv[1])
            m_n = jnp.where(head0, mm[0], mm[1])
            l_n = pltpu.roll(jnp.where(head0, pv[1], pv[0]), ATT_HEAD_DIM, 1)
            if p == 0:
                acc_ref[qrows, :] = acc_n
                m_ref[qrows, :] = m_n
                l_ref[qrows, :] = l_n
            else:
                m_o = m_ref[qrows, :]
                m_t = jnp.maximum(m_o, m_n)
                a_o = jnp.exp(m_o - m_t)
                a_n = jnp.exp(m_n - m_t)
                acc_ref[qrows, :] = acc_ref[qrows, :] * a_o + acc_n * a_n
                l_ref[qrows, :] = l_ref[qrows, :] * a_o + l_n * a_n
                m_ref[qrows, :] = m_t

        def chunks_body(g, carry):
            for u in range(ATT_UNROLL):
                chunk(g * ATT_UNROLL + u)
            return carry

        lax.fori_loop(0, n_flat // ATT_UNROLL, chunks_body, 0)

    def out_body(c, carry):
        rws = pl.ds(pl.multiple_of(c * ATT_TQ, ATT_TQ), ATT_TQ)
        o_ref[0, rws, :] = (acc_ref[rws, :] / l_ref[rws, :]).astype(o_ref.dtype)
        return carry

    lax.fori_loop(0, s_len // ATT_TQ, out_body, 0)


def _attention(proj, band):
    b, s, _ = proj.shape
    n_pair = N_ATT_HEADS // 2
    staged_rows = s + 2 * ATT_W * max(dil for _, dil in DILATED_PATTERNS)
    blk = lambda off: pl.BlockSpec((1, s, LANES), lambda bi, hp: (bi, 0, off + hp))
    return pl.pallas_call(
        _attn_kernel,
        grid=(b, n_pair),
        in_specs=[blk(COL_AQ), blk(COL_AK), blk(COL_AV),
                  pl.BlockSpec((len(DILATED_PATTERNS), 2, ATT_TQ, ATT_TK),
                               lambda bi, hp: (0, hp, 0, 0))],
        out_specs=pl.BlockSpec((1, s, LANES), lambda bi, hp: (bi, 0, hp)),
        out_shape=jax.ShapeDtypeStruct((b, s, ATT_WIDTH), BF16),
        scratch_shapes=[pltpu.VMEM((staged_rows, LANES), BF16)] * 3 + [
                        pltpu.VMEM((s, LANES), F32),
                        pltpu.VMEM((s, LANES), F32),
                        pltpu.VMEM((s, LANES), F32)],
        compiler_params=_cparams("arbitrary", "arbitrary"),
        name="dilated_attention",
    )(proj, proj, proj, band)


def _t5_bucket(rel):
    nb = REL_BUCKETS // 2
    ret = (rel > 0).astype(np.int32) * nb
    n = np.abs(rel)
    max_exact = nb // 2
    nf = np.maximum(n, 1).astype(np.float32)
    large = max_exact + (np.log(nf / max_exact) / math.log(REL_MAX_DIST / max_exact)
                         * (nb - max_exact)).astype(np.int32)
    large = np.minimum(large, nb - 1)
    return ret + np.where(n < max_exact, n, large)


def _attention_band(rel_bias):
    span = ATT_TQ - 1 + ATT_W
    rel = np.arange(-span, ATT_TK - ATT_W)
    out = []
    for _, dil in DILATED_PATTERNS:
        diag = jnp.where((np.abs(rel) <= ATT_W)[:, None], rel_bias[_t5_bucket(dil * rel)].astype(F32), NEG_INF)
        diag = jnp.transpose(diag)
        out.append(jnp.stack([diag[:, ATT_TQ - 1 - i:ATT_TQ - 1 - i + ATT_TK] for i in range(ATT_TQ)], axis=1))
    return jnp.stack(out, axis=0)


CONV_ROWS = 512


def _conv_kernel(cb_ref, cc_ref, ch_ref, w_ref, o_ref, zbuf):
    s_len = cb_ref.shape[1]
    width = cb_ref.shape[2]
    zero = jnp.zeros((SUBLANES, width), F32)
    zbuf[pl.ds(0, SUBLANES), :] = zero
    zbuf[pl.ds(SUBLANES + s_len, SUBLANES), :] = zero
    n_blk = s_len // CONV_ROWS

    for c in range(n_blk):
        r0 = c * CONV_ROWS
        zbuf[pl.ds(SUBLANES + r0, CONV_ROWS), :] = (
            cc_ref[0, pl.ds(r0, CONV_ROWS), :] * ch_ref[0, pl.ds(r0, CONV_ROWS), :])
    w0 = w_ref[0:1, :]
    w1 = w_ref[1:2, :]
    w2 = w_ref[2:3, :]

    for c in range(n_blk):
        r0 = c * CONV_ROWS
        y = (w0 * zbuf[pl.ds(SUBLANES - 1 + r0, CONV_ROWS), :]
             + w1 * zbuf[pl.ds(SUBLANES + r0, CONV_ROWS), :]
             + w2 * zbuf[pl.ds(SUBLANES + 1 + r0, CONV_ROWS), :])
        o_ref[0, pl.ds(r0, CONV_ROWS), :] = (cb_ref[0, pl.ds(r0, CONV_ROWS), :] * y).astype(o_ref.dtype)


def _short_conv(proj, conv_w):
    b, s, _ = proj.shape
    cw = 256
    blk = lambda off: pl.BlockSpec((1, s, cw), lambda bi, j: (bi, 0, off + j))
    return pl.pallas_call(
        _conv_kernel,
        grid=(b, CONV_WIDTH // cw),
        in_specs=[blk(COL_CB), blk(COL_CC), blk(COL_CH),
                  pl.BlockSpec((3, cw), lambda bi, j: (0, j))],
        out_specs=pl.BlockSpec((1, s, cw), lambda bi, j: (bi, 0, j)),
        out_shape=jax.ShapeDtypeStruct((b, s, CONV_WIDTH), BF16),
        scratch_shapes=[pltpu.VMEM((s + 2 * SUBLANES, cw), F32)],
        compiler_params=_cparams("arbitrary", "arbitrary"),
        name="short_conv",
    )(proj, proj, proj, conv_w)


RET_UNROLL = 4


def _ret_kernel(lg_ref, q_ref, k_ref, v_ref, g_ref, cos_ref, sin_ref, o_ref, qs, ks, os_f, os_b):
    s_len = q_ref.shape[1]
    c_len = RET_CHUNK
    n_chunk = s_len // c_len
    h = pl.program_id(1)
    lg_f = lg_ref[0, h]
    lg_b = lg_ref[1, h]

    def rot_body(n, carry):
        rws = pl.ds(pl.multiple_of(n * c_len, c_len), c_len)
        cs = cos_ref[rws, :]
        sn = sin_ref[rws, :]
        q = q_ref[0, rws, :]
        k = k_ref[0, rws, :]
        qs[rws, :] = q * cs + pltpu.roll(q, RET_HEAD_DIM // 2, 1) * sn
        ks[rws, :] = (k * cs + pltpu.roll(k, RET_HEAD_DIM // 2, 1) * sn) * (RET_HEAD_DIM ** -0.5)
        return carry

    lax.fori_loop(0, n_chunk, rot_body, 0)

    ri = lax.broadcasted_iota(jnp.int32, (c_len, c_len), 0)
    ci = lax.broadcasted_iota(jnp.int32, (c_len, c_len), 1)
    diff = (ri - ci).astype(F32)
    col = lax.broadcasted_iota(jnp.int32, (c_len, 1), 0).astype(F32)
    one = jnp.ones((1, 1), F32)

    dec_f = jnp.where(diff >= 0, jnp.exp(lg_f * jnp.maximum(diff, 0.0)), 0.0)
    kw_f = jnp.exp(lg_f * (c_len - 1 - col))
    qw_f = jnp.exp(lg_f * (col + 1))
    gc_f = jnp.exp(lg_f * c_len * one)
    dec_b = jnp.where(diff < 0, jnp.exp(lg_b * jnp.maximum(-diff, 0.0)), 0.0)
    kw_b = jnp.exp(lg_b * col)
    qw_b = jnp.exp(lg_b * (c_len - col))
    gc_b = jnp.exp(lg_b * c_len * one)

    def direction(n, state, dec, kw, qw, gc):
        rws = pl.ds(pl.multiple_of(n * c_len, c_len), c_len)
        q = qs[rws, :]
        k = ks[rws, :]
        v = v_ref[0, rws, :].astype(BF16)
        qb = q.astype(BF16)
        sc = lax.dot_general(qb, k.astype(BF16), (((1,), (1,)), ((), ())),
                             preferred_element_type=F32)
        o_in = jnp.dot((sc * dec).astype(BF16), v, preferred_element_type=F32)
        o_x = jnp.dot(qb, state.astype(BF16), preferred_element_type=F32) * qw
        kv = jnp.dot(jnp.transpose(k * kw).astype(BF16), v, preferred_element_type=F32)
        return rws, o_in + o_x, gc * state + kv

    def sweep_body(g, states):
        s_f, s_b = states
        for u in range(RET_UNROLL):
            n = g * RET_UNROLL + u
            rws, o, s_f = direction(n, s_f, dec_f, kw_f, qw_f, gc_f)
            os_f[rws, :] = o
            rws, o, s_b = direction(n_chunk - 1 - n, s_b, dec_b, kw_b, qw_b, gc_b)
            os_b[rws, :] = o
        return s_f, s_b

    zero_state = jnp.zeros((RET_HEAD_DIM, RET_HEAD_DIM), F32)
    lax.fori_loop(0, n_chunk // RET_UNROLL, sweep_body, (zero_state, zero_state))

    def norm_body(g, carry):
        for u in range(RET_UNROLL):
            rws = pl.ds(pl.multiple_of((g * RET_UNROLL + u) * c_len, c_len), c_len)
            o = os_f[rws, :] + os_b[rws, :]
            mu = jnp.mean(o, axis=-1, keepdims=True)
            oc = o - mu
            var = jnp.mean(oc * oc, axis=-1, keepdims=True)
            on = oc * lax.rsqrt(var + GN_EPS)
            gt = g_ref[0, rws, :]
            o_ref[0, rws, :] = (gt * jax.nn.sigmoid(gt) * on).astype(o_ref.dtype)
        return carry

    lax.fori_loop(0, n_chunk // RET_UNROLL, norm_body, 0)


def _retention(proj, log_g, cos2, sin2):
    b, s, _ = proj.shape
    blk = lambda off: pl.BlockSpec((1, s, LANES), lambda bi, h: (bi, 0, off + h))
    tab = pl.BlockSpec((s, LANES), lambda bi, h: (0, 0))
    return pl.pallas_call(
        _ret_kernel,
        grid=(b, N_RET_HEADS),
        in_specs=[pl.BlockSpec(memory_space=pltpu.SMEM),
                  blk(COL_RQ), blk(COL_RK), blk(COL_RV), blk(COL_RG), tab, tab],
        out_specs=pl.BlockSpec((1, s, LANES), lambda bi, h: (bi, 0, h)),
        out_shape=jax.ShapeDtypeStruct((b, s, RET_WIDTH), BF16),
        scratch_shapes=[pltpu.VMEM((s, LANES), F32)] * 4,
        compiler_params=_cparams("arbitrary", "arbitrary"),
        name="retention",
    )(log_g, proj, proj, proj, proj, cos2, sin2)


def _rotary_tables(s):
    d = RET_HEAD_DIM
    inv = ROPE_BASE ** (-jnp.arange(0, d, 2, dtype=F32) / d)
    ang = jnp.arange(s, dtype=F32)[:, None] * inv[None, :]
    cos, sin = jnp.cos(ang), jnp.sin(ang)
    return jnp.concatenate([cos, cos], axis=-1), jnp.concatenate([-sin, sin], axis=-1)


def _outproj_kernel(att_ref, conv_ref, ret_ref, x_ref, w_ref, o_ref):
    acc = jnp.dot(att_ref[...], w_ref[pl.ds(0, ATT_WIDTH), :], preferred_element_type=F32)
    acc += jnp.dot(conv_ref[...], w_ref[pl.ds(ATT_WIDTH, CONV_WIDTH), :], preferred_element_type=F32)
    acc += jnp.dot(ret_ref[...], w_ref[pl.ds(ATT_WIDTH + CONV_WIDTH, RET_WIDTH), :],
                   preferred_element_type=F32)
    o_ref[...] = x_ref[...] + acc


def _outproj(att, conv, ret, x, w, *, tm=512):
    t, d = x.shape
    row = lambda width: pl.BlockSpec((tm, width), lambda i: (i, 0))
    return pl.pallas_call(
        _outproj_kernel,
        grid=(t // tm,),
        in_specs=[row(ATT_WIDTH), row(CONV_WIDTH), row(RET_WIDTH), row(d),
                  pl.BlockSpec((MIX_WIDTH, d), lambda i: (0, 0))],
        out_specs=row(d),
        out_shape=jax.ShapeDtypeStruct((t, d), F32),
        compiler_params=_cparams("arbitrary"),
        name="outproj",
    )(att, conv, ret, x, w)


def _top16(s, payload=None):
    n = s.shape[0]
    row = lax.broadcasted_iota(jnp.int32, s.shape, 0).astype(F32)
    vals, picks = [], []
    for _ in range(PEER_TOPK):
        m = jnp.max(s, axis=0, keepdims=True)
        idx = jnp.min(jnp.where(s == m, row, float(n)), axis=0, keepdims=True)
        hit = row == idx
        vals.append(m)
        if payload is None:
            picks.append(idx)
        else:
            picks.append(jnp.max(jnp.where(hit, payload, -1.0), axis=0, keepdims=True))
        s = jnp.where(hit, -jnp.inf, s)
    return vals, picks


PAIR_COUNTS = tuple(PEER_TOPK // (a + 1) for a in range(PEER_TOPK))
N_PAIR = sum(PAIR_COUNTS)
N_PAIR_PAD = -(-N_PAIR // SUBLANES) * SUBLANES


def _peer_topk_kernel(x_ref, g_ref, wq_ref, keys_ref, h_ref, eidx_ref, gate_ref, sv_s, si_s, cand_s, eid_s):
    h = _rms(x_ref[...], g_ref[...])
    h_ref[...] = h
    q = jnp.dot(h.astype(BF16), wq_ref[...], preferred_element_type=F32).astype(BF16)
    for grp in range(2 * PEER_HEADS):
        s = lax.dot_general(keys_ref[grp], q[:, grp * PEER_DHALF:(grp + 1) * PEER_DHALF],
                            (((1,), (1,)), ((), ())), preferred_element_type=F32)
        vals, idxs = _top16(s)
        for j in range(PEER_TOPK):
            sv_s[grp, pl.ds(j, 1), :] = vals[j]
            si_s[grp, pl.ds(j, 1), :] = idxs[j]
    tm = cand_s.shape[1]
    cand_s[pl.ds(N_PAIR_PAD - SUBLANES, SUBLANES), :] = jnp.full((SUBLANES, tm), -jnp.inf, F32)
    eid_s[pl.ds(N_PAIR_PAD - SUBLANES, SUBLANES), :] = jnp.full((SUBLANES, tm), -1.0, F32)
    for hd in range(PEER_HEADS):
        sv0, sv1 = sv_s[2 * hd], sv_s[2 * hd + 1]
        si0, si1 = si_s[2 * hd], si_s[2 * hd + 1]
        off = 0
        for a, nb in enumerate(PAIR_COUNTS):
            cand_s[pl.ds(off, nb), :] = sv0[a:a + 1, :] + sv1[0:nb, :]
            eid_s[pl.ds(off, nb), :] = si0[a:a + 1, :] * float(PEER_NKEYS) + si1[0:nb, :]
            off += nb
        vals, picks = _top16(cand_s[...], eid_s[...])
        cv = jnp.concatenate(vals, axis=0)
        e = jnp.exp(cv - vals[0])
        gate_ref[pl.ds(hd * PEER_TOPK, PEER_TOPK), :] = e / jnp.sum(e, axis=0, keepdims=True)
        eidx_ref[pl.ds(hd * PEER_TOPK, PEER_TOPK), :] = jnp.concatenate(picks, axis=0).astype(jnp.int32)


def _peer_topk(x, g, wq, keys, *, tm=256):
    t, d = x.shape
    return pl.pallas_call(
        _peer_topk_kernel,
        grid=(t // tm,),
        in_specs=[pl.BlockSpec((tm, d), lambda i: (i, 0)),
                  pl.BlockSpec((1, d), lambda i: (0, 0)),
                  pl.BlockSpec((d, 2 * PEER_HEADS * PEER_DHALF), lambda i: (0, 0)),
                  pl.BlockSpec((2 * PEER_HEADS, PEER_NKEYS, PEER_DHALF), lambda i: (0, 0, 0))],
        out_specs=[pl.BlockSpec((tm, d), lambda i: (i, 0)),
                   pl.BlockSpec((PEER_SEL, tm), lambda i: (0, i)),
                   pl.BlockSpec((PEER_SEL, tm), lambda i: (0, i))],
        out_shape=[jax.ShapeDtypeStruct((t, d), F32),
                   jax.ShapeDtypeStruct((PEER_SEL, t), jnp.int32),
                   jax.ShapeDtypeStruct((PEER_SEL, t), F32)],
        scratch_shapes=[pltpu.VMEM((2 * PEER_HEADS, PEER_TOPK, tm), F32),
                        pltpu.VMEM((2 * PEER_HEADS, PEER_TOPK, tm), F32),
                        pltpu.VMEM((N_PAIR_PAD, tm), F32),
                        pltpu.VMEM((N_PAIR_PAD, tm), F32)],
        compiler_params=_cparams("arbitrary"),
        name="peer_topk",
    )(x, g.reshape(1, d), wq, keys)


PEER_TB = 512
N_SLOT = 8
LOOKAHEAD = N_SLOT - 1
SLAB = D_MODEL // LANES
EXPERT_GROUP = 16
N_GROUP = PEER_SEL // EXPERT_GROUP
SEGMENT = 8
N_SEGMENT = PEER_SEL // SEGMENT
ISSUE_PER_SEGMENT = PEER_SEL // (2 * N_SEGMENT)


def _gelu_tanh(x):
    return 0.5 * x * (1.0 + jnp.tanh(math.sqrt(2.0 / math.pi) * (x + 0.044715 * (x * x * x))))


def _peer_expert_kernel(idx_hbm, tab_hbm, sel_ref, h_ref, gate_ref, x_ref, o_ref,
                        idx_s, gbuf, pbuf, wsc, isem, gsem):
    i = pl.program_id(0)
    icp = pltpu.make_async_copy(idx_hbm.at[i], idx_s, isem)
    icp.start()
    icp.wait()

    def issue(t, slot, j0, n):
        row = idx_s.at[pl.ds(t * PEER_SEL, PEER_SEL)]
        for j in range(j0, j0 + n):
            pltpu.make_async_copy(tab_hbm.at[row[j]], gbuf.at[slot, j],
                                  gsem.at[slot]).start(priority=j % 2)

    def wait(slot):
        pltpu.make_async_copy(tab_hbm.at[pl.ds(0, PEER_SEL)], gbuf.at[slot], gsem.at[slot]).wait()

    lane_t = lax.broadcasted_iota(jnp.int32, (1, PEER_TB), 1)
    hi_mask = jnp.uint32(0xFFFF0000)

    def load_segment(slot, seg):
        return [gbuf[slot, seg * SEGMENT + k, half * SUBLANES:(half + 1) * SUBLANES, :]
                for k in range(SEGMENT) for half in range(2)]

    def key_side(t, slot, issue_args=None):
        x3 = h_ref[t]
        x0, x1 = x3[0:SUBLANES], x3[SUBLANES:SLAB]
        raw = load_segment(slot, 0)
        parts = []
        for seg in range(N_SEGMENT):
            nxt = load_segment(slot, seg + 1) if seg + 1 < N_SEGMENT else None
            if issue_args is not None:
                issue(*issue_args, seg * ISSUE_PER_SEGMENT, ISSUE_PER_SEGMENT)
            for k in range(SEGMENT):
                parts.append(pltpu.bitcast(raw[2 * k] << 16, F32) * x0
                             + pltpu.bitcast(raw[2 * k + 1] << 16, F32) * x1)
            if len(parts) == EXPERT_GROUP:
                grp = seg * SEGMENT // EXPERT_GROUP
                pbuf[pl.ds(grp * LANES, LANES), :] = jnp.concatenate(parts, axis=0).astype(BF16)
                parts = []
            raw = nxt

    def weight_inputs(t):
        q = jnp.dot(sel_ref[...], pbuf[...], preferred_element_type=F32)
        gate = jnp.sum(jnp.where(lane_t == t, gate_ref[...], 0.0), axis=1, keepdims=True)
        return q, gate

    def store_weights(q, gate):
        act = jnp.sum(q, axis=1, keepdims=True)
        wsc[...] = jnp.broadcast_to(gate * _gelu_tanh(act), (PEER_SEL, LANES))

    def value_side(t, slot, issue_args=None):
        acc = [jnp.zeros((SUBLANES, LANES), F32) for _ in range(4)]

        def load_weights(seg):
            return [wsc[pl.ds(seg * SEGMENT + k, 1), :] for k in range(SEGMENT)]

        raw, wts = load_segment(slot, 0), load_weights(0)
        for seg in range(N_SEGMENT):
            last = seg + 1 == N_SEGMENT
            nxt, wnxt = (None, None) if last else (load_segment(slot, seg + 1), load_weights(seg + 1))
            if issue_args is not None:
                issue(*issue_args, (N_SEGMENT + seg) * ISSUE_PER_SEGMENT, ISSUE_PER_SEGMENT)
            for k in range(SEGMENT):
                v0 = pltpu.bitcast(raw[2 * k] & hi_mask, F32)
                v1 = pltpu.bitcast(raw[2 * k + 1] & hi_mask, F32)
                acc[2 * (k % 2)] = acc[2 * (k % 2)] + wts[k] * v0
                acc[2 * (k % 2) + 1] = acc[2 * (k % 2) + 1] + wts[k] * v1
            raw, wts = nxt, wnxt
        o_ref[t] = x_ref[t] + jnp.concatenate([acc[0] + acc[2], acc[1] + acc[3]], axis=0)

    def prologue_body(n, carry):
        k = n // (PEER_SEL // SEGMENT)
        j = (n % (PEER_SEL // SEGMENT)) * SEGMENT
        for b in range(SEGMENT):
            pltpu.make_async_copy(tab_hbm.at[idx_s[k * PEER_SEL + j + b]], gbuf.at[k, j + b],
                                  gsem.at[k]).start(priority=b % 2)
        return carry

    lax.fori_loop(0, LOOKAHEAD * (PEER_SEL // SEGMENT), prologue_body, 0)
    wait(0)
    key_side(0, 0)
    store_weights(*weight_inputs(0))
    wait(1)
    key_side(1, 1)

    def token_body(t, carry):
        gather = (jnp.minimum(t + LOOKAHEAD, PEER_TB - 1), (t + LOOKAHEAD) % N_SLOT)
        wait((t + 2) % N_SLOT)
        q, gate = weight_inputs(t + 1)
        key_side(t + 2, (t + 2) % N_SLOT, gather)
        value_side(t, t % N_SLOT, gather)
        store_weights(q, gate)
        return carry

    n_here = jnp.minimum(PEER_TB, pl.num_programs(0) * PEER_TB - i * PEER_TB)
    lax.fori_loop(0, n_here - 2, token_body, 0)
    value_side(PEER_TB - 2, (PEER_TB - 2) % N_SLOT)
    store_weights(*weight_inputs(PEER_TB - 1))
    value_side(PEER_TB - 1, (PEER_TB - 1) % N_SLOT)
    for k in range(LOOKAHEAD - 2):
        wait(k)


def _peer_experts(idx, tab, h, gate_t, x, n_tokens):
    t, d = x.shape
    n_blk = n_tokens // PEER_TB
    slabs = pl.BlockSpec((PEER_TB, SLAB, LANES), lambda i: (i, 0, 0))
    sel = (jnp.arange(PEER_SEL)[:, None] == jnp.arange(PEER_SEL * SUBLANES)[None, :] // SUBLANES).astype(BF16)
    out = pl.pallas_call(
        _peer_expert_kernel,
        grid=(n_blk,),
        in_specs=[pl.BlockSpec(memory_space=pl.ANY),
                  pl.BlockSpec(memory_space=pl.ANY),
                  pl.BlockSpec((PEER_SEL, PEER_SEL * SUBLANES), lambda i: (0, 0)),
                  slabs,
                  pl.BlockSpec((PEER_SEL, PEER_TB), lambda i: (0, i)),
                  slabs],
        out_specs=slabs,
        out_shape=jax.ShapeDtypeStruct((n_tokens, SLAB, LANES), F32),
        scratch_shapes=[pltpu.SMEM((PEER_TB * PEER_SEL,), jnp.int32),
                        pltpu.VMEM((N_SLOT, PEER_SEL, SLAB, LANES), jnp.uint32),
                        pltpu.VMEM((PEER_SEL * SUBLANES, LANES), BF16),
                        pltpu.VMEM((PEER_SEL, LANES), F32),
                        pltpu.SemaphoreType.DMA,
                        pltpu.SemaphoreType.DMA((N_SLOT,))],
        compiler_params=_cparams("arbitrary"),
        name="peer_experts",
    )(idx, tab, sel, h.reshape(t, SLAB, LANES), gate_t, x.reshape(t, SLAB, LANES))
    return out.reshape(n_tokens, d)


def _pack_expert_table(u, v):
    ub = lax.bitcast_convert_type(u.astype(BF16), jnp.uint16).astype(jnp.uint32)
    vb = lax.bitcast_convert_type(v.astype(BF16), jnp.uint16).astype(jnp.uint32)
    return ub | (vb << 16)


SC_LANES = 16
SC_CHUNKS = D_MODEL // SC_LANES
SC_EXPERTS = 16
SC_BLOCK = 32
SC_PARTIALS = 4
PEER_SC_TOKENS = 2560
PEER_SC_TOKENS_LAST = 2048


def _gelu_tanh_exp(x):
    z = math.sqrt(2.0 / math.pi) * (x + 0.044715 * (x * x * x))
    return 0.5 * x * (2.0 - 2.0 / (jnp.exp(2.0 * z) + 1.0))


def _peer_experts_sc(idx, tab, h, gate, x, first, n):
    d = x.shape[1]
    info = plsc.get_sparse_core_info()
    n_core, n_worker = info.num_cores, info.num_cores * info.num_subcores
    per_worker = n // n_worker
    n_chunk = PEER_SEL // SC_EXPERTS
    hi_mask = jnp.uint32(0xFFFF0000)

    def body(idx_hbm, tab_hbm, h_hbm, gate_hbm, x_hbm, o_hbm, idx_v, gate_v, h_v, out_v, rows_v, w_v, sem):
        worker = lax.axis_index("subcore") * n_core + lax.axis_index("core")

        def gather(cb):
            return pltpu.make_async_copy(tab_hbm.at[idx_v.at[pl.ds(cb * SC_EXPERTS, SC_EXPERTS)]],
                                         rows_v.at[cb % 2], sem.at[cb % 2])

        def mix_chunk(cb):
            rows = rows_v.at[cb % 2]
            for e in range(SC_EXPERTS):
                w_v[e, :] = jnp.zeros((SC_LANES,), F32)

            def key_block(blk, carry):
                base = blk * (SC_BLOCK * SC_LANES)
                hs = [h_v[pl.ds(base + j * SC_LANES, SC_LANES)] for j in range(SC_BLOCK)]
                for e in range(SC_EXPERTS):
                    part = [w_v[e, :]] + [None] * (SC_PARTIALS - 1)
                    for j in range(SC_BLOCK):
                        word = rows[e, pl.ds(base + j * SC_LANES, SC_LANES)]
                        term = plsc.bitcast(word << 16, F32) * hs[j]
                        k = j % SC_PARTIALS
                        part[k] = term if part[k] is None else part[k] + term
                    w_v[e, :] = functools.reduce(lambda a, b: a + b, part)
                return carry

            lax.fori_loop(0, SC_CHUNKS // SC_BLOCK, key_block, 0)
            for e in range(SC_EXPERTS):
                act = jnp.full((SC_LANES,), jnp.sum(w_v[e, :]), F32)
                g = plsc.load_gather(gate_v, [jnp.full((SC_LANES,), cb * SC_EXPERTS + e, jnp.int32)])
                w_v[e, :] = g * _gelu_tanh_exp(act)

            def value_block(blk, carry):
                base = blk * (SC_BLOCK * SC_LANES)
                outs = [out_v[pl.ds(base + j * SC_LANES, SC_LANES)] for j in range(SC_BLOCK)]
                for e in range(SC_EXPERTS):
                    w = w_v[e, :]
                    for j in range(SC_BLOCK):
                        word = rows[e, pl.ds(base + j * SC_LANES, SC_LANES)]
                        outs[j] = outs[j] + w * plsc.bitcast(word & hi_mask, F32)
                for j in range(SC_BLOCK):
                    out_v[pl.ds(base + j * SC_LANES, SC_LANES)] = outs[j]
                return carry

            lax.fori_loop(0, SC_CHUNKS // SC_BLOCK, value_block, 0)

        @pl.loop(0, per_worker)
        def _(i):
            row = worker * per_worker + i
            t = first + row
            pltpu.sync_copy(idx_hbm.at[t], idx_v)
            gather(0).start()
            pltpu.sync_copy(gate_hbm.at[t], gate_v)
            pltpu.sync_copy(h_hbm.at[t], h_v)
            pltpu.sync_copy(x_hbm.at[t], out_v)

            @pl.loop(0, n_chunk)
            def _(cb):
                @pl.when(cb + 1 < n_chunk)
                def _():
                    gather(cb + 1).start()

                gather(cb).wait()
                mix_chunk(cb)

            pltpu.sync_copy(out_v, o_hbm.at[row])

    return pl.kernel(
        body,
        out_type=jax.ShapeDtypeStruct((n, d), F32),
        mesh=plsc.VectorSubcoreMesh(core_axis_name="core", subcore_axis_name="subcore"),
        scratch_types=[pltpu.VMEM((PEER_SEL,), jnp.int32),
                       pltpu.VMEM((PEER_SEL,), F32),
                       pltpu.VMEM((d,), F32),
                       pltpu.VMEM((d,), F32),
                       pltpu.VMEM((2, SC_EXPERTS, d), jnp.uint32),
                       pltpu.VMEM((SC_EXPERTS, SC_LANES), F32),
                       pltpu.SemaphoreType.DMA((2,))],
        compiler_params=pltpu.CompilerParams(needs_layout_passes=False),
        name="peer_experts_sc",
    )(idx, tab, h, gate, x)


def _final_norm_kernel(x_ref, g_ref, o_ref):
    o_ref[...] = _rms(x_ref[...], g_ref[...])


def _final_norm(x, g, *, tm=512):
    t, d = x.shape
    row = pl.BlockSpec((tm, d), lambda i: (i, 0))
    return pl.pallas_call(
        _final_norm_kernel,
        grid=(t // tm,),
        in_specs=[row, pl.BlockSpec((1, d), lambda i: (0, 0))],
        out_specs=row,
        out_shape=jax.ShapeDtypeStruct((t, d), F32),
        compiler_params=_cparams("arbitrary"),
        name="final_norm",
    )(x, g.reshape(1, d))


def _mixer_and_retrieval(x, shape, p, band, rot):
    b, s, _ = shape
    t = b * s
    proj = _inproj(x, p["norm_mix"], p["w_in"]).reshape(b, s, IN_COLS)
    att = _attention(proj, band)
    conv = _short_conv(proj, p["conv_w"])
    ret = _retention(proj, p["log_g"], *rot)
    x = _outproj(att.reshape(t, ATT_WIDTH), conv.reshape(t, CONV_WIDTH),
                 ret.reshape(t, RET_WIDTH), x, p["w_out"])
    return (x,) + tuple(_peer_topk(x, p["norm_ffn"], p["peer_wq"], p["peer_keys"]))


def _expert_mix(x, h, eidx_t, gate_t, p, n_sc):
    t = x.shape[0]
    t_tc = t - n_sc
    idx = jnp.transpose(eidx_t)
    x_tc = _peer_experts(idx.reshape(t // PEER_TB, PEER_TB * PEER_SEL),
                         p["peer_tab"].reshape(PEER_EXPERTS, SLAB, LANES), h, gate_t, x, t_tc)
    x_sc = _peer_experts_sc(idx, p["peer_tab"], h, jnp.transpose(gate_t), x, t_tc, n_sc)
    return jnp.concatenate([x_tc, x_sc], axis=0)


def _trunks(xs, params):
    shapes = [x.shape for x in xs]
    rots = [_rotary_tables(shape[1]) for shape in shapes]
    xs = [x.reshape(-1, D_MODEL) for x in xs]
    for l in range(DEPTH):
        p = params[l]
        staged = [_mixer_and_retrieval(x, shape, p, params["band"], rot)
                  for x, shape, rot in zip(xs, shapes, rots)]
        shares = [PEER_SC_TOKENS_LAST if (l, i) == (DEPTH - 1, len(xs) - 1) else PEER_SC_TOKENS
                  for i in range(len(xs))]
        xs = [_expert_mix(*st, p, n_sc) for st, n_sc in zip(staged, shares)]
    return tuple(_final_norm(x, params["final_norm"]).reshape(shape) for x, shape in zip(xs, shapes))


def kernel(x_prompt, x_sample, rel_bias, final_norm, norm_mix, norm_ffn, w_in, conv_w, ret_decay,
           w_out, peer_wq, peer_keys, peer_u, peer_v):
    params = {"band": _attention_band(rel_bias), "final_norm": final_norm}
    for l in range(DEPTH):
        params[l] = {
            "norm_mix": norm_mix[l],
            "norm_ffn": norm_ffn[l],
            "w_in": w_in[l].astype(BF16),
            "conv_w": conv_w[l],
            "log_g": jax.nn.log_sigmoid(ret_decay[l].astype(F32)),
            "w_out": w_out[l].astype(BF16),
            "peer_wq": peer_wq[l].astype(BF16),
            "peer_keys": peer_keys[l].reshape(2 * PEER_HEADS, PEER_NKEYS, PEER_DHALF).astype(BF16),
            "peer_tab": _pack_expert_table(peer_u[l], peer_v[l]),
        }
    return _trunks([x_prompt, x_sample], params)
```

```python
import functools
import math

import jax
import jax.numpy as jnp
import numpy as np
from jax import lax
from jax.experimental import pallas as pl
from jax.experimental.pallas import tpu as pltpu
from jax.experimental.pallas import tpu_sc as plsc

F32 = jnp.float32
BF16 = jnp.bfloat16

D_MODEL = 2048
DEPTH = 2
N_ATT_HEADS = 12
ATT_HEAD_DIM = 64
ATT_WIDTH = N_ATT_HEADS * ATT_HEAD_DIM
DILATED_PATTERNS = ((128, 1), (512, 4), (2048, 16))
REL_BUCKETS = 32
REL_MAX_DIST = 1024
CONV_WIDTH = 512
N_RET_HEADS = 6
RET_HEAD_DIM = 128
RET_WIDTH = N_RET_HEADS * RET_HEAD_DIM
RET_CHUNK = 128
ROPE_BASE = 10000.0
MIX_WIDTH = ATT_WIDTH + CONV_WIDTH + RET_WIDTH
IN_COLS = 3 * ATT_WIDTH + 3 * CONV_WIDTH + 4 * RET_WIDTH
PEER_HEADS = 8
PEER_NKEYS = 128
PEER_EXPERTS = PEER_NKEYS * PEER_NKEYS
PEER_DHALF = 128
PEER_TOPK = 16
PEER_SEL = PEER_HEADS * PEER_TOPK
RMS_EPS = 1e-6
GN_EPS = 1e-6
NEG_INF = -1e30

LANES = 128
SUBLANES = 8
VMEM_LIMIT_BYTES = 56 * 1024 * 1024

ATT_W = 64
ATT_TQ = 128
ATT_TK = ATT_TQ + 2 * ATT_W
ATT_UNROLL = 8

COL_AQ, COL_AK, COL_AV = 0, ATT_WIDTH // LANES, 2 * ATT_WIDTH // LANES
COL_CB = 3 * ATT_WIDTH // 256
COL_CC = COL_CB + CONV_WIDTH // 256
COL_CH = COL_CC + CONV_WIDTH // 256
COL_RQ = (3 * ATT_WIDTH + 3 * CONV_WIDTH) // LANES
COL_RK = COL_RQ + N_RET_HEADS
COL_RV = COL_RK + N_RET_HEADS
COL_RG = COL_RV + N_RET_HEADS


def _cparams(*sem):
    return pltpu.CompilerParams(dimension_semantics=sem, vmem_limit_bytes=VMEM_LIMIT_BYTES)


def _rms(x, g):
    ms = jnp.mean(x * x, axis=-1, keepdims=True)
    return x * lax.rsqrt(ms + RMS_EPS) * g


def _inproj_kernel(x_ref, g_ref, w_ref, o_ref):
    h = _rms(x_ref[...], g_ref[...]).astype(BF16)
    o_ref[...] = jnp.dot(h, w_ref[...], preferred_element_type=F32)


def _inproj(x, g, w, *, tm=512, tn=2304):
    t, d = x.shape
    n = w.shape[1]
    return pl.pallas_call(
        _inproj_kernel,
        grid=(n // tn, t // tm),
        in_specs=[pl.BlockSpec((tm, d), lambda j, i: (i, 0)),
                  pl.BlockSpec((1, d), lambda j, i: (0, 0)),
                  pl.BlockSpec((d, tn), lambda j, i: (0, j))],
        out_specs=pl.BlockSpec((tm, tn), lambda j, i: (i, j)),
        out_shape=jax.ShapeDtypeStruct((t, n), F32),
        compiler_params=_cparams("arbitrary", "arbitrary"),
        name="inproj",
    )(x, g.reshape(1, d), w)


def _attn_kernel(q_ref, k_ref, v_ref, band_ref, o_ref, kbuf, vbuf0, vbuf1, acc_ref, m_ref, l_ref):
    s_len = q_ref.shape[1]
    n_flat = s_len // ATT_TQ
    lane = lax.broadcasted_iota(jnp.int32, (1, LANES), 1)
    head0 = lane < ATT_HEAD_DIM
    kcol = lax.broadcasted_iota(jnp.int32, (1, ATT_TK), 1)
    zpad = jnp.zeros((ATT_W, LANES), BF16)
    one_b = jnp.ones((), BF16)

    for p, (_, dil) in enumerate(DILATED_PATTERNS):
        sub_len = s_len // dil
        n_chunk = sub_len // ATT_TQ
        res_rows = sub_len + 2 * ATT_W

        def rows(start, size):
            if dil == 1:
                return pl.ds(start, size)
            return pl.ds(start, size, stride=dil)

        def split(f):
            return f // n_chunk, f % n_chunk

        def pad_body(r, carry):
            for off in (0, ATT_W + sub_len):
                dst = pl.ds(pl.multiple_of(r * res_rows + off, ATT_W), ATT_W)
                kbuf[dst, :] = zpad
                vbuf0[dst, :] = zpad
                vbuf1[dst, :] = zpad
            return carry

        lax.fori_loop(0, dil, pad_body, 0)

        def stage_body(f, carry):
            r, c = split(f)
            src = rows(r + c * ATT_TQ * dil, ATT_TQ)
            dst = pl.ds(pl.multiple_of(r * res_rows + ATT_W + c * ATT_TQ, ATT_W), ATT_TQ)
            kbuf[dst, :] = k_ref[0, src, :].astype(BF16)
            v = v_ref[0, src, :].astype(BF16)
            vbuf0[dst, :] = jnp.where(head0, v, one_b)
            vbuf1[dst, :] = jnp.where(head0, one_b, v)
            return carry

        lax.fori_loop(0, n_flat, stage_body, 0)

        def chunk(f):
            r, c = split(f)
            qrows = rows(r + c * ATT_TQ * dil, ATT_TQ)
            q = q_ref[0, qrows, :] * (ATT_HEAD_DIM ** -0.5)
            win = pl.ds(pl.multiple_of(r * res_rows + c * ATT_TQ, ATT_W), ATT_TK)
            kw = kbuf[win, :]
            kpos = c * ATT_TQ - ATT_W + kcol
            pos_ok = (kpos >= 0) & (kpos < sub_len)
            pv, mm = [], []
            for h, vbuf in enumerate((vbuf0, vbuf1)):
                sel = head0 if h == 0 else jnp.logical_not(head0)
                qh = jnp.where(sel, q, 0.0).astype(BF16)
                s = lax.dot_general(qh, kw, (((1,), (1,)), ((), ())), preferred_element_type=F32)
                s = jnp.where(pos_ok, s + band_ref[p, h], NEG_INF)
                m = jnp.max(s, axis=1, keepdims=True)
                mm.append(m)
                pv.append(jnp.dot(jnp.exp(s - m).astype(BF16), vbuf[win, :], preferred_element_type=F32))
            acc_n = jnp.where(head0, pv[0], pv[1])
            m_n = jnp.where(head0, mm[0], mm[1])
            l_n = pltpu.roll(jnp.where(head0, pv[1], pv[0]), ATT_HEAD_DIM, 1)
            if p == 0:
                acc_ref[qrows, :] = acc_n
                m_ref[qrows, :] = m_n
                l_ref[qrows, :] = l_n
            else:
                m_o = m_ref[qrows, :]
                m_t = jnp.maximum(m_o, m_n)
                a_o = jnp.exp(m_o - m_t)
                a_n = jnp.exp(m_n - m_t)
                acc_ref[qrows, :] = acc_ref[qrows, :] * a_o + acc_n * a_n
                l_ref[qrows, :] = l_ref[qrows, :] * a_o + l_n * a_n
                m_ref[qrows, :] = m_t

        def chunks_body(g, carry):
            for u in range(ATT_UNROLL):
                chunk(g * ATT_UNROLL + u)
            return carry

        lax.fori_loop(0, n_flat // ATT_UNROLL, chunks_body, 0)

    def out_body(c, carry):
        rws = pl.ds(pl.multiple_of(c * ATT_TQ, ATT_TQ), ATT_TQ)
        o_ref[0, rws, :] = (acc_ref[rws, :] / l_ref[rws, :]).astype(o_ref.dtype)
        return carry

    lax.fori_loop(0, s_len // ATT_TQ, out_body, 0)


def _attention(proj, band):
    b, s, _ = proj.shape
    n_pair = N_ATT_HEADS // 2
    staged_rows = s + 2 * ATT_W * max(dil for _, dil in DILATED_PATTERNS)
    blk = lambda off: pl.BlockSpec((1, s, LANES), lambda bi, hp: (bi, 0, off + hp))
    return pl.pallas_call(
        _attn_kernel,
        grid=(b, n_pair),
        in_specs=[blk(COL_AQ), blk(COL_AK), blk(COL_AV),
                  pl.BlockSpec((len(DILATED_PATTERNS), 2, ATT_TQ, ATT_TK),
                               lambda bi, hp: (0, hp, 0, 0))],
        out_specs=pl.BlockSpec((1, s, LANES), lambda bi, hp: (bi, 0, hp)),
        out_shape=jax.ShapeDtypeStruct((b, s, ATT_WIDTH), BF16),
        scratch_shapes=[pltpu.VMEM((staged_rows, LANES), BF16)] * 3 + [
                        pltpu.VMEM((s, LANES), F32),
                        pltpu.VMEM((s, LANES), F32),
                        pltpu.VMEM((s, LANES), F32)],
        compiler_params=_cparams("arbitrary", "arbitrary"),
        name="dilated_attention",
    )(proj, proj, proj, band)


def _t5_bucket(rel):
    nb = REL_BUCKETS // 2
    ret = (rel > 0).astype(np.int32) * nb
    n = np.abs(rel)
    max_exact = nb // 2
    nf = np.maximum(n, 1).astype(np.float32)
    large = max_exact + (np.log(nf / max_exact) / math.log(REL_MAX_DIST / max_exact)
                         * (nb - max_exact)).astype(np.int32)
    large = np.minimum(large, nb - 1)
    return ret + np.where(n < max_exact, n, large)


def _attention_band(rel_bias):
    span = ATT_TQ - 1 + ATT_W
    rel = np.arange(-span, ATT_TK - ATT_W)
    out = []
    for _, dil in DILATED_PATTERNS:
        diag = jnp.where((np.abs(rel) <= ATT_W)[:, None], rel_bias[_t5_bucket(dil * rel)].astype(F32), NEG_INF)
        diag = jnp.transpose(diag)
        out.append(jnp.stack([diag[:, ATT_TQ - 1 - i:ATT_TQ - 1 - i + ATT_TK] for i in range(ATT_TQ)], axis=1))
    return jnp.stack(out, axis=0)


CONV_ROWS = 512


def _conv_kernel(cb_ref, cc_ref, ch_ref, w_ref, o_ref, zbuf):
    s_len = cb_ref.shape[1]
    width = cb_ref.shape[2]
    zero = jnp.zeros((SUBLANES, width), F32)
    zbuf[pl.ds(0, SUBLANES), :] = zero
    zbuf[pl.ds(SUBLANES + s_len, SUBLANES), :] = zero
    n_blk = s_len // CONV_ROWS

    for c in range(n_blk):
        r0 = c * CONV_ROWS
        zbuf[pl.ds(SUBLANES + r0, CONV_ROWS), :] = (
            cc_ref[0, pl.ds(r0, CONV_ROWS), :] * ch_ref[0, pl.ds(r0, CONV_ROWS), :])
    w0 = w_ref[0:1, :]
    w1 = w_ref[1:2, :]
    w2 = w_ref[2:3, :]

    for c in range(n_blk):
        r0 = c * CONV_ROWS
        y = (w0 * zbuf[pl.ds(SUBLANES - 1 + r0, CONV_ROWS), :]
             + w1 * zbuf[pl.ds(SUBLANES + r0, CONV_ROWS), :]
             + w2 * zbuf[pl.ds(SUBLANES + 1 + r0, CONV_ROWS), :])
        o_ref[0, pl.ds(r0, CONV_ROWS), :] = (cb_ref[0, pl.ds(r0, CONV_ROWS), :] * y).astype(o_ref.dtype)


def _short_conv(proj, conv_w):
    b, s, _ = proj.shape
    cw = 256
    blk = lambda off: pl.BlockSpec((1, s, cw), lambda bi, j: (bi, 0, off + j))
    return pl.pallas_call(
        _conv_kernel,
        grid=(b, CONV_WIDTH // cw),
        in_specs=[blk(COL_CB), blk(COL_CC), blk(COL_CH),
                  pl.BlockSpec((3, cw), lambda bi, j: (0, j))],
        out_specs=pl.BlockSpec((1, s, cw), lambda bi, j: (bi, 0, j)),
        out_shape=jax.ShapeDtypeStruct((b, s, CONV_WIDTH), BF16),
        scratch_shapes=[pltpu.VMEM((s + 2 * SUBLANES, cw), F32)],
        compiler_params=_cparams("arbitrary", "arbitrary"),
        name="short_conv",
    )(proj, proj, proj, conv_w)


RET_UNROLL = 4


def _ret_kernel(lg_ref, q_ref, k_ref, v_ref, g_ref, cos_ref, sin_ref, o_ref, qs, ks, os_f, os_b):
    s_len = q_ref.shape[1]
    c_len = RET_CHUNK
    n_chunk = s_len // c_len
    h = pl.program_id(1)
    lg_f = lg_ref[0, h]
    lg_b = lg_ref[1, h]

    def rot_body(n, carry):
        rws = pl.ds(pl.multiple_of(n * c_len, c_len), c_len)
        cs = cos_ref[rws, :]
        sn = sin_ref[rws, :]
        q = q_ref[0, rws, :]
        k = k_ref[0, rws, :]
        qs[rws, :] = q * cs + pltpu.roll(q, RET_HEAD_DIM // 2, 1) * sn
        ks[rws, :] = (k * cs + pltpu.roll(k, RET_HEAD_DIM // 2, 1) * sn) * (RET_HEAD_DIM ** -0.5)
        return carry

    lax.fori_loop(0, n_chunk, rot_body, 0)

    ri = lax.broadcasted_iota(jnp.int32, (c_len, c_len), 0)
    ci = lax.broadcasted_iota(jnp.int32, (c_len, c_len), 1)
    diff = (ri - ci).astype(F32)
    col = lax.broadcasted_iota(jnp.int32, (c_len, 1), 0).astype(F32)
    one = jnp.ones((1, 1), F32)

    dec_f = jnp.where(diff >= 0, jnp.exp(lg_f * jnp.maximum(diff, 0.0)), 0.0)
    kw_f = jnp.exp(lg_f * (c_len - 1 - col))
    qw_f = jnp.exp(lg_f * (col + 1))
    gc_f = jnp.exp(lg_f * c_len * one)
    dec_b = jnp.where(diff < 0, jnp.exp(lg_b * jnp.maximum(-diff, 0.0)), 0.0)
    kw_b = jnp.exp(lg_b * col)
    qw_b = jnp.exp(lg_b * (c_len - col))
    gc_b = jnp.exp(lg_b * c_len * one)

    def direction(n, state, dec, kw, qw, gc):
        rws = pl.ds(pl.multiple_of(n * c_len, c_len), c_len)
        q = qs[rws, :]
        k = ks[rws, :]
        v = v_ref[0, rws, :].astype(BF16)
        qb = q.astype(BF16)
        sc = lax.dot_general(qb, k.astype(BF16), (((1,), (1,)), ((), ())),
                             preferred_element_type=F32)
        o_in = jnp.dot((sc * dec).astype(BF16), v, preferred_element_type=F32)
        o_x = jnp.dot(qb, state.astype(BF16), preferred_element_type=F32) * qw
        kv = jnp.dot(jnp.transpose(k * kw).astype(BF16), v, preferred_element_type=F32)
        return rws, o_in + o_x, gc * state + kv

    def sweep_body(g, states):
        s_f, s_b = states
        for u in range(RET_UNROLL):
            n = g * RET_UNROLL + u
            rws, o, s_f = direction(n, s_f, dec_f, kw_f, qw_f, gc_f)
            os_f[rws, :] = o
            rws, o, s_b = direction(n_chunk - 1 - n, s_b, dec_b, kw_b, qw_b, gc_b)
            os_b[rws, :] = o
        return s_f, s_b

    zero_state = jnp.zeros((RET_HEAD_DIM, RET_HEAD_DIM), F32)
    lax.fori_loop(0, n_chunk // RET_UNROLL, sweep_body, (zero_state, zero_state))

    def norm_body(g, carry):
        for u in range(RET_UNROLL):
            rws = pl.ds(pl.multiple_of((g * RET_UNROLL + u) * c_len, c_len), c_len)
            o = os_f[rws, :] + os_b[rws, :]
            mu = jnp.mean(o, axis=-1, keepdims=True)
            oc = o - mu
            var = jnp.mean(oc * oc, axis=-1, keepdims=True)
            on = oc * lax.rsqrt(var + GN_EPS)
            gt = g_ref[0, rws, :]
            o_ref[0, rws, :] = (gt * jax.nn.sigmoid(gt) * on).astype(o_ref.dtype)
        return carry

    lax.fori_loop(0, n_chunk // RET_UNROLL, norm_body, 0)


def _retention(proj, log_g, cos2, sin2):
    b, s, _ = proj.shape
    blk = lambda off: pl.BlockSpec((1, s, LANES), lambda bi, h: (bi, 0, off + h))
    tab = pl.BlockSpec((s, LANES), lambda bi, h: (0, 0))
    return pl.pallas_call(
        _ret_kernel,
        grid=(b, N_RET_HEADS),
        in_specs=[pl.BlockSpec(memory_space=pltpu.SMEM),
                  blk(COL_RQ), blk(COL_RK), blk(COL_RV), blk(COL_RG), tab, tab],
        out_specs=pl.BlockSpec((1, s, LANES), lambda bi, h: (bi, 0, h)),
        out_shape=jax.ShapeDtypeStruct((b, s, RET_WIDTH), BF16),
        scratch_shapes=[pltpu.VMEM((s, LANES), F32)] * 4,
        compiler_params=_cparams("arbitrary", "arbitrary"),
        name="retention",
    )(log_g, proj, proj, proj, proj, cos2, sin2)


def _rotary_tables(s):
    d = RET_HEAD_DIM
    inv = ROPE_BASE ** (-jnp.arange(0, d, 2, dtype=F32) / d)
    ang = jnp.arange(s, dtype=F32)[:, None] * inv[None, :]
    cos, sin = jnp.cos(ang), jnp.sin(ang)
    return jnp.concatenate([cos, cos], axis=-1), jnp.concatenate([-sin, sin], axis=-1)


def _outproj_kernel(att_ref, conv_ref, ret_ref, x_ref, w_ref, o_ref):
    acc = jnp.dot(att_ref[...], w_ref[pl.ds(0, ATT_WIDTH), :], preferred_element_type=F32)
    acc += jnp.dot(conv_ref[...], w_ref[pl.ds(ATT_WIDTH, CONV_WIDTH), :], preferred_element_type=F32)
    acc += jnp.dot(ret_ref[...], w_ref[pl.ds(ATT_WIDTH + CONV_WIDTH, RET_WIDTH), :],
                   preferred_element_type=F32)
    o_ref[...] = x_ref[...] + acc


def _outproj(att, conv, ret, x, w, *, tm=512):
    t, d = x.shape
    row = lambda width: pl.BlockSpec((tm, width), lambda i: (i, 0))
    return pl.pallas_call(
        _outproj_kernel,
        grid=(t // tm,),
        in_specs=[row(ATT_WIDTH), row(CONV_WIDTH), row(RET_WIDTH), row(d),
                  pl.BlockSpec((MIX_WIDTH, d), lambda i: (0, 0))],
        out_specs=row(d),
        out_shape=jax.ShapeDtypeStruct((t, d), F32),
        compiler_params=_cparams("arbitrary"),
        name="outproj",
    )(att, conv, ret, x, w)


def _top16(s, payload=None):
    n = s.shape[0]
    row = lax.broadcasted_iota(jnp.int32, s.shape, 0).astype(F32)
    vals, picks = [], []
    for _ in range(PEER_TOPK):
        m = jnp.max(s, axis=0, keepdims=True)
        idx = jnp.min(jnp.where(s == m, row, float(n)), axis=0, keepdims=True)
        hit = row == idx
        vals.append(m)
        if payload is None:
            picks.append(idx)
        else:
            picks.append(jnp.max(jnp.where(hit, payload, -1.0), axis=0, keepdims=True))
        s = jnp.where(hit, -jnp.inf, s)
    return vals, picks


PAIR_COUNTS = tuple(PEER_TOPK // (a + 1) for a in range(PEER_TOPK))
N_PAIR = sum(PAIR_COUNTS)
N_PAIR_PAD = -(-N_PAIR // SUBLANES) * SUBLANES


def _peer_topk_kernel(x_ref, g_ref, wq_ref, keys_ref, h_ref, eidx_ref, gate_ref, sv_s, si_s, cand_s, eid_s):
    h = _rms(x_ref[...], g_ref[...])
    h_ref[...] = h
    q = jnp.dot(h.astype(BF16), wq_ref[...], preferred_element_type=F32).astype(BF16)
    for grp in range(2 * PEER_HEADS):
        s = lax.dot_general(keys_ref[grp], q[:, grp * PEER_DHALF:(grp + 1) * PEER_DHALF],
                            (((1,), (1,)), ((), ())), preferred_element_type=F32)
        vals, idxs = _top16(s)
        for j in range(PEER_TOPK):
            sv_s[grp, pl.ds(j, 1), :] = vals[j]
            si_s[grp, pl.ds(j, 1), :] = idxs[j]
    tm = cand_s.shape[1]
    cand_s[pl.ds(N_PAIR_PAD - SUBLANES, SUBLANES), :] = jnp.full((SUBLANES, tm), -jnp.inf, F32)
    eid_s[pl.ds(N_PAIR_PAD - SUBLANES, SUBLANES), :] = jnp.full((SUBLANES, tm), -1.0, F32)
    for hd in range(PEER_HEADS):
        sv0, sv1 = sv_s[2 * hd], sv_s[2 * hd + 1]
        si0, si1 = si_s[2 * hd], si_s[2 * hd + 1]
        off = 0
        for a, nb in enumerate(PAIR_COUNTS):
            cand_s[pl.ds(off, nb), :] = sv0[a:a + 1, :] + sv1[0:nb, :]
            eid_s[pl.ds(off, nb), :] = si0[a:a + 1, :] * float(PEER_NKEYS) + si1[0:nb, :]
            off += nb
        vals, picks = _top16(cand_s[...], eid_s[...])
        cv = jnp.concatenate(vals, axis=0)
        e = jnp.exp(cv - vals[0])
        gate_ref[pl.ds(hd * PEER_TOPK, PEER_TOPK), :] = e / jnp.sum(e, axis=0, keepdims=True)
        eidx_ref[pl.ds(hd * PEER_TOPK, PEER_TOPK), :] = jnp.concatenate(picks, axis=0).astype(jnp.int32)


def _peer_topk(x, g, wq, keys, *, tm=256):
    t, d = x.shape
    return pl.pallas_call(
        _peer_topk_kernel,
        grid=(t // tm,),
        in_specs=[pl.BlockSpec((tm, d), lambda i: (i, 0)),
                  pl.BlockSpec((1, d), lambda i: (0, 0)),
                  pl.BlockSpec((d, 2 * PEER_HEADS * PEER_DHALF), lambda i: (0, 0)),
                  pl.BlockSpec((2 * PEER_HEADS, PEER_NKEYS, PEER_DHALF), lambda i: (0, 0, 0))],
        out_specs=[pl.BlockSpec((tm, d), lambda i: (i, 0)),
                   pl.BlockSpec((PEER_SEL, tm), lambda i: (0, i)),
                   pl.BlockSpec((PEER_SEL, tm), lambda i: (0, i))],
        out_shape=[jax.ShapeDtypeStruct((t, d), F32),
                   jax.ShapeDtypeStruct((PEER_SEL, t), jnp.int32),
                   jax.ShapeDtypeStruct((PEER_SEL, t), F32)],
        scratch_shapes=[pltpu.VMEM((2 * PEER_HEADS, PEER_TOPK, tm), F32),
                        pltpu.VMEM((2 * PEER_HEADS, PEER_TOPK, tm), F32),
                        pltpu.VMEM((N_PAIR_PAD, tm), F32),
                        pltpu.VMEM((N_PAIR_PAD, tm), F32)],
        compiler_params=_cparams("arbitrary"),
        name="peer_topk",
    )(x, g.reshape(1, d), wq, keys)


PEER_TB = 512
N_SLOT = 8
LOOKAHEAD = N_SLOT - 1
SLAB = D_MODEL // LANES
EXPERT_GROUP = 16
N_GROUP = PEER_SEL // EXPERT_GROUP
SEGMENT = 8
N_SEGMENT = PEER_SEL // SEGMENT
ISSUE_PER_SEGMENT = PEER_SEL // (2 * N_SEGMENT)


def _gelu_tanh(x):
    return 0.5 * x * (1.0 + jnp.tanh(math.sqrt(2.0 / math.pi) * (x + 0.044715 * (x * x * x))))


def _peer_expert_kernel(idx_hbm, tab_hbm, sel_ref, h_ref, gate_ref, x_ref, o_ref,
                        idx_s, gbuf, pbuf, wsc, isem, gsem):
    i = pl.program_id(0)
    icp = pltpu.make_async_copy(idx_hbm.at[i], idx_s, isem)
    icp.start()
    icp.wait()

    def issue(t, slot, j0, n):
        row = idx_s.at[pl.ds(t * PEER_SEL, PEER_SEL)]
        for j in range(j0, j0 + n):
            pltpu.make_async_copy(tab_hbm.at[row[j]], gbuf.at[slot, j],
                                  gsem.at[slot]).start(priority=j % 2)

    def wait(slot):
        pltpu.make_async_copy(tab_hbm.at[pl.ds(0, PEER_SEL)], gbuf.at[slot], gsem.at[slot]).wait()

    lane_t = lax.broadcasted_iota(jnp.int32, (1, PEER_TB), 1)
    hi_mask = jnp.uint32(0xFFFF0000)

    def load_segment(slot, seg):
        return [gbuf[slot, seg * SEGMENT + k, half * SUBLANES:(half + 1) * SUBLANES, :]
                for k in range(SEGMENT) for half in range(2)]

    def key_side(t, slot, issue_args=None):
        x3 = h_ref[t]
        x0, x1 = x3[0:SUBLANES], x3[SUBLANES:SLAB]
        raw = load_segment(slot, 0)
        parts = []
        for seg in range(N_SEGMENT):
            nxt = load_segment(slot, seg + 1) if seg + 1 < N_SEGMENT else None
            if issue_args is not None:
                issue(*issue_args, seg * ISSUE_PER_SEGMENT, ISSUE_PER_SEGMENT)
            for k in range(SEGMENT):
                parts.append(pltpu.bitcast(raw[2 * k] << 16, F32) * x0
                             + pltpu.bitcast(raw[2 * k + 1] << 16, F32) * x1)
            if len(parts) == EXPERT_GROUP:
                grp = seg * SEGMENT // EXPERT_GROUP
                pbuf[pl.ds(grp * LANES, LANES), :] = jnp.concatenate(parts, axis=0).astype(BF16)
                parts = []
            raw = nxt

    def weight_inputs(t):
        q = jnp.dot(sel_ref[...], pbuf[...], preferred_element_type=F32)
        gate = jnp.sum(jnp.where(lane_t == t, gate_ref[...], 0.0), axis=1, keepdims=True)
        return q, gate

    def store_weights(q, gate):
        act = jnp.sum(q, axis=1, keepdims=True)
        wsc[...] = jnp.broadcast_to(gate * _gelu_tanh(act), (PEER_SEL, LANES))

    def value_side(t, slot, issue_args=None):
        acc = [jnp.zeros((SUBLANES, LANES), F32) for _ in range(4)]

        def load_weights(seg):
            return [wsc[pl.ds(seg * SEGMENT + k, 1), :] for k in range(SEGMENT)]

        raw, wts = load_segment(slot, 0), load_weights(0)
        for seg in range(N_SEGMENT):
            last = seg + 1 == N_SEGMENT
            nxt, wnxt = (None, None) if last else (load_segment(slot, seg + 1), load_weights(seg + 1))
            if issue_args is not None:
                issue(*issue_args, (N_SEGMENT + seg) * ISSUE_PER_SEGMENT, ISSUE_PER_SEGMENT)
            for k in range(SEGMENT):
                v0 = pltpu.bitcast(raw[2 * k] & hi_mask, F32)
                v1 = pltpu.bitcast(raw[2 * k + 1] & hi_mask, F32)
                acc[2 * (k % 2)] = acc[2 * (k % 2)] + wts[k] * v0
                acc[2 * (k % 2) + 1] = acc[2 * (k % 2) + 1] + wts[k] * v1
            raw, wts = nxt, wnxt
        o_ref[t] = x_ref[t] + jnp.concatenate([acc[0] + acc[2], acc[1] + acc[3]], axis=0)

    def prologue_body(n, carry):
        k = n // (PEER_SEL // SEGMENT)
        j = (n % (PEER_SEL // SEGMENT)) * SEGMENT
        for b in range(SEGMENT):
            pltpu.make_async_copy(tab_hbm.at[idx_s[k * PEER_SEL + j + b]], gbuf.at[k, j + b],
                                  gsem.at[k]).start(priority=b % 2)
        return carry

    lax.fori_loop(0, LOOKAHEAD * (PEER_SEL // SEGMENT), prologue_body, 0)
    wait(0)
    key_side(0, 0)
    store_weights(*weight_inputs(0))
    wait(1)
    key_side(1, 1)

    def token_body(t, carry):
        gather = (jnp.minimum(t + LOOKAHEAD, PEER_TB - 1), (t + LOOKAHEAD) % N_SLOT)
        wait((t + 2) % N_SLOT)
        q, gate = weight_inputs(t + 1)
        key_side(t + 2, (t + 2) % N_SLOT, gather)
        value_side(t, t % N_SLOT, gather)
        store_weights(q, gate)
        return carry

    n_here = jnp.minimum(PEER_TB, pl.num_programs(0) * PEER_TB - i * PEER_TB)
    lax.fori_loop(0, n_here - 2, token_body, 0)
    value_side(PEER_TB - 2, (PEER_TB - 2) % N_SLOT)
    store_weights(*weight_inputs(PEER_TB - 1))
    value_side(PEER_TB - 1, (PEER_TB - 1) % N_SLOT)
    for k in range(LOOKAHEAD - 2):
        wait(k)


def _peer_experts(idx, tab, h, gate_t, x, n_tokens):
    t, d = x.shape
    n_blk = n_tokens // PEER_TB
    slabs = pl.BlockSpec((PEER_TB, SLAB, LANES), lambda i: (i, 0, 0))
    sel = (jnp.arange(PEER_SEL)[:, None] == jnp.arange(PEER_SEL * SUBLANES)[None, :] // SUBLANES).astype(BF16)
    out = pl.pallas_call(
        _peer_expert_kernel,
        grid=(n_blk,),
        in_specs=[pl.BlockSpec(memory_space=pl.ANY),
                  pl.BlockSpec(memory_space=pl.ANY),
                  pl.BlockSpec((PEER_SEL, PEER_SEL * SUBLANES), lambda i: (0, 0)),
                  slabs,
                  pl.BlockSpec((PEER_SEL, PEER_TB), lambda i: (0, i)),
                  slabs],
        out_specs=slabs,
        out_shape=jax.ShapeDtypeStruct((n_tokens, SLAB, LANES), F32),
        scratch_shapes=[pltpu.SMEM((PEER_TB * PEER_SEL,), jnp.int32),
                        pltpu.VMEM((N_SLOT, PEER_SEL, SLAB, LANES), jnp.uint32),
                        pltpu.VMEM((PEER_SEL * SUBLANES, LANES), BF16),
                        pltpu.VMEM((PEER_SEL, LANES), F32),
                        pltpu.SemaphoreType.DMA,
                        pltpu.SemaphoreType.DMA((N_SLOT,))],
        compiler_params=_cparams("arbitrary"),
        name="peer_experts",
    )(idx, tab, sel, h.reshape(t, SLAB, LANES), gate_t, x.reshape(t, SLAB, LANES))
    return out.reshape(n_tokens, d)


PACK_ROWS = 8


def _pack_expert_table(u, v, layer):
    _, e, d = u.shape

    def body(u_hbm, v_hbm, o_hbm):
        def block(u_v, v_v, o_v):
            @pl.loop(0, PACK_ROWS)
            def _(r):
                @pl.loop(0, d // SC_LANES, step=SUBLANES)
                def _(c0):
                    for k in range(SUBLANES):
                        sl = pl.ds((c0 + k) * SC_LANES, SC_LANES)
                        pair = plsc.pack(u_v[0, r, sl], v_v[0, r, sl], format=plsc.PackFormat.INTERLEAVED)
                        o_v[r, sl] = plsc.bitcast(pair, jnp.uint32)

        rows = pl.BlockSpec((PACK_ROWS, d), lambda i: (i, 0))
        layer_rows = pl.BlockSpec((1, PACK_ROWS, d), lambda i: (layer, i, 0))
        pltpu.emit_pipeline(
            block,
            grid=(e // PACK_ROWS,),
            in_specs=[layer_rows, layer_rows],
            out_specs=[rows],
            core_axis_name=("core", "subcore"),
            dimension_semantics=(pltpu.PARALLEL,),
        )(u_hbm, v_hbm, o_hbm)

    return pl.kernel(
        body,
        out_type=jax.ShapeDtypeStruct((e, d), jnp.uint32),
        mesh=plsc.VectorSubcoreMesh(core_axis_name="core", subcore_axis_name="subcore"),
        scratch_types=[],
        compiler_params=pltpu.CompilerParams(needs_layout_passes=False),
        name="pack_expert_table_sc",
    )(u, v)


SC_LANES = 16
SC_CHUNKS = D_MODEL // SC_LANES
SC_EXPERTS = 16
SC_BLOCK = 32
SC_PARTIALS = 4
PEER_SC_TOKENS = 2560
PEER_SC_TOKENS_LAST = 2048


def _gelu_tanh_exp(x):
    z = math.sqrt(2.0 / math.pi) * (x + 0.044715 * (x * x * x))
    return 0.5 * x * (2.0 - 2.0 / (jnp.exp(2.0 * z) + 1.0))


def _peer_experts_sc(idx, tab, h, gate, x, first, n):
    d = x.shape[1]
    info = plsc.get_sparse_core_info()
    n_core, n_worker = info.num_cores, info.num_cores * info.num_subcores
    per_worker = n // n_worker
    n_chunk = PEER_SEL // SC_EXPERTS
    hi_mask = jnp.uint32(0xFFFF0000)

    def body(idx_hbm, tab_hbm, h_hbm, gate_hbm, x_hbm, o_hbm, idx_v, gate_v, h_v, out_v, rows_v, w_v, sem):
        worker = lax.axis_index("subcore") * n_core + lax.axis_index("core")

        def gather(cb):
            return pltpu.make_async_copy(tab_hbm.at[idx_v.at[pl.ds(cb * SC_EXPERTS, SC_EXPERTS)]],
                                         rows_v.at[cb % 2], sem.at[cb % 2])

        def mix_chunk(cb):
            rows = rows_v.at[cb % 2]
            for e in range(SC_EXPERTS):
                w_v[e, :] = jnp.zeros((SC_LANES,), F32)

            def key_block(blk, carry):
                base = blk * (SC_BLOCK * SC_LANES)
                hs = [h_v[pl.ds(base + j * SC_LANES, SC_LANES)] for j in range(SC_BLOCK)]
                for e in range(SC_EXPERTS):
                    part = [w_v[e, :]] + [None] * (SC_PARTIALS - 1)
                    for j in range(SC_BLOCK):
                        word = rows[e, pl.ds(base + j * SC_LANES, SC_LANES)]
                        term = plsc.bitcast(word << 16, F32) * hs[j]
                        k = j % SC_PARTIALS
                        part[k] = term if part[k] is None else part[k] + term
                    w_v[e, :] = functools.reduce(lambda a, b: a + b, part)
                return carry

            lax.fori_loop(0, SC_CHUNKS // SC_BLOCK, key_block, 0)
            for e in range(SC_EXPERTS):
                act = jnp.full((SC_LANES,), jnp.sum(w_v[e, :]), F32)
                g = plsc.load_gather(gate_v, [jnp.full((SC_LANES,), cb * SC_EXPERTS + e, jnp.int32)])
                w_v[e, :] = g * _gelu_tanh_exp(act)

            def value_block(blk, carry):
                base = blk * (SC_BLOCK * SC_LANES)
                outs = [out_v[pl.ds(base + j * SC_LANES, SC_LANES)] for j in range(SC_BLOCK)]
                for e in range(SC_EXPERTS):
                    w = w_v[e, :]
                    for j in range(SC_BLOCK):
                        word = rows[e, pl.ds(base + j * SC_LANES, SC_LANES)]
                        outs[j] = outs[j] + w * plsc.bitcast(word & hi_mask, F32)
                for j in range(SC_BLOCK):
                    out_v[pl.ds(base + j * SC_LANES, SC_LANES)] = outs[j]
                return carry

            lax.fori_loop(0, SC_CHUNKS // SC_BLOCK, value_block, 0)

        @pl.loop(0, per_worker)
        def _(i):
            row = worker * per_worker + i
            t = first + row
            pltpu.sync_copy(idx_hbm.at[t], idx_v)
            gather(0).start()
            pltpu.sync_copy(gate_hbm.at[t], gate_v)
            pltpu.sync_copy(h_hbm.at[t], h_v)
            pltpu.sync_copy(x_hbm.at[t], out_v)

            @pl.loop(0, n_chunk)
            def _(cb):
                @pl.when(cb + 1 < n_chunk)
                def _():
                    gather(cb + 1).start()

                gather(cb).wait()
                mix_chunk(cb)

            pltpu.sync_copy(out_v, o_hbm.at[row])

    return pl.kernel(
        body,
        out_type=jax.ShapeDtypeStruct((n, d), F32),
        mesh=plsc.VectorSubcoreMesh(core_axis_name="core", subcore_axis_name="subcore"),
        scratch_types=[pltpu.VMEM((PEER_SEL,), jnp.int32),
                       pltpu.VMEM((PEER_SEL,), F32),
                       pltpu.VMEM((d,), F32),
                       pltpu.VMEM((d,), F32),
                       pltpu.VMEM((2, SC_EXPERTS, d), jnp.uint32),
                       pltpu.VMEM((SC_EXPERTS, SC_LANES), F32),
                       pltpu.SemaphoreType.DMA((2,))],
        compiler_params=pltpu.CompilerParams(needs_layout_passes=False),
        name="peer_experts_sc",
    )(idx, tab, h, gate, x)


def _final_norm_kernel(x_ref, g_ref, o_ref):
    o_ref[...] = _rms(x_ref[...], g_ref[...])


def _final_norm(x, g, *, tm=512):
    t, d = x.shape
    row = pl.BlockSpec((tm, d), lambda i: (i, 0))
    return pl.pallas_call(
        _final_norm_kernel,
        grid=(t // tm,),
        in_specs=[row, pl.BlockSpec((1, d), lambda i: (0, 0))],
        out_specs=row,
        out_shape=jax.ShapeDtypeStruct((t, d), F32),
        compiler_params=_cparams("arbitrary"),
        name="final_norm",
    )(x, g.reshape(1, d))


def _mixer_and_retrieval(x, shape, p, band, rot):
    b, s, _ = shape
    t = b * s
    proj = _inproj(x, p["norm_mix"], p["w_in"]).reshape(b, s, IN_COLS)
    att = _attention(proj, band)
    conv = _short_conv(proj, p["conv_w"])
    ret = _retention(proj, p["log_g"], *rot)
    x = _outproj(att.reshape(t, ATT_WIDTH), conv.reshape(t, CONV_WIDTH),
                 ret.reshape(t, RET_WIDTH), x, p["w_out"])
    return (x,) + tuple(_peer_topk(x, p["norm_ffn"], p["peer_wq"], p["peer_keys"]))


def _expert_mix(x, h, eidx_t, gate_t, p, n_sc):
    t = x.shape[0]
    t_tc = t - n_sc
    idx = jnp.transpose(eidx_t)
    x_tc = _peer_experts(idx.reshape(t // PEER_TB, PEER_TB * PEER_SEL),
                         p["peer_tab"].reshape(PEER_EXPERTS, SLAB, LANES), h, gate_t, x, t_tc)
    x_sc = _peer_experts_sc(idx, p["peer_tab"], h, jnp.transpose(gate_t), x, t_tc, n_sc)
    return jnp.concatenate([x_tc, x_sc], axis=0)


def _trunks(xs, params):
    shapes = [x.shape for x in xs]
    rots = [_rotary_tables(shape[1]) for shape in shapes]
    xs = [x.reshape(-1, D_MODEL) for x in xs]
    for l in range(DEPTH):
        p = params[l]
        staged = [_mixer_and_retrieval(x, shape, p, params["band"], rot)
                  for x, shape, rot in zip(xs, shapes, rots)]
        shares = [PEER_SC_TOKENS_LAST if (l, i) == (DEPTH - 1, len(xs) - 1) else PEER_SC_TOKENS
                  for i in range(len(xs))]
        xs = [_expert_mix(*st, p, n_sc) for st, n_sc in zip(staged, shares)]
    return tuple(_final_norm(x, params["final_norm"]).reshape(shape) for x, shape in zip(xs, shapes))


def kernel(x_prompt, x_sample, rel_bias, final_norm, norm_mix, norm_ffn, w_in, conv_w, ret_decay,
           w_out, peer_wq, peer_keys, peer_u, peer_v):
    params = {"band": _attention_band(rel_bias), "final_norm": final_norm}
    for l in range(DEPTH):
        params[l] = {
            "norm_mix": norm_mix[l],
            "norm_ffn": norm_ffn[l],
            "w_in": w_in[l].astype(BF16),
            "conv_w": conv_w[l],
            "log_g": jax.nn.log_sigmoid(ret_decay[l].astype(F32)),
            "w_out": w_out[l].astype(BF16),
            "peer_wq": peer_wq[l].astype(BF16),
            "peer_keys": peer_keys[l].reshape(2 * PEER_HEADS, PEER_NKEYS, PEER_DHALF).astype(BF16),
            "peer_tab": _pack_expert_table(peer_u, peer_v, l),
        }
    return _trunks([x_prompt, x_sample], params)
```

```python
import functools
import math

import jax
import jax.numpy as jnp
import numpy as np
from jax import lax
from jax.experimental import pallas as pl
from jax.experimental.pallas import tpu as pltpu
from jax.experimental.pallas import tpu_sc as plsc

F32 = jnp.float32
BF16 = jnp.bfloat16

D_MODEL = 2048
DEPTH = 2
N_ATT_HEADS = 12
ATT_HEAD_DIM = 64
ATT_WIDTH = N_ATT_HEADS * ATT_HEAD_DIM
DILATED_PATTERNS = ((128, 1), (512, 4), (2048, 16))
REL_BUCKETS = 32
REL_MAX_DIST = 1024
CONV_WIDTH = 512
N_RET_HEADS = 6
RET_HEAD_DIM = 128
RET_WIDTH = N_RET_HEADS * RET_HEAD_DIM
RET_CHUNK = 128
ROPE_BASE = 10000.0
MIX_WIDTH = ATT_WIDTH + CONV_WIDTH + RET_WIDTH
IN_COLS = 3 * ATT_WIDTH + 3 * CONV_WIDTH + 4 * RET_WIDTH
PEER_HEADS = 8
PEER_NKEYS = 128
PEER_EXPERTS = PEER_NKEYS * PEER_NKEYS
PEER_DHALF = 128
PEER_TOPK = 16
PEER_SEL = PEER_HEADS * PEER_TOPK
RMS_EPS = 1e-6
GN_EPS = 1e-6
NEG_INF = -1e30

LANES = 128
SUBLANES = 8
VMEM_LIMIT_BYTES = 56 * 1024 * 1024

ATT_W = 64
ATT_TQ = 128
ATT_TK = ATT_TQ + 2 * ATT_W
ATT_UNROLL = 8

COL_AQ, COL_AK, COL_AV = 0, ATT_WIDTH // LANES, 2 * ATT_WIDTH // LANES
COL_CB = 3 * ATT_WIDTH // 256
COL_CC = COL_CB + CONV_WIDTH // 256
COL_CH = COL_CC + CONV_WIDTH // 256
COL_RQ = (3 * ATT_WIDTH + 3 * CONV_WIDTH) // LANES
COL_RK = COL_RQ + N_RET_HEADS
COL_RV = COL_RK + N_RET_HEADS
COL_RG = COL_RV + N_RET_HEADS


def _cparams(*sem):
    return pltpu.CompilerParams(dimension_semantics=sem, vmem_limit_bytes=VMEM_LIMIT_BYTES)


def _rms(x, g):
    ms = jnp.mean(x * x, axis=-1, keepdims=True)
    return x * lax.rsqrt(ms + RMS_EPS) * g


def _inproj_kernel(x_ref, g_ref, w_ref, o_ref):
    h = _rms(x_ref[...], g_ref[...]).astype(BF16)
    o_ref[...] = jnp.dot(h, w_ref[...], preferred_element_type=F32)


def _inproj(x, g, w, *, tm=512, tn=2304):
    t, d = x.shape
    n = w.shape[1]
    return pl.pallas_call(
        _inproj_kernel,
        grid=(n // tn, t // tm),
        in_specs=[pl.BlockSpec((tm, d), lambda j, i: (i, 0)),
                  pl.BlockSpec((1, d), lambda j, i: (0, 0)),
                  pl.BlockSpec((d, tn), lambda j, i: (0, j))],
        out_specs=pl.BlockSpec((tm, tn), lambda j, i: (i, j)),
        out_shape=jax.ShapeDtypeStruct((t, n), F32),
        compiler_params=_cparams("arbitrary", "arbitrary"),
        name="inproj",
    )(x, g.reshape(1, d), w)


def _attn_kernel(q_ref, k_ref, v_ref, band_ref, o_ref, kbuf, vbuf0, vbuf1, acc_ref, m_ref, l_ref):
    s_len = q_ref.shape[1]
    n_flat = s_len // ATT_TQ
    lane = lax.broadcasted_iota(jnp.int32, (1, LANES), 1)
    head0 = lane < ATT_HEAD_DIM
    kcol = lax.broadcasted_iota(jnp.int32, (1, ATT_TK), 1)
    zpad = jnp.zeros((ATT_W, LANES), BF16)
    one_b = jnp.ones((), BF16)

    for p, (_, dil) in enumerate(DILATED_PATTERNS):
        sub_len = s_len // dil
        n_chunk = sub_len // ATT_TQ
        res_rows = sub_len + 2 * ATT_W

        def rows(start, size):
            if dil == 1:
                return pl.ds(start, size)
            return pl.ds(start, size, stride=dil)

        def split(f):
            return f // n_chunk, f % n_chunk

        def pad_body(r, carry):
            for off in (0, ATT_W + sub_len):
                dst = pl.ds(pl.multiple_of(r * res_rows + off, ATT_W), ATT_W)
                kbuf[dst, :] = zpad
                vbuf0[dst, :] = zpad
                vbuf1[dst, :] = zpad
            return carry

        lax.fori_loop(0, dil, pad_body, 0)

        def stage_body(f, carry):
            r, c = split(f)
            src = rows(r + c * ATT_TQ * dil, ATT_TQ)
            dst = pl.ds(pl.multiple_of(r * res_rows + ATT_W + c * ATT_TQ, ATT_W), ATT_TQ)
            kbuf[dst, :] = k_ref[0, src, :].astype(BF16)
            v = v_ref[0, src, :].astype(BF16)
            vbuf0[dst, :] = jnp.where(head0, v, one_b)
            vbuf1[dst, :] = jnp.where(head0, one_b, v)
            return carry

        lax.fori_loop(0, n_flat, stage_body, 0)

        def chunk(f):
            r, c = split(f)
            qrows = rows(r + c * ATT_TQ * dil, ATT_TQ)
            q = q_ref[0, qrows, :] * (ATT_HEAD_DIM ** -0.5)
            win = pl.ds(pl.multiple_of(r * res_rows + c * ATT_TQ, ATT_W), ATT_TK)
            kw = kbuf[win, :]
            kpos = c * ATT_TQ - ATT_W + kcol
            pos_ok = (kpos >= 0) & (kpos < sub_len)
            pv, mm = [], []
            for h, vbuf in enumerate((vbuf0, vbuf1)):
                sel = head0 if h == 0 else jnp.logical_not(head0)
                qh = jnp.where(sel, q, 0.0).astype(BF16)
                s = lax.dot_general(qh, kw, (((1,), (1,)), ((), ())), preferred_element_type=F32)
                s = jnp.where(pos_ok, s + band_ref[p, h], NEG_INF)
                m = jnp.max(s, axis=1, keepdims=True)
                mm.append(m)
                pv.append(jnp.dot(jnp.exp(s - m).astype(BF16), vbuf[win, :], preferred_element_type=F32))
            acc_n = jnp.where(head0, pv[0], pv[1])
            m_n = jnp.where(head0, mm[0], mm[1])
            l_n = pltpu.roll(jnp.where(head0, pv[1], pv[0]), ATT_HEAD_DIM, 1)
            if p == 0:
                acc_ref[qrows, :] = acc_n
                m_ref[qrows, :] = m_n
                l_ref[qrows, :] = l_n
            else:
                m_o = m_ref[qrows, :]
                m_t = jnp.maximum(m_o, m_n)
                a_o = jnp.exp(m_o - m_t)
                a_n = jnp.exp(m_n - m_t)
                acc_ref[qrows, :] = acc_ref[qrows, :] * a_o + acc_n * a_n
                l_ref[qrows, :] = l_ref[qrows, :] * a_o + l_n * a_n
                m_ref[qrows, :] = m_t

        def chunks_body(g, carry):
            for u in range(ATT_UNROLL):
                chunk(g * ATT_UNROLL + u)
            return carry

        lax.fori_loop(0, n_flat // ATT_UNROLL, chunks_body, 0)

    def out_body(c, carry):
        rws = pl.ds(pl.multiple_of(c * ATT_TQ, ATT_TQ), ATT_TQ)
        o_ref[0, rws, :] = (acc_ref[rws, :] / l_ref[rws, :]).astype(o_ref.dtype)
        return carry

    lax.fori_loop(0, s_len // ATT_TQ, out_body, 0)


def _attention(proj, band):
    b, s, _ = proj.shape
    n_pair = N_ATT_HEADS // 2
    staged_rows = s + 2 * ATT_W * max(dil for _, dil in DILATED_PATTERNS)
    blk = lambda off: pl.BlockSpec((1, s, LANES), lambda bi, hp: (bi, 0, off + hp))
    return pl.pallas_call(
        _attn_kernel,
        grid=(b, n_pair),
        in_specs=[blk(COL_AQ), blk(COL_AK), blk(COL_AV),
                  pl.BlockSpec((len(DILATED_PATTERNS), 2, ATT_TQ, ATT_TK),
                               lambda bi, hp: (0, hp, 0, 0))],
        out_specs=pl.BlockSpec((1, s, LANES), lambda bi, hp: (bi, 0, hp)),
        out_shape=jax.ShapeDtypeStruct((b, s, ATT_WIDTH), BF16),
        scratch_shapes=[pltpu.VMEM((staged_rows, LANES), BF16)] * 3 + [
                        pltpu.VMEM((s, LANES), F32),
                        pltpu.VMEM((s, LANES), F32),
                        pltpu.VMEM((s, LANES), F32)],
        compiler_params=_cparams("arbitrary", "arbitrary"),
        name="dilated_attention",
    )(proj, proj, proj, band)


def _t5_bucket(rel):
    nb = REL_BUCKETS // 2
    ret = (rel > 0).astype(np.int32) * nb
    n = np.abs(rel)
    max_exact = nb // 2
    nf = np.maximum(n, 1).astype(np.float32)
    large = max_exact + (np.log(nf / max_exact) / math.log(REL_MAX_DIST / max_exact)
                         * (nb - max_exact)).astype(np.int32)
    large = np.minimum(large, nb - 1)
    return ret + np.where(n < max_exact, n, large)


def _attention_band(rel_bias):
    span = ATT_TQ - 1 + ATT_W
    rel = np.arange(-span, ATT_TK - ATT_W)
    out = []
    for _, dil in DILATED_PATTERNS:
        diag = jnp.where((np.abs(rel) <= ATT_W)[:, None], rel_bias[_t5_bucket(dil * rel)].astype(F32), NEG_INF)
        diag = jnp.transpose(diag)
        out.append(jnp.stack([diag[:, ATT_TQ - 1 - i:ATT_TQ - 1 - i + ATT_TK] for i in range(ATT_TQ)], axis=1))
    return jnp.stack(out, axis=0)


CONV_ROWS = 512


def _conv_kernel(cb_ref, cc_ref, ch_ref, w_ref, o_ref, zbuf):
    s_len = cb_ref.shape[1]
    width = cb_ref.shape[2]
    zero = jnp.zeros((SUBLANES, width), F32)
    zbuf[pl.ds(0, SUBLANES), :] = zero
    zbuf[pl.ds(SUBLANES + s_len, SUBLANES), :] = zero
    n_blk = s_len // CONV_ROWS

    for c in range(n_blk):
        r0 = c * CONV_ROWS
        zbuf[pl.ds(SUBLANES + r0, CONV_ROWS), :] = (
            cc_ref[0, pl.ds(r0, CONV_ROWS), :] * ch_ref[0, pl.ds(r0, CONV_ROWS), :])
    w0 = w_ref[0:1, :]
    w1 = w_ref[1:2, :]
    w2 = w_ref[2:3, :]

    for c in range(n_blk):
        r0 = c * CONV_ROWS
        y = (w0 * zbuf[pl.ds(SUBLANES - 1 + r0, CONV_ROWS), :]
             + w1 * zbuf[pl.ds(SUBLANES + r0, CONV_ROWS), :]
             + w2 * zbuf[pl.ds(SUBLANES + 1 + r0, CONV_ROWS), :])
        o_ref[0, pl.ds(r0, CONV_ROWS), :] = (cb_ref[0, pl.ds(r0, CONV_ROWS), :] * y).astype(o_ref.dtype)


def _short_conv(proj, conv_w):
    b, s, _ = proj.shape
    cw = 256
    blk = lambda off: pl.BlockSpec((1, s, cw), lambda bi, j: (bi, 0, off + j))
    return pl.pallas_call(
        _conv_kernel,
        grid=(b, CONV_WIDTH // cw),
        in_specs=[blk(COL_CB), blk(COL_CC), blk(COL_CH),
                  pl.BlockSpec((3, cw), lambda bi, j: (0, j))],
        out_specs=pl.BlockSpec((1, s, cw), lambda bi, j: (bi, 0, j)),
        out_shape=jax.ShapeDtypeStruct((b, s, CONV_WIDTH), BF16),
        scratch_shapes=[pltpu.VMEM((s + 2 * SUBLANES, cw), F32)],
        compiler_params=_cparams("arbitrary", "arbitrary"),
        name="short_conv",
    )(proj, proj, proj, conv_w)


RET_UNROLL = 4


def _ret_kernel(lg_ref, q_ref, k_ref, v_ref, g_ref, cos_ref, sin_ref, o_ref, qs, ks, os_f, os_b):
    s_len = q_ref.shape[1]
    c_len = RET_CHUNK
    n_chunk = s_len // c_len
    h = pl.program_id(1)
    lg_f = lg_ref[0, h]
    lg_b = lg_ref[1, h]

    def rot_body(n, carry):
        rws = pl.ds(pl.multiple_of(n * c_len, c_len), c_len)
        cs = cos_ref[rws, :]
        sn = sin_ref[rws, :]
        q = q_ref[0, rws, :]
        k = k_ref[0, rws, :]
        qs[rws, :] = q * cs + pltpu.roll(q, RET_HEAD_DIM // 2, 1) * sn
        ks[rws, :] = (k * cs + pltpu.roll(k, RET_HEAD_DIM // 2, 1) * sn) * (RET_HEAD_DIM ** -0.5)
        return carry

    lax.fori_loop(0, n_chunk, rot_body, 0)

    ri = lax.broadcasted_iota(jnp.int32, (c_len, c_len), 0)
    ci = lax.broadcasted_iota(jnp.int32, (c_len, c_len), 1)
    diff = (ri - ci).astype(F32)
    col = lax.broadcasted_iota(jnp.int32, (c_len, 1), 0).astype(F32)
    one = jnp.ones((1, 1), F32)

    dec_f = jnp.where(diff >= 0, jnp.exp(lg_f * jnp.maximum(diff, 0.0)), 0.0)
    kw_f = jnp.exp(lg_f * (c_len - 1 - col))
    qw_f = jnp.exp(lg_f * (col + 1))
    gc_f = jnp.exp(lg_f * c_len * one)
    dec_b = jnp.where(diff < 0, jnp.exp(lg_b * jnp.maximum(-diff, 0.0)), 0.0)
    kw_b = jnp.exp(lg_b * col)
    qw_b = jnp.exp(lg_b * (c_len - col))
    gc_b = jnp.exp(lg_b * c_len * one)

    def direction(n, state, dec, kw, qw, gc):
        rws = pl.ds(pl.multiple_of(n * c_len, c_len), c_len)
        q = qs[rws, :]
        k = ks[rws, :]
        v = v_ref[0, rws, :].astype(BF16)
        qb = q.astype(BF16)
        sc = lax.dot_general(qb, k.astype(BF16), (((1,), (1,)), ((), ())),
                             preferred_element_type=F32)
        o_in = jnp.dot((sc * dec).astype(BF16), v, preferred_element_type=F32)
        o_x = jnp.dot(qb, state.astype(BF16), preferred_element_type=F32) * qw
        kv = jnp.dot(jnp.transpose(k * kw).astype(BF16), v, preferred_element_type=F32)
        return rws, o_in + o_x, gc * state + kv

    def sweep_body(g, states):
        s_f, s_b = states
        for u in range(RET_UNROLL):
            n = g * RET_UNROLL + u
            rws, o, s_f = direction(n, s_f, dec_f, kw_f, qw_f, gc_f)
            os_f[rws, :] = o
            rws, o, s_b = direction(n_chunk - 1 - n, s_b, dec_b, kw_b, qw_b, gc_b)
            os_b[rws, :] = o
        return s_f, s_b

    zero_state = jnp.zeros((RET_HEAD_DIM, RET_HEAD_DIM), F32)
    lax.fori_loop(0, n_chunk // RET_UNROLL, sweep_body, (zero_state, zero_state))

    def norm_body(g, carry):
        for u in range(RET_UNROLL):
            rws = pl.ds(pl.multiple_of((g * RET_UNROLL + u) * c_len, c_len), c_len)
            o = os_f[rws, :] + os_b[rws, :]
            mu = jnp.mean(o, axis=-1, keepdims=True)
            oc = o - mu
            var = jnp.mean(oc * oc, axis=-1, keepdims=True)
            on = oc * lax.rsqrt(var + GN_EPS)
            gt = g_ref[0, rws, :]
            o_ref[0, rws, :] = (gt * jax.nn.sigmoid(gt) * on).astype(o_ref.dtype)
        return carry

    lax.fori_loop(0, n_chunk // RET_UNROLL, norm_body, 0)


def _retention(proj, log_g, cos2, sin2):
    b, s, _ = proj.shape
    blk = lambda off: pl.BlockSpec((1, s, LANES), lambda bi, h: (bi, 0, off + h))
    tab = pl.BlockSpec((s, LANES), lambda bi, h: (0, 0))
    return pl.pallas_call(
        _ret_kernel,
        grid=(b, N_RET_HEADS),
        in_specs=[pl.BlockSpec(memory_space=pltpu.SMEM),
                  blk(COL_RQ), blk(COL_RK), blk(COL_RV), blk(COL_RG), tab, tab],
        out_specs=pl.BlockSpec((1, s, LANES), lambda bi, h: (bi, 0, h)),
        out_shape=jax.ShapeDtypeStruct((b, s, RET_WIDTH), BF16),
        scratch_shapes=[pltpu.VMEM((s, LANES), F32)] * 4,
        compiler_params=_cparams("arbitrary", "arbitrary"),
        name="retention",
    )(log_g, proj, proj, proj, proj, cos2, sin2)


def _rotary_tables(s):
    d = RET_HEAD_DIM
    inv = ROPE_BASE ** (-jnp.arange(0, d, 2, dtype=F32) / d)
    ang = jnp.arange(s, dtype=F32)[:, None] * inv[None, :]
    cos, sin = jnp.cos(ang), jnp.sin(ang)
    return jnp.concatenate([cos, cos], axis=-1), jnp.concatenate([-sin, sin], axis=-1)


def _outproj_kernel(att_ref, conv_ref, ret_ref, x_ref, w_ref, o_ref):
    acc = jnp.dot(att_ref[...], w_ref[pl.ds(0, ATT_WIDTH), :], preferred_element_type=F32)
    acc += jnp.dot(conv_ref[...], w_ref[pl.ds(ATT_WIDTH, CONV_WIDTH), :], preferred_element_type=F32)
    acc += jnp.dot(ret_ref[...], w_ref[pl.ds(ATT_WIDTH + CONV_WIDTH, RET_WIDTH), :],
                   preferred_element_type=F32)
    o_ref[...] = x_ref[...] + acc


def _outproj(att, conv, ret, x, w, *, tm=512):
    t, d = x.shape
    row = lambda width: pl.BlockSpec((tm, width), lambda i: (i, 0))
    return pl.pallas_call(
        _outproj_kernel,
        grid=(t // tm,),
        in_specs=[row(ATT_WIDTH), row(CONV_WIDTH), row(RET_WIDTH), row(d),
                  pl.BlockSpec((MIX_WIDTH, d), lambda i: (0, 0))],
        out_specs=row(d),
        out_shape=jax.ShapeDtypeStruct((t, d), F32),
        compiler_params=_cparams("arbitrary"),
        name="outproj",
    )(att, conv, ret, x, w)


def _top16(s, payload=None):
    n = s.shape[0]
    row = lax.broadcasted_iota(jnp.int32, s.shape, 0).astype(F32)
    vals, picks = [], []
    for _ in range(PEER_TOPK):
        m = jnp.max(s, axis=0, keepdims=True)
        idx = jnp.min(jnp.where(s == m, row, float(n)), axis=0, keepdims=True)
        hit = row == idx
        vals.append(m)
        if payload is None:
            picks.append(idx)
        else:
            picks.append(jnp.max(jnp.where(hit, payload, -1.0), axis=0, keepdims=True))
        s = jnp.where(hit, -jnp.inf, s)
    return vals, picks


PAIR_COUNTS = tuple(PEER_TOPK // (a + 1) for a in range(PEER_TOPK))
N_PAIR = sum(PAIR_COUNTS)
N_PAIR_PAD = -(-N_PAIR // SUBLANES) * SUBLANES


def _peer_topk_kernel(x_ref, g_ref, wq_ref, keys_ref, h_ref, eidx_ref, gate_ref, sv_s, si_s, cand_s, eid_s):
    h = _rms(x_ref[...], g_ref[...])
    h_ref[...] = h
    q = jnp.dot(h.astype(BF16), wq_ref[...], preferred_element_type=F32).astype(BF16)
    for grp in range(2 * PEER_HEADS):
        s = lax.dot_general(keys_ref[grp], q[:, grp * PEER_DHALF:(grp + 1) * PEER_DHALF],
                            (((1,), (1,)), ((), ())), preferred_element_type=F32)
        vals, idxs = _top16(s)
        for j in range(PEER_TOPK):
            sv_s[grp, pl.ds(j, 1), :] = vals[j]
            si_s[grp, pl.ds(j, 1), :] = idxs[j]
    tm = cand_s.shape[1]
    cand_s[pl.ds(N_PAIR_PAD - SUBLANES, SUBLANES), :] = jnp.full((SUBLANES, tm), -jnp.inf, F32)
    eid_s[pl.ds(N_PAIR_PAD - SUBLANES, SUBLANES), :] = jnp.full((SUBLANES, tm), -1.0, F32)
    for hd in range(PEER_HEADS):
        sv0, sv1 = sv_s[2 * hd], sv_s[2 * hd + 1]
        si0, si1 = si_s[2 * hd], si_s[2 * hd + 1]
        off = 0
        for a, nb in enumerate(PAIR_COUNTS):
            cand_s[pl.ds(off, nb), :] = sv0[a:a + 1, :] + sv1[0:nb, :]
            eid_s[pl.ds(off, nb), :] = si0[a:a + 1, :] * float(PEER_NKEYS) + si1[0:nb, :]
            off += nb
        vals, picks = _top16(cand_s[...], eid_s[...])
        cv = jnp.concatenate(vals, axis=0)
        e = jnp.exp(cv - vals[0])
        gate_ref[pl.ds(hd * PEER_TOPK, PEER_TOPK), :] = e / jnp.sum(e, axis=0, keepdims=True)
        eidx_ref[pl.ds(hd * PEER_TOPK, PEER_TOPK), :] = jnp.concatenate(picks, axis=0).astype(jnp.int32)


def _peer_topk(x, g, wq, keys, *, tm=256):
    t, d = x.shape
    return pl.pallas_call(
        _peer_topk_kernel,
        grid=(t // tm,),
        in_specs=[pl.BlockSpec((tm, d), lambda i: (i, 0)),
                  pl.BlockSpec((1, d), lambda i: (0, 0)),
                  pl.BlockSpec((d, 2 * PEER_HEADS * PEER_DHALF), lambda i: (0, 0)),
                  pl.BlockSpec((2 * PEER_HEADS, PEER_NKEYS, PEER_DHALF), lambda i: (0, 0, 0))],
        out_specs=[pl.BlockSpec((tm, d), lambda i: (i, 0)),
                   pl.BlockSpec((PEER_SEL, tm), lambda i: (0, i)),
                   pl.BlockSpec((PEER_SEL, tm), lambda i: (0, i))],
        out_shape=[jax.ShapeDtypeStruct((t, d), F32),
                   jax.ShapeDtypeStruct((PEER_SEL, t), jnp.int32),
                   jax.ShapeDtypeStruct((PEER_SEL, t), F32)],
        scratch_shapes=[pltpu.VMEM((2 * PEER_HEADS, PEER_TOPK, tm), F32),
                        pltpu.VMEM((2 * PEER_HEADS, PEER_TOPK, tm), F32),
                        pltpu.VMEM((N_PAIR_PAD, tm), F32),
                        pltpu.VMEM((N_PAIR_PAD, tm), F32)],
        compiler_params=_cparams("arbitrary"),
        name="peer_topk",
    )(x, g.reshape(1, d), wq, keys)


PEER_TB = 512
N_SLOT = 8
LOOKAHEAD = N_SLOT - 1
SLAB = D_MODEL // LANES
EXPERT_GROUP = 16
N_GROUP = PEER_SEL // EXPERT_GROUP
SEGMENT = 8
N_SEGMENT = PEER_SEL // SEGMENT
ISSUE_PER_SEGMENT = PEER_SEL // (2 * N_SEGMENT)


def _gelu_tanh(x):
    return 0.5 * x * (1.0 + jnp.tanh(math.sqrt(2.0 / math.pi) * (x + 0.044715 * (x * x * x))))


def _peer_expert_kernel(idx_hbm, tab_hbm, sel_ref, h_ref, gate_ref, x_ref, o_ref,
                        idx_s, gbuf, pbuf, wsc, isem, gsem):
    i = pl.program_id(0)
    icp = pltpu.make_async_copy(idx_hbm.at[i], idx_s, isem)
    icp.start()
    icp.wait()

    def issue(t, slot, j0, n):
        row = idx_s.at[pl.ds(t * PEER_SEL, PEER_SEL)]
        for j in range(j0, j0 + n):
            pltpu.make_async_copy(tab_hbm.at[row[j]], gbuf.at[slot, j],
                                  gsem.at[slot]).start(priority=j % 2)

    def wait(slot):
        pltpu.make_async_copy(tab_hbm.at[pl.ds(0, PEER_SEL)], gbuf.at[slot], gsem.at[slot]).wait()

    lane_t = lax.broadcasted_iota(jnp.int32, (1, PEER_TB), 1)
    hi_mask = jnp.uint32(0xFFFF0000)

    def load_segment(slot, seg):
        return [gbuf[slot, seg * SEGMENT + k, half * SUBLANES:(half + 1) * SUBLANES, :]
                for k in range(SEGMENT) for half in range(2)]

    def key_side(t, slot, issue_args=None):
        x3 = h_ref[t]
        x0, x1 = x3[0:SUBLANES], x3[SUBLANES:SLAB]
        raw = load_segment(slot, 0)
        parts = []
        for seg in range(N_SEGMENT):
            nxt = load_segment(slot, seg + 1) if seg + 1 < N_SEGMENT else None
            if issue_args is not None:
                issue(*issue_args, seg * ISSUE_PER_SEGMENT, ISSUE_PER_SEGMENT)
            for k in range(SEGMENT):
                parts.append(pltpu.bitcast(raw[2 * k] << 16, F32) * x0
                             + pltpu.bitcast(raw[2 * k + 1] << 16, F32) * x1)
            if len(parts) == EXPERT_GROUP:
                grp = seg * SEGMENT // EXPERT_GROUP
                pbuf[pl.ds(grp * LANES, LANES), :] = jnp.concatenate(parts, axis=0).astype(BF16)
                parts = []
            raw = nxt

    def weight_inputs(t):
        q = jnp.dot(sel_ref[...], pbuf[...], preferred_element_type=F32)
        gate = jnp.sum(jnp.where(lane_t == t, gate_ref[...], 0.0), axis=1, keepdims=True)
        return q, gate

    def store_weights(q, gate):
        act = jnp.sum(q, axis=1, keepdims=True)
        wsc[...] = jnp.broadcast_to(gate * _gelu_tanh(act), (PEER_SEL, LANES))

    def value_side(t, slot, issue_args=None):
        acc = [jnp.zeros((SUBLANES, LANES), F32) for _ in range(4)]

        def load_weights(seg):
            return [wsc[pl.ds(seg * SEGMENT + k, 1), :] for k in range(SEGMENT)]

        raw, wts = load_segment(slot, 0), load_weights(0)
        for seg in range(N_SEGMENT):
            last = seg + 1 == N_SEGMENT
            nxt, wnxt = (None, None) if last else (load_segment(slot, seg + 1), load_weights(seg + 1))
            if issue_args is not None:
                issue(*issue_args, (N_SEGMENT + seg) * ISSUE_PER_SEGMENT, ISSUE_PER_SEGMENT)
            for k in range(SEGMENT):
                v0 = pltpu.bitcast(raw[2 * k] & hi_mask, F32)
                v1 = pltpu.bitcast(raw[2 * k + 1] & hi_mask, F32)
                acc[2 * (k % 2)] = acc[2 * (k % 2)] + wts[k] * v0
                acc[2 * (k % 2) + 1] = acc[2 * (k % 2) + 1] + wts[k] * v1
            raw, wts = nxt, wnxt
        o_ref[t] = x_ref[t] + jnp.concatenate([acc[0] + acc[2], acc[1] + acc[3]], axis=0)

    def prologue_body(n, carry):
        k = n // (PEER_SEL // SEGMENT)
        j = (n % (PEER_SEL // SEGMENT)) * SEGMENT
        for b in range(SEGMENT):
            pltpu.make_async_copy(tab_hbm.at[idx_s[k * PEER_SEL + j + b]], gbuf.at[k, j + b],
                                  gsem.at[k]).start(priority=b % 2)
        return carry

    lax.fori_loop(0, LOOKAHEAD * (PEER_SEL // SEGMENT), prologue_body, 0)
    wait(0)
    key_side(0, 0)
    store_weights(*weight_inputs(0))
    wait(1)
    key_side(1, 1)

    def token_body(t, carry):
        gather = (jnp.minimum(t + LOOKAHEAD, PEER_TB - 1), (t + LOOKAHEAD) % N_SLOT)
        wait((t + 2) % N_SLOT)
        q, gate = weight_inputs(t + 1)
        key_side(t + 2, (t + 2) % N_SLOT, gather)
        value_side(t, t % N_SLOT, gather)
        store_weights(q, gate)
        return carry

    n_here = jnp.minimum(PEER_TB, pl.num_programs(0) * PEER_TB - i * PEER_TB)
    lax.fori_loop(0, n_here - 2, token_body, 0)
    value_side(PEER_TB - 2, (PEER_TB - 2) % N_SLOT)
    store_weights(*weight_inputs(PEER_TB - 1))
    value_side(PEER_TB - 1, (PEER_TB - 1) % N_SLOT)
    for k in range(LOOKAHEAD - 2):
        wait(k)


def _peer_experts(idx, tab, h, gate_t, x, n_tokens):
    t, d = x.shape
    n_blk = n_tokens // PEER_TB
    slabs = pl.BlockSpec((PEER_TB, SLAB, LANES), lambda i: (i, 0, 0))
    sel = (jnp.arange(PEER_SEL)[:, None] == jnp.arange(PEER_SEL * SUBLANES)[None, :] // SUBLANES).astype(BF16)
    out = pl.pallas_call(
        _peer_expert_kernel,
        grid=(n_blk,),
        in_specs=[pl.BlockSpec(memory_space=pl.ANY),
                  pl.BlockSpec(memory_space=pl.ANY),
                  pl.BlockSpec((PEER_SEL, PEER_SEL * SUBLANES), lambda i: (0, 0)),
                  slabs,
                  pl.BlockSpec((PEER_SEL, PEER_TB), lambda i: (0, i)),
                  slabs],
        out_specs=slabs,
        out_shape=jax.ShapeDtypeStruct((n_tokens, SLAB, LANES), F32),
        scratch_shapes=[pltpu.SMEM((PEER_TB * PEER_SEL,), jnp.int32),
                        pltpu.VMEM((N_SLOT, PEER_SEL, SLAB, LANES), jnp.uint32),
                        pltpu.VMEM((PEER_SEL * SUBLANES, LANES), BF16),
                        pltpu.VMEM((PEER_SEL, LANES), F32),
                        pltpu.SemaphoreType.DMA,
                        pltpu.SemaphoreType.DMA((N_SLOT,))],
        compiler_params=_cparams("arbitrary"),
        name="peer_experts",
    )(idx, tab, sel, h.reshape(t, SLAB, LANES), gate_t, x.reshape(t, SLAB, LANES))
    return out.reshape(n_tokens, d)


PACK_ROWS = 8


def _pack_expert_table(u, v, layer):
    _, e, d = u.shape

    def body(u_hbm, v_hbm, o_hbm):
        def block(u_v, v_v, o_v):
            @pl.loop(0, PACK_ROWS)
            def _(r):
                @pl.loop(0, d // SC_LANES, step=SUBLANES)
                def _(c0):
                    for k in range(SUBLANES):
                        sl = pl.ds((c0 + k) * SC_LANES, SC_LANES)
                        pair = plsc.pack(u_v[0, r, sl], v_v[0, r, sl], format=plsc.PackFormat.INTERLEAVED)
                        o_v[r, sl] = plsc.bitcast(pair, jnp.uint32)

        rows = pl.BlockSpec((PACK_ROWS, d), lambda i: (i, 0))
        layer_rows = pl.BlockSpec((1, PACK_ROWS, d), lambda i: (layer, i, 0))
        pltpu.emit_pipeline(
            block,
            grid=(e // PACK_ROWS,),
            in_specs=[layer_rows, layer_rows],
            out_specs=[rows],
            core_axis_name=("core", "subcore"),
            dimension_semantics=(pltpu.PARALLEL,),
        )(u_hbm, v_hbm, o_hbm)

    return pl.kernel(
        body,
        out_type=jax.ShapeDtypeStruct((e, d), jnp.uint32),
        mesh=plsc.VectorSubcoreMesh(core_axis_name="core", subcore_axis_name="subcore"),
        scratch_types=[],
        compiler_params=pltpu.CompilerParams(needs_layout_passes=False),
        name="pack_expert_table_sc",
    )(u, v)


SC_LANES = 16
SC_CHUNKS = D_MODEL // SC_LANES
SC_EXPERTS = 16
SC_BLOCK = 32
SC_PARTIALS = 4
PEER_SC_TOKENS = 2560
PEER_SC_TOKENS_LAST = 2048


def _gelu_tanh_exp(x):
    z = math.sqrt(2.0 / math.pi) * (x + 0.044715 * (x * x * x))
    return 0.5 * x * (2.0 - 2.0 / (jnp.exp(2.0 * z) + 1.0))


def _peer_experts_sc(idx, tab, h, gate, x, first, n):
    d = x.shape[1]
    info = plsc.get_sparse_core_info()
    n_core, n_worker = info.num_cores, info.num_cores * info.num_subcores
    per_worker = n // n_worker
    n_chunk = PEER_SEL // SC_EXPERTS
    hi_mask = jnp.uint32(0xFFFF0000)

    def body(idx_hbm, tab_hbm, h_hbm, gate_hbm, x_hbm, o_hbm, idx_v, gate_v, h_v, out_v, rows_v, w_v, sem):
        worker = lax.axis_index("subcore") * n_core + lax.axis_index("core")

        def gather(cb):
            return pltpu.make_async_copy(tab_hbm.at[idx_v.at[pl.ds(cb * SC_EXPERTS, SC_EXPERTS)]],
                                         rows_v.at[cb % 2], sem.at[cb % 2])

        def mix_chunk(cb):
            rows = rows_v.at[cb % 2]
            for e in range(SC_EXPERTS):
                w_v[e, :] = jnp.zeros((SC_LANES,), F32)

            def key_block(blk, carry):
                base = blk * (SC_BLOCK * SC_LANES)
                hs = [h_v[pl.ds(base + j * SC_LANES, SC_LANES)] for j in range(SC_BLOCK)]
                for e in range(SC_EXPERTS):
                    part = [w_v[e, :]] + [None] * (SC_PARTIALS - 1)
                    for j in range(SC_BLOCK):
                        word = rows[e, pl.ds(base + j * SC_LANES, SC_LANES)]
                        term = plsc.bitcast(word << 16, F32) * hs[j]
                        k = j % SC_PARTIALS
                        part[k] = term if part[k] is None else part[k] + term
                    w_v[e, :] = functools.reduce(lambda a, b: a + b, part)
                return carry

            lax.fori_loop(0, SC_CHUNKS // SC_BLOCK, key_block, 0)
            for e in range(SC_EXPERTS):
                act = jnp.full((SC_LANES,), jnp.sum(w_v[e, :]), F32)
                g = plsc.load_gather(gate_v, [jnp.full((SC_LANES,), cb * SC_EXPERTS + e, jnp.int32)])
                w_v[e, :] = g * _gelu_tanh_exp(act)

            def value_block(blk, carry):
                base = blk * (SC_BLOCK * SC_LANES)
                outs = [out_v[pl.ds(base + j * SC_LANES, SC_LANES)] for j in range(SC_BLOCK)]
                for e in range(SC_EXPERTS):
                    w = w_v[e, :]
                    for j in range(SC_BLOCK):
                        word = rows[e, pl.ds(base + j * SC_LANES, SC_LANES)]
                        outs[j] = outs[j] + w * plsc.bitcast(word & hi_mask, F32)
                for j in range(SC_BLOCK):
                    out_v[pl.ds(base + j * SC_LANES, SC_LANES)] = outs[j]
                return carry

            lax.fori_loop(0, SC_CHUNKS // SC_BLOCK, value_block, 0)

        @pl.loop(0, per_worker)
        def _(i):
            row = worker * per_worker + i
            t = first + row
            pltpu.sync_copy(idx_hbm.at[t], idx_v)
            gather(0).start()
            pltpu.sync_copy(gate_hbm.at[t], gate_v)
            pltpu.sync_copy(h_hbm.at[t], h_v)
            pltpu.sync_copy(x_hbm.at[t], out_v)

            @pl.loop(0, n_chunk)
            def _(cb):
                @pl.when(cb + 1 < n_chunk)
                def _():
                    gather(cb + 1).start()

                gather(cb).wait()
                mix_chunk(cb)

            pltpu.sync_copy(out_v, o_hbm.at[row])

    return pl.kernel(
        body,
        out_type=jax.ShapeDtypeStruct((n, d), F32),
        mesh=plsc.VectorSubcoreMesh(core_axis_name="core", subcore_axis_name="subcore"),
        scratch_types=[pltpu.VMEM((PEER_SEL,), jnp.int32),
                       pltpu.VMEM((PEER_SEL,), F32),
                       pltpu.VMEM((d,), F32),
                       pltpu.VMEM((d,), F32),
                       pltpu.VMEM((2, SC_EXPERTS, d), jnp.uint32),
                       pltpu.VMEM((SC_EXPERTS, SC_LANES), F32),
                       pltpu.SemaphoreType.DMA((2,))],
        compiler_params=pltpu.CompilerParams(needs_layout_passes=False),
        name="peer_experts_sc",
    )(idx, tab, h, gate, x)


def _final_norm_kernel(x_ref, g_ref, o_ref):
    o_ref[...] = _rms(x_ref[...], g_ref[...])


def _final_norm(x, g, *, tm=512):
    t, d = x.shape
    row = pl.BlockSpec((tm, d), lambda i: (i, 0))
    return pl.pallas_call(
        _final_norm_kernel,
        grid=(t // tm,),
        in_specs=[row, pl.BlockSpec((1, d), lambda i: (0, 0))],
        out_specs=row,
        out_shape=jax.ShapeDtypeStruct((t, d), F32),
        compiler_params=_cparams("arbitrary"),
        name="final_norm",
    )(x, g.reshape(1, d))


def _mixer_and_retrieval(x, shape, p, band, rot):
    b, s, _ = shape
    t = b * s
    proj = _inproj(x, p["norm_mix"], p["w_in"]).reshape(b, s, IN_COLS)
    att = _attention(proj, band)
    conv = _short_conv(proj, p["conv_w"])
    ret = _retention(proj, p["log_g"], *rot)
    x = _outproj(att.reshape(t, ATT_WIDTH), conv.reshape(t, CONV_WIDTH),
                 ret.reshape(t, RET_WIDTH), x, p["w_out"])
    return (x,) + tuple(_peer_topk(x, p["norm_ffn"], p["peer_wq"], p["peer_keys"]))


def _expert_mix(x, h, eidx_t, gate_t, p, n_sc):
    t = x.shape[0]
    t_tc = t - n_sc
    idx = jnp.transpose(eidx_t)
    x_tc = _peer_experts(idx.reshape(t // PEER_TB, PEER_TB * PEER_SEL),
                         p["peer_tab"].reshape(PEER_EXPERTS, SLAB, LANES), h, gate_t, x, t_tc)
    x_sc = _peer_experts_sc(idx, p["peer_tab"], h, jnp.transpose(gate_t), x, t_tc, n_sc)
    return jnp.concatenate([x_tc, x_sc], axis=0)


def _trunks(xs, params):
    shapes = [x.shape for x in xs]
    rots = [_rotary_tables(shape[1]) for shape in shapes]
    xs = [x.reshape(-1, D_MODEL) for x in xs]
    for l in range(DEPTH):
        p = params[l]
        staged = [_mixer_and_retrieval(x, shape, p, params["band"], rot)
                  for x, shape, rot in zip(xs, shapes, rots)]
        p = dict(p, peer_tab=_pack_expert_table(*p["peer_uv"]))
        shares = [PEER_SC_TOKENS_LAST if (l, i) == (DEPTH - 1, len(xs) - 1) else PEER_SC_TOKENS
                  for i in range(len(xs))]
        xs = [_expert_mix(*st, p, n_sc) for st, n_sc in zip(staged, shares)]
    return tuple(_final_norm(x, params["final_norm"]).reshape(shape) for x, shape in zip(xs, shapes))


def kernel(x_prompt, x_sample, rel_bias, final_norm, norm_mix, norm_ffn, w_in, conv_w, ret_decay,
           w_out, peer_wq, peer_keys, peer_u, peer_v):
    params = {"band": _attention_band(rel_bias), "final_norm": final_norm}
    for l in range(DEPTH):
        params[l] = {
            "norm_mix": norm_mix[l],
            "norm_ffn": norm_ffn[l],
            "w_in": w_in[l].astype(BF16),
            "conv_w": conv_w[l],
            "log_g": jax.nn.log_sigmoid(ret_decay[l].astype(F32)),
            "w_out": w_out[l].astype(BF16),
            "peer_wq": peer_wq[l].astype(BF16),
            "peer_keys": peer_keys[l].reshape(2 * PEER_HEADS, PEER_NKEYS, PEER_DHALF).astype(BF16),
            "peer_uv": (peer_u, peer_v, l),
        }
    return _trunks([x_prompt, x_sample], params)
```

```python
import functools
import math

import jax
import jax.numpy as jnp
import numpy as np
from jax import lax
from jax.experimental import pallas as pl
from jax.experimental.pallas import tpu as pltpu
from jax.experimental.pallas import tpu_sc as plsc

F32 = jnp.float32
BF16 = jnp.bfloat16

D_MODEL = 2048
DEPTH = 2
N_ATT_HEADS = 12
ATT_HEAD_DIM = 64
ATT_WIDTH = N_ATT_HEADS * ATT_HEAD_DIM
DILATED_PATTERNS = ((128, 1), (512, 4), (2048, 16))
REL_BUCKETS = 32
REL_MAX_DIST = 1024
CONV_WIDTH = 512
N_RET_HEADS = 6
RET_HEAD_DIM = 128
RET_WIDTH = N_RET_HEADS * RET_HEAD_DIM
RET_CHUNK = 128
ROPE_BASE = 10000.0
MIX_WIDTH = ATT_WIDTH + CONV_WIDTH + RET_WIDTH
IN_COLS = 3 * ATT_WIDTH + 3 * CONV_WIDTH + 4 * RET_WIDTH
PEER_HEADS = 8
PEER_NKEYS = 128
PEER_EXPERTS = PEER_NKEYS * PEER_NKEYS
PEER_DHALF = 128
PEER_TOPK = 16
PEER_SEL = PEER_HEADS * PEER_TOPK
RMS_EPS = 1e-6
GN_EPS = 1e-6
NEG_INF = -1e30

LANES = 128
SUBLANES = 8
VMEM_LIMIT_BYTES = 56 * 1024 * 1024

ATT_W = 64
ATT_TQ = 128
ATT_TK = ATT_TQ + 2 * ATT_W
ATT_UNROLL = 8

COL_AQ, COL_AK, COL_AV = 0, ATT_WIDTH // LANES, 2 * ATT_WIDTH // LANES
COL_CB = 3 * ATT_WIDTH // 256
COL_CC = COL_CB + CONV_WIDTH // 256
COL_CH = COL_CC + CONV_WIDTH // 256
COL_RQ = (3 * ATT_WIDTH + 3 * CONV_WIDTH) // LANES
COL_RK = COL_RQ + N_RET_HEADS
COL_RV = COL_RK + N_RET_HEADS
COL_RG = COL_RV + N_RET_HEADS


def _cparams(*sem):
    return pltpu.CompilerParams(dimension_semantics=sem, vmem_limit_bytes=VMEM_LIMIT_BYTES)


def _rms(x, g):
    ms = jnp.mean(x * x, axis=-1, keepdims=True)
    return x * lax.rsqrt(ms + RMS_EPS) * g


def _inproj_kernel(x_ref, g_ref, w_ref, o_ref):
    h = _rms(x_ref[...], g_ref[...]).astype(BF16)
    o_ref[...] = jnp.dot(h, w_ref[...], preferred_element_type=F32)


def _inproj(x, g, w, *, tm=512, tn=2304):
    t, d = x.shape
    n = w.shape[1]
    return pl.pallas_call(
        _inproj_kernel,
        grid=(n // tn, t // tm),
        in_specs=[pl.BlockSpec((tm, d), lambda j, i: (i, 0)),
                  pl.BlockSpec((1, d), lambda j, i: (0, 0)),
                  pl.BlockSpec((d, tn), lambda j, i: (0, j))],
        out_specs=pl.BlockSpec((tm, tn), lambda j, i: (i, j)),
        out_shape=jax.ShapeDtypeStruct((t, n), F32),
        compiler_params=_cparams("arbitrary", "arbitrary"),
        name="inproj",
    )(x, g.reshape(1, d), w)


def _attn_kernel(q_ref, k_ref, v_ref, band_ref, o_ref, kbuf, vbuf0, vbuf1, acc_ref, m_ref, l_ref):
    s_len = q_ref.shape[1]
    n_flat = s_len // ATT_TQ
    lane = lax.broadcasted_iota(jnp.int32, (1, LANES), 1)
    head0 = lane < ATT_HEAD_DIM
    kcol = lax.broadcasted_iota(jnp.int32, (1, ATT_TK), 1)
    zpad = jnp.zeros((ATT_W, LANES), BF16)
    one_b = jnp.ones((), BF16)

    for p, (_, dil) in enumerate(DILATED_PATTERNS):
        sub_len = s_len // dil
        n_chunk = sub_len // ATT_TQ
        res_rows = sub_len + 2 * ATT_W

        def rows(start, size):
            if dil == 1:
                return pl.ds(start, size)
            return pl.ds(start, size, stride=dil)

        def split(f):
            return f // n_chunk, f % n_chunk

        def pad_body(r, carry):
            for off in (0, ATT_W + sub_len):
                dst = pl.ds(pl.multiple_of(r * res_rows + off, ATT_W), ATT_W)
                kbuf[dst, :] = zpad
                vbuf0[dst, :] = zpad
                vbuf1[dst, :] = zpad
            return carry

        lax.fori_loop(0, dil, pad_body, 0)

        def stage_body(f, carry):
            r, c = split(f)
            src = rows(r + c * ATT_TQ * dil, ATT_TQ)
            dst = pl.ds(pl.multiple_of(r * res_rows + ATT_W + c * ATT_TQ, ATT_W), ATT_TQ)
            kbuf[dst, :] = k_ref[0, src, :].astype(BF16)
            v = v_ref[0, src, :].astype(BF16)
            vbuf0[dst, :] = jnp.where(head0, v, one_b)
            vbuf1[dst, :] = jnp.where(head0, one_b, v)
            return carry

        lax.fori_loop(0, n_flat, stage_body, 0)

        def chunk(f):
            r, c = split(f)
            qrows = rows(r + c * ATT_TQ * dil, ATT_TQ)
            q = q_ref[0, qrows, :] * (ATT_HEAD_DIM ** -0.5)
            win = pl.ds(pl.multiple_of(r * res_rows + c * ATT_TQ, ATT_W), ATT_TK)
            kw = kbuf[win, :]
            kpos = c * ATT_TQ - ATT_W + kcol
            pos_ok = (kpos >= 0) & (kpos < sub_len)
            pv, mm = [], []
            for h, vbuf in enumerate((vbuf0, vbuf1)):
                sel = head0 if h == 0 else jnp.logical_not(head0)
                qh = jnp.where(sel, q, 0.0).astype(BF16)
                s = lax.dot_general(qh, kw, (((1,), (1,)), ((), ())), preferred_element_type=F32)
                s = jnp.where(pos_ok, s + band_ref[p, h], NEG_INF)
                m = jnp.max(s, axis=1, keepdims=True)
                mm.append(m)
                pv.append(jnp.dot(jnp.exp(s - m).astype(BF16), vbuf[win, :], preferred_element_type=F32))
            acc_n = jnp.where(head0, pv[0], pv[1])
            m_n = jnp.where(head0, mm[0], mm[1])
            l_n = pltpu.roll(jnp.where(head0, pv[1], pv[0]), ATT_HEAD_DIM, 1)
            if p == 0:
                acc_ref[qrows, :] = acc_n
                m_ref[qrows, :] = m_n
                l_ref[qrows, :] = l_n
            else:
                m_o = m_ref[qrows, :]
                m_t = jnp.maximum(m_o, m_n)
                a_o = jnp.exp(m_o - m_t)
                a_n = jnp.exp(m_n - m_t)
                acc_ref[qrows, :] = acc_ref[qrows, :] * a_o + acc_n * a_n
                l_ref[qrows, :] = l_ref[qrows, :] * a_o + l_n * a_n
                m_ref[qrows, :] = m_t

        def chunks_body(g, carry):
            for u in range(ATT_UNROLL):
                chunk(g * ATT_UNROLL + u)
            return carry

        lax.fori_loop(0, n_flat // ATT_UNROLL, chunks_body, 0)

    def out_body(c, carry):
        rws = pl.ds(pl.multiple_of(c * ATT_TQ, ATT_TQ), ATT_TQ)
        o_ref[0, rws, :] = (acc_ref[rws, :] / l_ref[rws, :]).astype(o_ref.dtype)
        return carry

    lax.fori_loop(0, s_len // ATT_TQ, out_body, 0)


def _attention(proj, band):
    b, s, _ = proj.shape
    n_pair = N_ATT_HEADS // 2
    staged_rows = s + 2 * ATT_W * max(dil for _, dil in DILATED_PATTERNS)
    blk = lambda off: pl.BlockSpec((1, s, LANES), lambda bi, hp: (bi, 0, off + hp))
    return pl.pallas_call(
        _attn_kernel,
        grid=(b, n_pair),
        in_specs=[blk(COL_AQ), blk(COL_AK), blk(COL_AV),
                  pl.BlockSpec((len(DILATED_PATTERNS), 2, ATT_TQ, ATT_TK),
                               lambda bi, hp: (0, hp, 0, 0))],
        out_specs=pl.BlockSpec((1, s, LANES), lambda bi, hp: (bi, 0, hp)),
        out_shape=jax.ShapeDtypeStruct((b, s, ATT_WIDTH), BF16),
        scratch_shapes=[pltpu.VMEM((staged_rows, LANES), BF16)] * 3 + [
                        pltpu.VMEM((s, LANES), F32),
                        pltpu.VMEM((s, LANES), F32),
                        pltpu.VMEM((s, LANES), F32)],
        compiler_params=_cparams("arbitrary", "arbitrary"),
        name="dilated_attention",
    )(proj, proj, proj, band)


def _t5_bucket(rel):
    nb = REL_BUCKETS // 2
    ret = (rel > 0).astype(np.int32) * nb
    n = np.abs(rel)
    max_exact = nb // 2
    nf = np.maximum(n, 1).astype(np.float32)
    large = max_exact + (np.log(nf / max_exact) / math.log(REL_MAX_DIST / max_exact)
                         * (nb - max_exact)).astype(np.int32)
    large = np.minimum(large, nb - 1)
    return ret + np.where(n < max_exact, n, large)


def _attention_band(rel_bias):
    span = ATT_TQ - 1 + ATT_W
    rel = np.arange(-span, ATT_TK - ATT_W)
    out = []
    for _, dil in DILATED_PATTERNS:
        diag = jnp.where((np.abs(rel) <= ATT_W)[:, None], rel_bias[_t5_bucket(dil * rel)].astype(F32), NEG_INF)
        diag = jnp.transpose(diag)
        out.append(jnp.stack([diag[:, ATT_TQ - 1 - i:ATT_TQ - 1 - i + ATT_TK] for i in range(ATT_TQ)], axis=1))
    return jnp.stack(out, axis=0)


CONV_ROWS = 512


def _conv_kernel(cb_ref, cc_ref, ch_ref, w_ref, o_ref, zbuf):
    s_len = cb_ref.shape[1]
    width = cb_ref.shape[2]
    zero = jnp.zeros((SUBLANES, width), F32)
    zbuf[pl.ds(0, SUBLANES), :] = zero
    zbuf[pl.ds(SUBLANES + s_len, SUBLANES), :] = zero
    n_blk = s_len // CONV_ROWS

    for c in range(n_blk):
        r0 = c * CONV_ROWS
        zbuf[pl.ds(SUBLANES + r0, CONV_ROWS), :] = (
            cc_ref[0, pl.ds(r0, CONV_ROWS), :] * ch_ref[0, pl.ds(r0, CONV_ROWS), :])
    w0 = w_ref[0:1, :]
    w1 = w_ref[1:2, :]
    w2 = w_ref[2:3, :]

    for c in range(n_blk):
        r0 = c * CONV_ROWS
        y = (w0 * zbuf[pl.ds(SUBLANES - 1 + r0, CONV_ROWS), :]
             + w1 * zbuf[pl.ds(SUBLANES + r0, CONV_ROWS), :]
             + w2 * zbuf[pl.ds(SUBLANES + 1 + r0, CONV_ROWS), :])
        o_ref[0, pl.ds(r0, CONV_ROWS), :] = (cb_ref[0, pl.ds(r0, CONV_ROWS), :] * y).astype(o_ref.dtype)


def _short_conv(proj, conv_w):
    b, s, _ = proj.shape
    cw = 256
    blk = lambda off: pl.BlockSpec((1, s, cw), lambda bi, j: (bi, 0, off + j))
    return pl.pallas_call(
        _conv_kernel,
        grid=(b, CONV_WIDTH // cw),
        in_specs=[blk(COL_CB), blk(COL_CC), blk(COL_CH),
                  pl.BlockSpec((3, cw), lambda bi, j: (0, j))],
        out_specs=pl.BlockSpec((1, s, cw), lambda bi, j: (bi, 0, j)),
        out_shape=jax.ShapeDtypeStruct((b, s, CONV_WIDTH), BF16),
        scratch_shapes=[pltpu.VMEM((s + 2 * SUBLANES, cw), F32)],
        compiler_params=_cparams("arbitrary", "arbitrary"),
        name="short_conv",
    )(proj, proj, proj, conv_w)


RET_UNROLL = 4


def _ret_kernel(lg_ref, q_ref, k_ref, v_ref, g_ref, cos_ref, sin_ref, o_ref, qs, ks, os_f, os_b):
    s_len = q_ref.shape[1]
    c_len = RET_CHUNK
    n_chunk = s_len // c_len
    h = pl.program_id(1)
    lg_f = lg_ref[0, h]
    lg_b = lg_ref[1, h]

    def rot_body(n, carry):
        rws = pl.ds(pl.multiple_of(n * c_len, c_len), c_len)
        cs = cos_ref[rws, :]
        sn = sin_ref[rws, :]
        q = q_ref[0, rws, :]
        k = k_ref[0, rws, :]
        qs[rws, :] = q * cs + pltpu.roll(q, RET_HEAD_DIM // 2, 1) * sn
        ks[rws, :] = (k * cs + pltpu.roll(k, RET_HEAD_DIM // 2, 1) * sn) * (RET_HEAD_DIM ** -0.5)
        return carry

    lax.fori_loop(0, n_chunk, rot_body, 0)

    ri = lax.broadcasted_iota(jnp.int32, (c_len, c_len), 0)
    ci = lax.broadcasted_iota(jnp.int32, (c_len, c_len), 1)
    diff = (ri - ci).astype(F32)
    col = lax.broadcasted_iota(jnp.int32, (c_len, 1), 0).astype(F32)
    one = jnp.ones((1, 1), F32)

    dec_f = jnp.where(diff >= 0, jnp.exp(lg_f * jnp.maximum(diff, 0.0)), 0.0)
    kw_f = jnp.exp(lg_f * (c_len - 1 - col))
    qw_f = jnp.exp(lg_f * (col + 1))
    gc_f = jnp.exp(lg_f * c_len * one)
    dec_b = jnp.where(diff < 0, jnp.exp(lg_b * jnp.maximum(-diff, 0.0)), 0.0)
    kw_b = jnp.exp(lg_b * col)
    qw_b = jnp.exp(lg_b * (c_len - col))
    gc_b = jnp.exp(lg_b * c_len * one)

    def direction(n, state, dec, kw, qw, gc):
        rws = pl.ds(pl.multiple_of(n * c_len, c_len), c_len)
        q = qs[rws, :]
        k = ks[rws, :]
        v = v_ref[0, rws, :].astype(BF16)
        qb = q.astype(BF16)
        sc = lax.dot_general(qb, k.astype(BF16), (((1,), (1,)), ((), ())),
                             preferred_element_type=F32)
        o_in = jnp.dot((sc * dec).astype(BF16), v, preferred_element_type=F32)
        o_x = jnp.dot(qb, state.astype(BF16), preferred_element_type=F32) * qw
        kv = jnp.dot(jnp.transpose(k * kw).astype(BF16), v, preferred_element_type=F32)
        return rws, o_in + o_x, gc * state + kv

    def sweep_body(g, states):
        s_f, s_b = states
        for u in range(RET_UNROLL):
            n = g * RET_UNROLL + u
            rws, o, s_f = direction(n, s_f, dec_f, kw_f, qw_f, gc_f)
            os_f[rws, :] = o
            rws, o, s_b = direction(n_chunk - 1 - n, s_b, dec_b, kw_b, qw_b, gc_b)
            os_b[rws, :] = o
        return s_f, s_b

    zero_state = jnp.zeros((RET_HEAD_DIM, RET_HEAD_DIM), F32)
    lax.fori_loop(0, n_chunk // RET_UNROLL, sweep_body, (zero_state, zero_state))

    def norm_body(g, carry):
        for u in range(RET_UNROLL):
            rws = pl.ds(pl.multiple_of((g * RET_UNROLL + u) * c_len, c_len), c_len)
            o = os_f[rws, :] + os_b[rws, :]
            mu = jnp.mean(o, axis=-1, keepdims=True)
            oc = o - mu
            var = jnp.mean(oc * oc, axis=-1, keepdims=True)
            on = oc * lax.rsqrt(var + GN_EPS)
            gt = g_ref[0, rws, :]
            o_ref[0, rws, :] = (gt * jax.nn.sigmoid(gt) * on).astype(o_ref.dtype)
        return carry

    lax.fori_loop(0, n_chunk // RET_UNROLL, norm_body, 0)


def _retention(proj, log_g, cos2, sin2):
    b, s, _ = proj.shape
    blk = lambda off: pl.BlockSpec((1, s, LANES), lambda bi, h: (bi, 0, off + h))
    tab = pl.BlockSpec((s, LANES), lambda bi, h: (0, 0))
    return pl.pallas_call(
        _ret_kernel,
        grid=(b, N_RET_HEADS),
        in_specs=[pl.BlockSpec(memory_space=pltpu.SMEM),
                  blk(COL_RQ), blk(COL_RK), blk(COL_RV), blk(COL_RG), tab, tab],
        out_specs=pl.BlockSpec((1, s, LANES), lambda bi, h: (bi, 0, h)),
        out_shape=jax.ShapeDtypeStruct((b, s, RET_WIDTH), BF16),
        scratch_shapes=[pltpu.VMEM((s, LANES), F32)] * 4,
        compiler_params=_cparams("arbitrary", "arbitrary"),
        name="retention",
    )(log_g, proj, proj, proj, proj, cos2, sin2)


def _rotary_tables(s):
    d = RET_HEAD_DIM
    inv = ROPE_BASE ** (-jnp.arange(0, d, 2, dtype=F32) / d)
    ang = jnp.arange(s, dtype=F32)[:, None] * inv[None, :]
    cos, sin = jnp.cos(ang), jnp.sin(ang)
    return jnp.concatenate([cos, cos], axis=-1), jnp.concatenate([-sin, sin], axis=-1)


def _outproj_kernel(att_ref, conv_ref, ret_ref, x_ref, w_ref, o_ref):
    acc = jnp.dot(att_ref[...], w_ref[pl.ds(0, ATT_WIDTH), :], preferred_element_type=F32)
    acc += jnp.dot(conv_ref[...], w_ref[pl.ds(ATT_WIDTH, CONV_WIDTH), :], preferred_element_type=F32)
    acc += jnp.dot(ret_ref[...], w_ref[pl.ds(ATT_WIDTH + CONV_WIDTH, RET_WIDTH), :],
                   preferred_element_type=F32)
    o_ref[...] = x_ref[...] + acc


def _outproj(att, conv, ret, x, w, *, tm=512):
    t, d = x.shape
    row = lambda width: pl.BlockSpec((tm, width), lambda i: (i, 0))
    return pl.pallas_call(
        _outproj_kernel,
        grid=(t // tm,),
        in_specs=[row(ATT_WIDTH), row(CONV_WIDTH), row(RET_WIDTH), row(d),
                  pl.BlockSpec((MIX_WIDTH, d), lambda i: (0, 0))],
        out_specs=row(d),
        out_shape=jax.ShapeDtypeStruct((t, d), F32),
        compiler_params=_cparams("arbitrary"),
        name="outproj",
    )(att, conv, ret, x, w)


def _top16(s, payload=None):
    n = s.shape[0]
    row = lax.broadcasted_iota(jnp.int32, s.shape, 0).astype(F32)
    vals, picks = [], []
    for _ in range(PEER_TOPK):
        m = jnp.max(s, axis=0, keepdims=True)
        idx = jnp.min(jnp.where(s == m, row, float(n)), axis=0, keepdims=True)
        hit = row == idx
        vals.append(m)
        if payload is None:
            picks.append(idx)
        else:
            picks.append(jnp.max(jnp.where(hit, payload, -1.0), axis=0, keepdims=True))
        s = jnp.where(hit, -jnp.inf, s)
    return vals, picks


PAIR_COUNTS = tuple(PEER_TOPK // (a + 1) for a in range(PEER_TOPK))
N_PAIR = sum(PAIR_COUNTS)
N_PAIR_PAD = -(-N_PAIR // SUBLANES) * SUBLANES


def _peer_topk_kernel(x_ref, g_ref, wq_ref, keys_ref, h_ref, eidx_ref, gate_ref, sv_s, si_s, cand_s, eid_s):
    h = _rms(x_ref[...], g_ref[...])
    h_ref[...] = h
    q = jnp.dot(h.astype(BF16), wq_ref[...], preferred_element_type=F32).astype(BF16)
    for grp in range(2 * PEER_HEADS):
        s = lax.dot_general(keys_ref[grp], q[:, grp * PEER_DHALF:(grp + 1) * PEER_DHALF],
                            (((1,), (1,)), ((), ())), preferred_element_type=F32)
        vals, idxs = _top16(s)
        for j in range(PEER_TOPK):
            sv_s[grp, pl.ds(j, 1), :] = vals[j]
            si_s[grp, pl.ds(j, 1), :] = idxs[j]
    tm = cand_s.shape[1]
    cand_s[pl.ds(N_PAIR_PAD - SUBLANES, SUBLANES), :] = jnp.full((SUBLANES, tm), -jnp.inf, F32)
    eid_s[pl.ds(N_PAIR_PAD - SUBLANES, SUBLANES), :] = jnp.full((SUBLANES, tm), -1.0, F32)
    for hd in range(PEER_HEADS):
        sv0, sv1 = sv_s[2 * hd], sv_s[2 * hd + 1]
        si0, si1 = si_s[2 * hd], si_s[2 * hd + 1]
        off = 0
        for a, nb in enumerate(PAIR_COUNTS):
            cand_s[pl.ds(off, nb), :] = sv0[a:a + 1, :] + sv1[0:nb, :]
            eid_s[pl.ds(off, nb), :] = si0[a:a + 1, :] * float(PEER_NKEYS) + si1[0:nb, :]
            off += nb
        vals, picks = _top16(cand_s[...], eid_s[...])
        cv = jnp.concatenate(vals, axis=0)
        e = jnp.exp(cv - vals[0])
        gate_ref[pl.ds(hd * PEER_TOPK, PEER_TOPK), :] = e / jnp.sum(e, axis=0, keepdims=True)
        eidx_ref[pl.ds(hd * PEER_TOPK, PEER_TOPK), :] = jnp.concatenate(picks, axis=0).astype(jnp.int32)


def _peer_topk(x, g, wq, keys, *, tm=256):
    t, d = x.shape
    return pl.pallas_call(
        _peer_topk_kernel,
        grid=(t // tm,),
        in_specs=[pl.BlockSpec((tm, d), lambda i: (i, 0)),
                  pl.BlockSpec((1, d), lambda i: (0, 0)),
                  pl.BlockSpec((d, 2 * PEER_HEADS * PEER_DHALF), lambda i: (0, 0)),
                  pl.BlockSpec((2 * PEER_HEADS, PEER_NKEYS, PEER_DHALF), lambda i: (0, 0, 0))],
        out_specs=[pl.BlockSpec((tm, d), lambda i: (i, 0)),
                   pl.BlockSpec((PEER_SEL, tm), lambda i: (0, i)),
                   pl.BlockSpec((PEER_SEL, tm), lambda i: (0, i))],
        out_shape=[jax.ShapeDtypeStruct((t, d), F32),
                   jax.ShapeDtypeStruct((PEER_SEL, t), jnp.int32),
                   jax.ShapeDtypeStruct((PEER_SEL, t), F32)],
        scratch_shapes=[pltpu.VMEM((2 * PEER_HEADS, PEER_TOPK, tm), F32),
                        pltpu.VMEM((2 * PEER_HEADS, PEER_TOPK, tm), F32),
                        pltpu.VMEM((N_PAIR_PAD, tm), F32),
                        pltpu.VMEM((N_PAIR_PAD, tm), F32)],
        compiler_params=_cparams("arbitrary"),
        name="peer_topk",
    )(x, g.reshape(1, d), wq, keys)


PEER_TB = 512
N_SLOT = 8
LOOKAHEAD = N_SLOT - 1
SLAB = D_MODEL // LANES
EXPERT_GROUP = 16
N_GROUP = PEER_SEL // EXPERT_GROUP
SEGMENT = 8
N_SEGMENT = PEER_SEL // SEGMENT
ISSUE_PER_SEGMENT = PEER_SEL // (2 * N_SEGMENT)


def _gelu_tanh(x):
    return 0.5 * x * (1.0 + jnp.tanh(math.sqrt(2.0 / math.pi) * (x + 0.044715 * (x * x * x))))


def _peer_expert_kernel(idx_hbm, tab_hbm, sel_ref, h_ref, gate_ref, o_ref,
                        idx_s, gbuf, pbuf, wsc, isem, gsem):
    i = pl.program_id(0)
    icp = pltpu.make_async_copy(idx_hbm.at[i], idx_s, isem)
    icp.start()
    icp.wait()

    def issue(t, slot, j0, n):
        row = idx_s.at[pl.ds(t * PEER_SEL, PEER_SEL)]
        for j in range(j0, j0 + n):
            pltpu.make_async_copy(tab_hbm.at[row[j]], gbuf.at[slot, j],
                                  gsem.at[slot]).start(priority=j % 2)

    def wait(slot):
        pltpu.make_async_copy(tab_hbm.at[pl.ds(0, PEER_SEL)], gbuf.at[slot], gsem.at[slot]).wait()

    lane_t = lax.broadcasted_iota(jnp.int32, (1, PEER_TB), 1)
    hi_mask = jnp.uint32(0xFFFF0000)

    def load_segment(slot, seg):
        return [gbuf[slot, seg * SEGMENT + k, half * SUBLANES:(half + 1) * SUBLANES, :]
                for k in range(SEGMENT) for half in range(2)]

    def key_side(t, slot, issue_args=None):
        x3 = h_ref[t]
        x0, x1 = x3[0:SUBLANES], x3[SUBLANES:SLAB]
        raw = load_segment(slot, 0)
        parts = []
        for seg in range(N_SEGMENT):
            nxt = load_segment(slot, seg + 1) if seg + 1 < N_SEGMENT else None
            if issue_args is not None:
                issue(*issue_args, seg * ISSUE_PER_SEGMENT, ISSUE_PER_SEGMENT)
            for k in range(SEGMENT):
                parts.append(pltpu.bitcast(raw[2 * k] << 16, F32) * x0
                             + pltpu.bitcast(raw[2 * k + 1] << 16, F32) * x1)
            if len(parts) == EXPERT_GROUP:
                grp = seg * SEGMENT // EXPERT_GROUP
                pbuf[pl.ds(grp * LANES, LANES), :] = jnp.concatenate(parts, axis=0).astype(BF16)
                parts = []
            raw = nxt

    def weight_inputs(t):
        q = jnp.dot(sel_ref[...], pbuf[...], preferred_element_type=F32)
        gate = jnp.sum(jnp.where(lane_t == t, gate_ref[...], 0.0), axis=1, keepdims=True)
        return q, gate

    def store_weights(q, gate):
        act = jnp.sum(q, axis=1, keepdims=True)
        wsc[...] = jnp.broadcast_to(gate * _gelu_tanh(act), (PEER_SEL, LANES))

    def value_side(t, slot, issue_args=None):
        acc = [jnp.zeros((SUBLANES, LANES), F32) for _ in range(4)]

        def load_weights(seg):
            return [wsc[pl.ds(seg * SEGMENT + k, 1), :] for k in range(SEGMENT)]

        raw, wts = load_segment(slot, 0), load_weights(0)
        for seg in range(N_SEGMENT):
            last = seg + 1 == N_SEGMENT
            nxt, wnxt = (None, None) if last else (load_segment(slot, seg + 1), load_weights(seg + 1))
            if issue_args is not None:
                issue(*issue_args, (N_SEGMENT + seg) * ISSUE_PER_SEGMENT, ISSUE_PER_SEGMENT)
            for k in range(SEGMENT):
                v0 = pltpu.bitcast(raw[2 * k] & hi_mask, F32)
                v1 = pltpu.bitcast(raw[2 * k + 1] & hi_mask, F32)
                acc[2 * (k % 2)] = acc[2 * (k % 2)] + wts[k] * v0
                acc[2 * (k % 2) + 1] = acc[2 * (k % 2) + 1] + wts[k] * v1
            raw, wts = nxt, wnxt
        o_ref[t] = jnp.concatenate([acc[0] + acc[2], acc[1] + acc[3]], axis=0)

    def prologue_body(n, carry):
        k = n // (PEER_SEL // SEGMENT)
        j = (n % (PEER_SEL // SEGMENT)) * SEGMENT
        for b in range(SEGMENT):
            pltpu.make_async_copy(tab_hbm.at[idx_s[k * PEER_SEL + j + b]], gbuf.at[k, j + b],
                                  gsem.at[k]).start(priority=b % 2)
        return carry

    lax.fori_loop(0, LOOKAHEAD * (PEER_SEL // SEGMENT), prologue_body, 0)
    wait(0)
    key_side(0, 0)
    store_weights(*weight_inputs(0))
    wait(1)
    key_side(1, 1)

    def token_body(t, carry):
        gather = (jnp.minimum(t + LOOKAHEAD, PEER_TB - 1), (t + LOOKAHEAD) % N_SLOT)
        wait((t + 2) % N_SLOT)
        q, gate = weight_inputs(t + 1)
        key_side(t + 2, (t + 2) % N_SLOT, gather)
        value_side(t, t % N_SLOT, gather)
        store_weights(q, gate)
        return carry

    n_here = jnp.minimum(PEER_TB, pl.num_programs(0) * PEER_TB - i * PEER_TB)
    lax.fori_loop(0, n_here - 2, token_body, 0)
    value_side(PEER_TB - 2, (PEER_TB - 2) % N_SLOT)
    store_weights(*weight_inputs(PEER_TB - 1))
    value_side(PEER_TB - 1, (PEER_TB - 1) % N_SLOT)
    for k in range(LOOKAHEAD - 2):
        wait(k)


def _peer_experts(idx, tab, h, gate_t, x, n_tokens):
    t, d = x.shape
    n_blk = n_tokens // PEER_TB
    slabs = pl.BlockSpec((PEER_TB, SLAB, LANES), lambda i: (i, 0, 0))
    sel = (jnp.arange(PEER_SEL)[:, None] == jnp.arange(PEER_SEL * SUBLANES)[None, :] // SUBLANES).astype(BF16)
    out = pl.pallas_call(
        _peer_expert_kernel,
        grid=(n_blk,),
        in_specs=[pl.BlockSpec(memory_space=pl.ANY),
                  pl.BlockSpec(memory_space=pl.ANY),
                  pl.BlockSpec((PEER_SEL, PEER_SEL * SUBLANES), lambda i: (0, 0)),
                  slabs,
                  pl.BlockSpec((PEER_SEL, PEER_TB), lambda i: (0, i))],
        out_specs=slabs,
        out_shape=jax.ShapeDtypeStruct((n_tokens, SLAB, LANES), F32),
        scratch_shapes=[pltpu.SMEM((PEER_TB * PEER_SEL,), jnp.int32),
                        pltpu.VMEM((N_SLOT, PEER_SEL, SLAB, LANES), jnp.uint32),
                        pltpu.VMEM((PEER_SEL * SUBLANES, LANES), BF16),
                        pltpu.VMEM((PEER_SEL, LANES), F32),
                        pltpu.SemaphoreType.DMA,
                        pltpu.SemaphoreType.DMA((N_SLOT,))],
        compiler_params=_cparams("arbitrary"),
        name="peer_experts",
    )(idx, tab, sel, h.reshape(t, SLAB, LANES), gate_t)
    return out.reshape(n_tokens, d)


PACK_ROWS = 8


def _pack_expert_table(u, v, layer):
    _, e, d = u.shape

    def body(u_hbm, v_hbm, o_hbm):
        def block(u_v, v_v, o_v):
            @pl.loop(0, PACK_ROWS)
            def _(r):
                @pl.loop(0, d // SC_LANES, step=SUBLANES)
                def _(c0):
                    for k in range(SUBLANES):
                        sl = pl.ds((c0 + k) * SC_LANES, SC_LANES)
                        pair = plsc.pack(u_v[0, r, sl], v_v[0, r, sl], format=plsc.PackFormat.INTERLEAVED)
                        o_v[r, sl] = plsc.bitcast(pair, jnp.uint32)

        rows = pl.BlockSpec((PACK_ROWS, d), lambda i: (i, 0))
        layer_rows = pl.BlockSpec((1, PACK_ROWS, d), lambda i: (layer, i, 0))
        pltpu.emit_pipeline(
            block,
            grid=(e // PACK_ROWS,),
            in_specs=[layer_rows, layer_rows],
            out_specs=[rows],
            core_axis_name=("core", "subcore"),
            dimension_semantics=(pltpu.PARALLEL,),
        )(u_hbm, v_hbm, o_hbm)

    return pl.kernel(
        body,
        out_type=jax.ShapeDtypeStruct((e, d), jnp.uint32),
        mesh=plsc.VectorSubcoreMesh(core_axis_name="core", subcore_axis_name="subcore"),
        scratch_types=[],
        compiler_params=pltpu.CompilerParams(needs_layout_passes=False),
        name="pack_expert_table_sc",
    )(u, v)


SC_LANES = 16
SC_CHUNKS = D_MODEL // SC_LANES
SC_EXPERTS = 16
SC_BLOCK = 32
SC_PARTIALS = 4
PEER_SC_TOKENS = 2560
PEER_SC_TOKENS_LAST = 2048


def _gelu_tanh_exp(x):
    z = math.sqrt(2.0 / math.pi) * (x + 0.044715 * (x * x * x))
    return 0.5 * x * (2.0 - 2.0 / (jnp.exp(2.0 * z) + 1.0))


def _peer_experts_sc(idx, tab, h, gate, x, first, n):
    d = x.shape[1]
    info = plsc.get_sparse_core_info()
    n_core, n_worker = info.num_cores, info.num_cores * info.num_subcores
    per_worker = n // n_worker
    n_chunk = PEER_SEL // SC_EXPERTS
    hi_mask = jnp.uint32(0xFFFF0000)

    def body(idx_hbm, tab_hbm, h_hbm, gate_hbm, x_hbm, o_hbm, idx_v, gate_v, h_v, out_v, rows_v, w_v, sem):
        worker = lax.axis_index("subcore") * n_core + lax.axis_index("core")

        def gather(cb):
            return pltpu.make_async_copy(tab_hbm.at[idx_v.at[pl.ds(cb * SC_EXPERTS, SC_EXPERTS)]],
                                         rows_v.at[cb % 2], sem.at[cb % 2])

        def mix_chunk(cb):
            rows = rows_v.at[cb % 2]
            for e in range(SC_EXPERTS):
                w_v[e, :] = jnp.zeros((SC_LANES,), F32)

            def key_block(blk, carry):
                base = blk * (SC_BLOCK * SC_LANES)
                hs = [h_v[pl.ds(base + j * SC_LANES, SC_LANES)] for j in range(SC_BLOCK)]
                for e in range(SC_EXPERTS):
                    part = [w_v[e, :]] + [None] * (SC_PARTIALS - 1)
                    for j in range(SC_BLOCK):
                        word = rows[e, pl.ds(base + j * SC_LANES, SC_LANES)]
                        term = plsc.bitcast(word << 16, F32) * hs[j]
                        k = j % SC_PARTIALS
                        part[k] = term if part[k] is None else part[k] + term
                    w_v[e, :] = functools.reduce(lambda a, b: a + b, part)
                return carry

            lax.fori_loop(0, SC_CHUNKS // SC_BLOCK, key_block, 0)
            for e in range(SC_EXPERTS):
                act = jnp.full((SC_LANES,), jnp.sum(w_v[e, :]), F32)
                g = plsc.load_gather(gate_v, [jnp.full((SC_LANES,), cb * SC_EXPERTS + e, jnp.int32)])
                w_v[e, :] = g * _gelu_tanh_exp(act)

            def value_block(blk, carry):
                base = blk * (SC_BLOCK * SC_LANES)
                outs = [out_v[pl.ds(base + j * SC_LANES, SC_LANES)] for j in range(SC_BLOCK)]
                for e in range(SC_EXPERTS):
                    w = w_v[e, :]
                    for j in range(SC_BLOCK):
                        word = rows[e, pl.ds(base + j * SC_LANES, SC_LANES)]
                        outs[j] = outs[j] + w * plsc.bitcast(word & hi_mask, F32)
                for j in range(SC_BLOCK):
                    out_v[pl.ds(base + j * SC_LANES, SC_LANES)] = outs[j]
                return carry

            lax.fori_loop(0, SC_CHUNKS // SC_BLOCK, value_block, 0)

        @pl.loop(0, per_worker)
        def _(i):
            row = worker * per_worker + i
            t = first + row
            pltpu.sync_copy(idx_hbm.at[t], idx_v)
            gather(0).start()
            pltpu.sync_copy(gate_hbm.at[t], gate_v)
            pltpu.sync_copy(h_hbm.at[t], h_v)
            @pl.loop(0, SC_CHUNKS, step=SUBLANES)
            def _(c0):
                for k in range(SUBLANES):
                    out_v[pl.ds((c0 + k) * SC_LANES, SC_LANES)] = jnp.zeros((SC_LANES,), F32)

            @pl.loop(0, n_chunk)
            def _(cb):
                @pl.when(cb + 1 < n_chunk)
                def _():
                    gather(cb + 1).start()

                gather(cb).wait()
                mix_chunk(cb)

            pltpu.sync_copy(out_v, o_hbm.at[row])

    return pl.kernel(
        body,
        out_type=jax.ShapeDtypeStruct((n, d), F32),
        mesh=plsc.VectorSubcoreMesh(core_axis_name="core", subcore_axis_name="subcore"),
        scratch_types=[pltpu.VMEM((PEER_SEL,), jnp.int32),
                       pltpu.VMEM((PEER_SEL,), F32),
                       pltpu.VMEM((d,), F32),
                       pltpu.VMEM((d,), F32),
                       pltpu.VMEM((2, SC_EXPERTS, d), jnp.uint32),
                       pltpu.VMEM((SC_EXPERTS, SC_LANES), F32),
                       pltpu.SemaphoreType.DMA((2,))],
        compiler_params=pltpu.CompilerParams(needs_layout_passes=False),
        name="peer_experts_sc",
    )(idx, tab, h, gate, x)


def _final_norm_kernel(x_ref, g_ref, o_ref):
    o_ref[...] = _rms(x_ref[...], g_ref[...])


def _final_norm(x, g, *, tm=512):
    t, d = x.shape
    row = pl.BlockSpec((tm, d), lambda i: (i, 0))
    return pl.pallas_call(
        _final_norm_kernel,
        grid=(t // tm,),
        in_specs=[row, pl.BlockSpec((1, d), lambda i: (0, 0))],
        out_specs=row,
        out_shape=jax.ShapeDtypeStruct((t, d), F32),
        compiler_params=_cparams("arbitrary"),
        name="final_norm",
    )(x, g.reshape(1, d))


def _mixer_and_retrieval(x, shape, p, band, rot):
    b, s, _ = shape
    t = b * s
    proj = _inproj(x, p["norm_mix"], p["w_in"]).reshape(b, s, IN_COLS)
    att = _attention(proj, band)
    conv = _short_conv(proj, p["conv_w"])
    ret = _retention(proj, p["log_g"], *rot)
    x = _outproj(att.reshape(t, ATT_WIDTH), conv.reshape(t, CONV_WIDTH),
                 ret.reshape(t, RET_WIDTH), x, p["w_out"])
    return (x,) + tuple(_peer_topk(x, p["norm_ffn"], p["peer_wq"], p["peer_keys"]))


def _expert_mix(x, h, eidx_t, gate_t, p, n_sc):
    t = x.shape[0]
    t_tc = t - n_sc
    idx = jnp.transpose(eidx_t)
    x_tc = _peer_experts(idx.reshape(t // PEER_TB, PEER_TB * PEER_SEL),
                         p["peer_tab"].reshape(PEER_EXPERTS, SLAB, LANES), h, gate_t, x, t_tc)
    x_sc = _peer_experts_sc(idx, p["peer_tab"], h, jnp.transpose(gate_t), x, t_tc, n_sc)
    return x + jnp.concatenate([x_tc, x_sc], axis=0)


def _trunks(xs, params):
    shapes = [x.shape for x in xs]
    rots = [_rotary_tables(shape[1]) for shape in shapes]
    xs = [x.reshape(-1, D_MODEL) for x in xs]
    for l in range(DEPTH):
        p = params[l]
        staged = [_mixer_and_retrieval(x, shape, p, params["band"], rot)
                  for x, shape, rot in zip(xs, shapes, rots)]
        p = dict(p, peer_tab=_pack_expert_table(*p["peer_uv"]))
        shares = [PEER_SC_TOKENS_LAST if (l, i) == (DEPTH - 1, len(xs) - 1) else PEER_SC_TOKENS
                  for i in range(len(xs))]
        xs = [_expert_mix(*st, p, n_sc) for st, n_sc in zip(staged, shares)]
    return tuple(_final_norm(x, params["final_norm"]).reshape(shape) for x, shape in zip(xs, shapes))


def kernel(x_prompt, x_sample, rel_bias, final_norm, norm_mix, norm_ffn, w_in, conv_w, ret_decay,
           w_out, peer_wq, peer_keys, peer_u, peer_v):
    params = {"band": _attention_band(rel_bias), "final_norm": final_norm}
    for l in range(DEPTH):
        params[l] = {
            "norm_mix": norm_mix[l],
            "norm_ffn": norm_ffn[l],
            "w_in": w_in[l].astype(BF16),
            "conv_w": conv_w[l],
            "log_g": jax.nn.log_sigmoid(ret_decay[l].astype(F32)),
            "w_out": w_out[l].astype(BF16),
            "peer_wq": peer_wq[l].astype(BF16),
            "peer_keys": peer_keys[l].reshape(2 * PEER_HEADS, PEER_NKEYS, PEER_DHALF).astype(BF16),
            "peer_uv": (peer_u, peer_v, l),
        }
    return _trunks([x_prompt, x_sample], params)
```

```python
import functools
import math

import jax
import jax.numpy as jnp
import numpy as np
from jax import lax
from jax.experimental import pallas as pl
from jax.experimental.pallas import tpu as pltpu
from jax.experimental.pallas import tpu_sc as plsc

F32 = jnp.float32
BF16 = jnp.bfloat16

D_MODEL = 2048
DEPTH = 2
N_ATT_HEADS = 12
ATT_HEAD_DIM = 64
ATT_WIDTH = N_ATT_HEADS * ATT_HEAD_DIM
DILATED_PATTERNS = ((128, 1), (512, 4), (2048, 16))
REL_BUCKETS = 32
REL_MAX_DIST = 1024
CONV_WIDTH = 512
N_RET_HEADS = 6
RET_HEAD_DIM = 128
RET_WIDTH = N_RET_HEADS * RET_HEAD_DIM
RET_CHUNK = 128
ROPE_BASE = 10000.0
MIX_WIDTH = ATT_WIDTH + CONV_WIDTH + RET_WIDTH
IN_COLS = 3 * ATT_WIDTH + 3 * CONV_WIDTH + 4 * RET_WIDTH
PEER_HEADS = 8
PEER_NKEYS = 128
PEER_EXPERTS = PEER_NKEYS * PEER_NKEYS
PEER_DHALF = 128
PEER_TOPK = 16
PEER_SEL = PEER_HEADS * PEER_TOPK
RMS_EPS = 1e-6
GN_EPS = 1e-6
NEG_INF = -1e30

LANES = 128
SUBLANES = 8
VMEM_LIMIT_BYTES = 56 * 1024 * 1024

ATT_W = 64
ATT_TQ = 128
ATT_TK = ATT_TQ + 2 * ATT_W
ATT_UNROLL = 8

COL_AQ, COL_AK, COL_AV = 0, ATT_WIDTH // LANES, 2 * ATT_WIDTH // LANES
COL_CB = 3 * ATT_WIDTH // 256
COL_CC = COL_CB + CONV_WIDTH // 256
COL_CH = COL_CC + CONV_WIDTH // 256
COL_RQ = (3 * ATT_WIDTH + 3 * CONV_WIDTH) // LANES
COL_RK = COL_RQ + N_RET_HEADS
COL_RV = COL_RK + N_RET_HEADS
COL_RG = COL_RV + N_RET_HEADS


def _cparams(*sem):
    return pltpu.CompilerParams(dimension_semantics=sem, vmem_limit_bytes=VMEM_LIMIT_BYTES)


def _rms(x, g):
    ms = jnp.mean(x * x, axis=-1, keepdims=True)
    return x * lax.rsqrt(ms + RMS_EPS) * g


def _inproj_kernel(x_ref, g_ref, w_ref, o_ref):
    h = _rms(x_ref[...], g_ref[...]).astype(BF16)
    o_ref[...] = jnp.dot(h, w_ref[...], preferred_element_type=F32)


def _inproj(x, g, w, *, tm=512, tn=2304):
    t, d = x.shape
    n = w.shape[1]
    return pl.pallas_call(
        _inproj_kernel,
        grid=(n // tn, t // tm),
        in_specs=[pl.BlockSpec((tm, d), lambda j, i: (i, 0)),
                  pl.BlockSpec((1, d), lambda j, i: (0, 0)),
                  pl.BlockSpec((d, tn), lambda j, i: (0, j))],
        out_specs=pl.BlockSpec((tm, tn), lambda j, i: (i, j)),
        out_shape=jax.ShapeDtypeStruct((t, n), F32),
        compiler_params=_cparams("arbitrary", "arbitrary"),
        name="inproj",
    )(x, g.reshape(1, d), w)


def _attn_kernel(q_ref, k_ref, v_ref, band_ref, o_ref, kbuf, vbuf0, vbuf1, acc_ref, m_ref, l_ref):
    s_len = q_ref.shape[1]
    n_flat = s_len // ATT_TQ
    lane = lax.broadcasted_iota(jnp.int32, (1, LANES), 1)
    head0 = lane < ATT_HEAD_DIM
    kcol = lax.broadcasted_iota(jnp.int32, (1, ATT_TK), 1)
    zpad = jnp.zeros((ATT_W, LANES), BF16)
    one_b = jnp.ones((), BF16)

    for p, (_, dil) in enumerate(DILATED_PATTERNS):
        sub_len = s_len // dil
        n_chunk = sub_len // ATT_TQ
        res_rows = sub_len + 2 * ATT_W

        def rows(start, size):
            if dil == 1:
                return pl.ds(start, size)
            return pl.ds(start, size, stride=dil)

        def split(f):
            return f // n_chunk, f % n_chunk

        def pad_body(r, carry):
            for off in (0, ATT_W + sub_len):
                dst = pl.ds(pl.multiple_of(r * res_rows + off, ATT_W), ATT_W)
                kbuf[dst, :] = zpad
                vbuf0[dst, :] = zpad
                vbuf1[dst, :] = zpad
            return carry

        lax.fori_loop(0, dil, pad_body, 0)

        def stage_body(f, carry):
            r, c = split(f)
            src = rows(r + c * ATT_TQ * dil, ATT_TQ)
            dst = pl.ds(pl.multiple_of(r * res_rows + ATT_W + c * ATT_TQ, ATT_W), ATT_TQ)
            kbuf[dst, :] = k_ref[0, src, :].astype(BF16)
            v = v_ref[0, src, :].astype(BF16)
            vbuf0[dst, :] = jnp.where(head0, v, one_b)
            vbuf1[dst, :] = jnp.where(head0, one_b, v)
            return carry

        lax.fori_loop(0, n_flat, stage_body, 0)

        def chunk(f):
            r, c = split(f)
            qrows = rows(r + c * ATT_TQ * dil, ATT_TQ)
            q = q_ref[0, qrows, :] * (ATT_HEAD_DIM ** -0.5)
            win = pl.ds(pl.multiple_of(r * res_rows + c * ATT_TQ, ATT_W), ATT_TK)
            kw = kbuf[win, :]
            kpos = c * ATT_TQ - ATT_W + kcol
            pos_ok = (kpos >= 0) & (kpos < sub_len)
            pv, mm = [], []
            for h, vbuf in enumerate((vbuf0, vbuf1)):
                sel = head0 if h == 0 else jnp.logical_not(head0)
                qh = jnp.where(sel, q, 0.0).astype(BF16)
                s = lax.dot_general(qh, kw, (((1,), (1,)), ((), ())), preferred_element_type=F32)
                s = jnp.where(pos_ok, s + band_ref[p, h], NEG_INF)
                m = jnp.max(s, axis=1, keepdims=True)
                mm.append(m)
                pv.append(jnp.dot(jnp.exp(s - m).astype(BF16), vbuf[win, :], preferred_element_type=F32))
            acc_n = jnp.where(head0, pv[0], pv[1])
            m_n = jnp.where(head0, mm[0], mm[1])
            l_n = pltpu.roll(jnp.where(head0, pv[1], pv[0]), ATT_HEAD_DIM, 1)
            if p == 0:
                acc_ref[qrows, :] = acc_n
                m_ref[qrows, :] = m_n
                l_ref[qrows, :] = l_n
            else:
                m_o = m_ref[qrows, :]
                m_t = jnp.maximum(m_o, m_n)
                a_o = jnp.exp(m_o - m_t)
                a_n = jnp.exp(m_n - m_t)
                acc_ref[qrows, :] = acc_ref[qrows, :] * a_o + acc_n * a_n
                l_ref[qrows, :] = l_ref[qrows, :] * a_o + l_n * a_n
                m_ref[qrows, :] = m_t

        def chunks_body(g, carry):
            for u in range(ATT_UNROLL):
                chunk(g * ATT_UNROLL + u)
            return carry

        lax.fori_loop(0, n_flat // ATT_UNROLL, chunks_body, 0)

    def out_body(c, carry):
        rws = pl.ds(pl.multiple_of(c * ATT_TQ, ATT_TQ), ATT_TQ)
        o_ref[0, rws, :] = (acc_ref[rws, :] / l_ref[rws, :]).astype(o_ref.dtype)
        return carry

    lax.fori_loop(0, s_len // ATT_TQ, out_body, 0)


def _attention(proj, band):
    b, s, _ = proj.shape
    n_pair = N_ATT_HEADS // 2
    staged_rows = s + 2 * ATT_W * max(dil for _, dil in DILATED_PATTERNS)
    blk = lambda off: pl.BlockSpec((1, s, LANES), lambda bi, hp: (bi, 0, off + hp))
    return pl.pallas_call(
        _attn_kernel,
        grid=(b, n_pair),
        in_specs=[blk(COL_AQ), blk(COL_AK), blk(COL_AV),
                  pl.BlockSpec((len(DILATED_PATTERNS), 2, ATT_TQ, ATT_TK),
                               lambda bi, hp: (0, hp, 0, 0))],
        out_specs=pl.BlockSpec((1, s, LANES), lambda bi, hp: (bi, 0, hp)),
        out_shape=jax.ShapeDtypeStruct((b, s, ATT_WIDTH), BF16),
        scratch_shapes=[pltpu.VMEM((staged_rows, LANES), BF16)] * 3 + [
                        pltpu.VMEM((s, LANES), F32),
                        pltpu.VMEM((s, LANES), F32),
                        pltpu.VMEM((s, LANES), F32)],
        compiler_params=_cparams("arbitrary", "arbitrary"),
        name="dilated_attention",
    )(proj, proj, proj, band)


def _t5_bucket(rel):
    nb = REL_BUCKETS // 2
    ret = (rel > 0).astype(np.int32) * nb
    n = np.abs(rel)
    max_exact = nb // 2
    nf = np.maximum(n, 1).astype(np.float32)
    large = max_exact + (np.log(nf / max_exact) / math.log(REL_MAX_DIST / max_exact)
                         * (nb - max_exact)).astype(np.int32)
    large = np.minimum(large, nb - 1)
    return ret + np.where(n < max_exact, n, large)


def _attention_band(rel_bias):
    span = ATT_TQ - 1 + ATT_W
    rel = np.arange(-span, ATT_TK - ATT_W)
    out = []
    for _, dil in DILATED_PATTERNS:
        diag = jnp.where((np.abs(rel) <= ATT_W)[:, None], rel_bias[_t5_bucket(dil * rel)].astype(F32), NEG_INF)
        diag = jnp.transpose(diag)
        out.append(jnp.stack([diag[:, ATT_TQ - 1 - i:ATT_TQ - 1 - i + ATT_TK] for i in range(ATT_TQ)], axis=1))
    return jnp.stack(out, axis=0)


CONV_ROWS = 512


def _conv_kernel(cb_ref, cc_ref, ch_ref, w_ref, o_ref, zbuf):
    s_len = cb_ref.shape[1]
    width = cb_ref.shape[2]
    zero = jnp.zeros((SUBLANES, width), F32)
    zbuf[pl.ds(0, SUBLANES), :] = zero
    zbuf[pl.ds(SUBLANES + s_len, SUBLANES), :] = zero
    n_blk = s_len // CONV_ROWS

    for c in range(n_blk):
        r0 = c * CONV_ROWS
        zbuf[pl.ds(SUBLANES + r0, CONV_ROWS), :] = (
            cc_ref[0, pl.ds(r0, CONV_ROWS), :] * ch_ref[0, pl.ds(r0, CONV_ROWS), :])
    w0 = w_ref[0:1, :]
    w1 = w_ref[1:2, :]
    w2 = w_ref[2:3, :]

    for c in range(n_blk):
        r0 = c * CONV_ROWS
        y = (w0 * zbuf[pl.ds(SUBLANES - 1 + r0, CONV_ROWS), :]
             + w1 * zbuf[pl.ds(SUBLANES + r0, CONV_ROWS), :]
             + w2 * zbuf[pl.ds(SUBLANES + 1 + r0, CONV_ROWS), :])
        o_ref[0, pl.ds(r0, CONV_ROWS), :] = (cb_ref[0, pl.ds(r0, CONV_ROWS), :] * y).astype(o_ref.dtype)


def _short_conv(proj, conv_w):
    b, s, _ = proj.shape
    cw = 256
    blk = lambda off: pl.BlockSpec((1, s, cw), lambda bi, j: (bi, 0, off + j))
    return pl.pallas_call(
        _conv_kernel,
        grid=(b, CONV_WIDTH // cw),
        in_specs=[blk(COL_CB), blk(COL_CC), blk(COL_CH),
                  pl.BlockSpec((3, cw), lambda bi, j: (0, j))],
        out_specs=pl.BlockSpec((1, s, cw), lambda bi, j: (bi, 0, j)),
        out_shape=jax.ShapeDtypeStruct((b, s, CONV_WIDTH), BF16),
        scratch_shapes=[pltpu.VMEM((s + 2 * SUBLANES, cw), F32)],
        compiler_params=_cparams("arbitrary", "arbitrary"),
        name="short_conv",
    )(proj, proj, proj, conv_w)


RET_UNROLL = 4


def _ret_kernel(lg_ref, q_ref, k_ref, v_ref, g_ref, cos_ref, sin_ref, o_ref, qs, ks, os_f, os_b):
    s_len = q_ref.shape[1]
    c_len = RET_CHUNK
    n_chunk = s_len // c_len
    h = pl.program_id(1)
    lg_f = lg_ref[0, h]
    lg_b = lg_ref[1, h]

    def rot_body(n, carry):
        rws = pl.ds(pl.multiple_of(n * c_len, c_len), c_len)
        cs = cos_ref[rws, :]
        sn = sin_ref[rws, :]
        q = q_ref[0, rws, :]
        k = k_ref[0, rws, :]
        qs[rws, :] = q * cs + pltpu.roll(q, RET_HEAD_DIM // 2, 1) * sn
        ks[rws, :] = (k * cs + pltpu.roll(k, RET_HEAD_DIM // 2, 1) * sn) * (RET_HEAD_DIM ** -0.5)
        return carry

    lax.fori_loop(0, n_chunk, rot_body, 0)

    ri = lax.broadcasted_iota(jnp.int32, (c_len, c_len), 0)
    ci = lax.broadcasted_iota(jnp.int32, (c_len, c_len), 1)
    diff = (ri - ci).astype(F32)
    col = lax.broadcasted_iota(jnp.int32, (c_len, 1), 0).astype(F32)
    one = jnp.ones((1, 1), F32)

    dec_f = jnp.where(diff >= 0, jnp.exp(lg_f * jnp.maximum(diff, 0.0)), 0.0)
    kw_f = jnp.exp(lg_f * (c_len - 1 - col))
    qw_f = jnp.exp(lg_f * (col + 1))
    gc_f = jnp.exp(lg_f * c_len * one)
    dec_b = jnp.where(diff < 0, jnp.exp(lg_b * jnp.maximum(-diff, 0.0)), 0.0)
    kw_b = jnp.exp(lg_b * col)
    qw_b = jnp.exp(lg_b * (c_len - col))
    gc_b = jnp.exp(lg_b * c_len * one)

    def direction(n, state, dec, kw, qw, gc):
        rws = pl.ds(pl.multiple_of(n * c_len, c_len), c_len)
        q = qs[rws, :]
        k = ks[rws, :]
        v = v_ref[0, rws, :].astype(BF16)
        qb = q.astype(BF16)
        sc = lax.dot_general(qb, k.astype(BF16), (((1,), (1,)), ((), ())),
                             preferred_element_type=F32)
        o_in = jnp.dot((sc * dec).astype(BF16), v, preferred_element_type=F32)
        o_x = jnp.dot(qb, state.astype(BF16), preferred_element_type=F32) * qw
        kv = jnp.dot(jnp.transpose(k * kw).astype(BF16), v, preferred_element_type=F32)
        return rws, o_in + o_x, gc * state + kv

    def sweep_body(g, states):
        s_f, s_b = states
        for u in range(RET_UNROLL):
            n = g * RET_UNROLL + u
            rws, o, s_f = direction(n, s_f, dec_f, kw_f, qw_f, gc_f)
            os_f[rws, :] = o
            rws, o, s_b = direction(n_chunk - 1 - n, s_b, dec_b, kw_b, qw_b, gc_b)
            os_b[rws, :] = o
        return s_f, s_b

    zero_state = jnp.zeros((RET_HEAD_DIM, RET_HEAD_DIM), F32)
    lax.fori_loop(0, n_chunk // RET_UNROLL, sweep_body, (zero_state, zero_state))

    def norm_body(g, carry):
        for u in range(RET_UNROLL):
            rws = pl.ds(pl.multiple_of((g * RET_UNROLL + u) * c_len, c_len), c_len)
            o = os_f[rws, :] + os_b[rws, :]
            mu = jnp.mean(o, axis=-1, keepdims=True)
            oc = o - mu
            var = jnp.mean(oc * oc, axis=-1, keepdims=True)
            on = oc * lax.rsqrt(var + GN_EPS)
            gt = g_ref[0, rws, :]
            o_ref[0, rws, :] = (gt * jax.nn.sigmoid(gt) * on).astype(o_ref.dtype)
        return carry

    lax.fori_loop(0, n_chunk // RET_UNROLL, norm_body, 0)


def _retention(proj, log_g, cos2, sin2):
    b, s, _ = proj.shape
    blk = lambda off: pl.BlockSpec((1, s, LANES), lambda bi, h: (bi, 0, off + h))
    tab = pl.BlockSpec((s, LANES), lambda bi, h: (0, 0))
    return pl.pallas_call(
        _ret_kernel,
        grid=(b, N_RET_HEADS),
        in_specs=[pl.BlockSpec(memory_space=pltpu.SMEM),
                  blk(COL_RQ), blk(COL_RK), blk(COL_RV), blk(COL_RG), tab, tab],
        out_specs=pl.BlockSpec((1, s, LANES), lambda bi, h: (bi, 0, h)),
        out_shape=jax.ShapeDtypeStruct((b, s, RET_WIDTH), BF16),
        scratch_shapes=[pltpu.VMEM((s, LANES), F32)] * 4,
        compiler_params=_cparams("arbitrary", "arbitrary"),
        name="retention",
    )(log_g, proj, proj, proj, proj, cos2, sin2)


def _rotary_tables(s):
    d = RET_HEAD_DIM
    inv = ROPE_BASE ** (-jnp.arange(0, d, 2, dtype=F32) / d)
    ang = jnp.arange(s, dtype=F32)[:, None] * inv[None, :]
    cos, sin = jnp.cos(ang), jnp.sin(ang)
    return jnp.concatenate([cos, cos], axis=-1), jnp.concatenate([-sin, sin], axis=-1)


def _outproj_kernel(att_ref, conv_ref, ret_ref, x_ref, w_ref, o_ref):
    acc = jnp.dot(att_ref[...], w_ref[pl.ds(0, ATT_WIDTH), :], preferred_element_type=F32)
    acc += jnp.dot(conv_ref[...], w_ref[pl.ds(ATT_WIDTH, CONV_WIDTH), :], preferred_element_type=F32)
    acc += jnp.dot(ret_ref[...], w_ref[pl.ds(ATT_WIDTH + CONV_WIDTH, RET_WIDTH), :],
                   preferred_element_type=F32)
    o_ref[...] = x_ref[...] + acc


def _outproj(att, conv, ret, x, w, *, tm=512):
    t, d = x.shape
    row = lambda width: pl.BlockSpec((tm, width), lambda i: (i, 0))
    return pl.pallas_call(
        _outproj_kernel,
        grid=(t // tm,),
        in_specs=[row(ATT_WIDTH), row(CONV_WIDTH), row(RET_WIDTH), row(d),
                  pl.BlockSpec((MIX_WIDTH, d), lambda i: (0, 0))],
        out_specs=row(d),
        out_shape=jax.ShapeDtypeStruct((t, d), F32),
        compiler_params=_cparams("arbitrary"),
        name="outproj",
    )(att, conv, ret, x, w)


def _top16(s, payload=None):
    n = s.shape[0]
    row = lax.broadcasted_iota(jnp.int32, s.shape, 0).astype(F32)
    vals, picks = [], []
    for _ in range(PEER_TOPK):
        m = jnp.max(s, axis=0, keepdims=True)
        idx = jnp.min(jnp.where(s == m, row, float(n)), axis=0, keepdims=True)
        hit = row == idx
        vals.append(m)
        if payload is None:
            picks.append(idx)
        else:
            picks.append(jnp.max(jnp.where(hit, payload, -1.0), axis=0, keepdims=True))
        s = jnp.where(hit, -jnp.inf, s)
    return vals, picks


PAIR_COUNTS = tuple(PEER_TOPK // (a + 1) for a in range(PEER_TOPK))
N_PAIR = sum(PAIR_COUNTS)
N_PAIR_PAD = -(-N_PAIR // SUBLANES) * SUBLANES


def _peer_topk_kernel(x_ref, g_ref, wq_ref, keys_ref, h_ref, eidx_ref, gate_ref, sv_s, si_s, cand_s, eid_s):
    h = _rms(x_ref[...], g_ref[...])
    h_ref[...] = h
    q = jnp.dot(h.astype(BF16), wq_ref[...], preferred_element_type=F32).astype(BF16)
    for grp in range(2 * PEER_HEADS):
        s = lax.dot_general(keys_ref[grp], q[:, grp * PEER_DHALF:(grp + 1) * PEER_DHALF],
                            (((1,), (1,)), ((), ())), preferred_element_type=F32)
        vals, idxs = _top16(s)
        for j in range(PEER_TOPK):
            sv_s[grp, pl.ds(j, 1), :] = vals[j]
            si_s[grp, pl.ds(j, 1), :] = idxs[j]
    tm = cand_s.shape[1]
    cand_s[pl.ds(N_PAIR_PAD - SUBLANES, SUBLANES), :] = jnp.full((SUBLANES, tm), -jnp.inf, F32)
    eid_s[pl.ds(N_PAIR_PAD - SUBLANES, SUBLANES), :] = jnp.full((SUBLANES, tm), -1.0, F32)
    for hd in range(PEER_HEADS):
        sv0, sv1 = sv_s[2 * hd], sv_s[2 * hd + 1]
        si0, si1 = si_s[2 * hd], si_s[2 * hd + 1]
        off = 0
        for a, nb in enumerate(PAIR_COUNTS):
            cand_s[pl.ds(off, nb), :] = sv0[a:a + 1, :] + sv1[0:nb, :]
            eid_s[pl.ds(off, nb), :] = si0[a:a + 1, :] * float(PEER_NKEYS) + si1[0:nb, :]
            off += nb
        vals, picks = _top16(cand_s[...], eid_s[...])
        cv = jnp.concatenate(vals, axis=0)
        e = jnp.exp(cv - vals[0])
        gate_ref[pl.ds(hd * PEER_TOPK, PEER_TOPK), :] = e / jnp.sum(e, axis=0, keepdims=True)
        eidx_ref[pl.ds(hd * PEER_TOPK, PEER_TOPK), :] = jnp.concatenate(picks, axis=0).astype(jnp.int32)


def _peer_topk(x, g, wq, keys, *, tm=256):
    t, d = x.shape
    return pl.pallas_call(
        _peer_topk_kernel,
        grid=(t // tm,),
        in_specs=[pl.BlockSpec((tm, d), lambda i: (i, 0)),
                  pl.BlockSpec((1, d), lambda i: (0, 0)),
                  pl.BlockSpec((d, 2 * PEER_HEADS * PEER_DHALF), lambda i: (0, 0)),
                  pl.BlockSpec((2 * PEER_HEADS, PEER_NKEYS, PEER_DHALF), lambda i: (0, 0, 0))],
        out_specs=[pl.BlockSpec((tm, d), lambda i: (i, 0)),
                   pl.BlockSpec((PEER_SEL, tm), lambda i: (0, i)),
                   pl.BlockSpec((PEER_SEL, tm), lambda i: (0, i))],
        out_shape=[jax.ShapeDtypeStruct((t, d), F32),
                   jax.ShapeDtypeStruct((PEER_SEL, t), jnp.int32),
                   jax.ShapeDtypeStruct((PEER_SEL, t), F32)],
        scratch_shapes=[pltpu.VMEM((2 * PEER_HEADS, PEER_TOPK, tm), F32),
                        pltpu.VMEM((2 * PEER_HEADS, PEER_TOPK, tm), F32),
                        pltpu.VMEM((N_PAIR_PAD, tm), F32),
                        pltpu.VMEM((N_PAIR_PAD, tm), F32)],
        compiler_params=_cparams("arbitrary"),
        name="peer_topk",
    )(x, g.reshape(1, d), wq, keys)


PEER_TB = 512
N_SLOT = 16
LOOKAHEAD = N_SLOT - 1
SLAB = D_MODEL // LANES
EXPERT_GROUP = 16
N_GROUP = PEER_SEL // EXPERT_GROUP
SEGMENT = 8
N_SEGMENT = PEER_SEL // SEGMENT
ISSUE_PER_SEGMENT = PEER_SEL // (2 * N_SEGMENT)


def _gelu_tanh(x):
    return 0.5 * x * (1.0 + jnp.tanh(math.sqrt(2.0 / math.pi) * (x + 0.044715 * (x * x * x))))


def _peer_expert_kernel(idx_hbm, tab_hbm, sel_ref, h_ref, gate_ref, o_ref,
                        idx_s, gbuf, pbuf, wsc, isem, gsem):
    i = pl.program_id(0)
    icp = pltpu.make_async_copy(idx_hbm.at[i], idx_s, isem)
    icp.start()
    icp.wait()

    def issue(t, slot, j0, n):
        row = idx_s.at[pl.ds(t * PEER_SEL, PEER_SEL)]
        for j in range(j0, j0 + n):
            pltpu.make_async_copy(tab_hbm.at[row[j]], gbuf.at[slot, j],
                                  gsem.at[slot]).start(priority=j % 2)

    def wait(slot):
        pltpu.make_async_copy(tab_hbm.at[pl.ds(0, PEER_SEL)], gbuf.at[slot], gsem.at[slot]).wait()

    lane_t = lax.broadcasted_iota(jnp.int32, (1, PEER_TB), 1)
    hi_mask = jnp.uint32(0xFFFF0000)

    def load_segment(slot, seg):
        return [gbuf[slot, seg * SEGMENT + k, half * SUBLANES:(half + 1) * SUBLANES, :]
                for k in range(SEGMENT) for half in range(2)]

    def key_side(t, slot, issue_args=None):
        x3 = h_ref[t]
        x0, x1 = x3[0:SUBLANES], x3[SUBLANES:SLAB]
        raw = load_segment(slot, 0)
        parts = []
        for seg in range(N_SEGMENT):
            nxt = load_segment(slot, seg + 1) if seg + 1 < N_SEGMENT else None
            if issue_args is not None:
                issue(*issue_args, seg * ISSUE_PER_SEGMENT, ISSUE_PER_SEGMENT)
            for k in range(SEGMENT):
                parts.append(pltpu.bitcast(raw[2 * k] << 16, F32) * x0
                             + pltpu.bitcast(raw[2 * k + 1] << 16, F32) * x1)
            if len(parts) == EXPERT_GROUP:
                grp = seg * SEGMENT // EXPERT_GROUP
                pbuf[pl.ds(grp * LANES, LANES), :] = jnp.concatenate(parts, axis=0).astype(BF16)
                parts = []
            raw = nxt

    def weight_inputs(t):
        q = jnp.dot(sel_ref[...], pbuf[...], preferred_element_type=F32)
        gate = jnp.sum(jnp.where(lane_t == t, gate_ref[...], 0.0), axis=1, keepdims=True)
        return q, gate

    def store_weights(q, gate):
        act = jnp.sum(q, axis=1, keepdims=True)
        wsc[...] = jnp.broadcast_to(gate * _gelu_tanh(act), (PEER_SEL, LANES))

    def value_side(t, slot, issue_args=None):
        acc = [jnp.zeros((SUBLANES, LANES), F32) for _ in range(4)]

        def load_weights(seg):
            return [wsc[pl.ds(seg * SEGMENT + k, 1), :] for k in range(SEGMENT)]

        raw, wts = load_segment(slot, 0), load_weights(0)
        for seg in range(N_SEGMENT):
            last = seg + 1 == N_SEGMENT
            nxt, wnxt = (None, None) if last else (load_segment(slot, seg + 1), load_weights(seg + 1))
            if issue_args is not None:
                issue(*issue_args, (N_SEGMENT + seg) * ISSUE_PER_SEGMENT, ISSUE_PER_SEGMENT)
            for k in range(SEGMENT):
                v0 = pltpu.bitcast(raw[2 * k] & hi_mask, F32)
                v1 = pltpu.bitcast(raw[2 * k + 1] & hi_mask, F32)
                acc[2 * (k % 2)] = acc[2 * (k % 2)] + wts[k] * v0
                acc[2 * (k % 2) + 1] = acc[2 * (k % 2) + 1] + wts[k] * v1
            raw, wts = nxt, wnxt
        o_ref[t] = jnp.concatenate([acc[0] + acc[2], acc[1] + acc[3]], axis=0)

    def prologue_body(n, carry):
        k = n // (PEER_SEL // SEGMENT)
        j = (n % (PEER_SEL // SEGMENT)) * SEGMENT
        for b in range(SEGMENT):
            pltpu.make_async_copy(tab_hbm.at[idx_s[k * PEER_SEL + j + b]], gbuf.at[k, j + b],
                                  gsem.at[k]).start(priority=b % 2)
        return carry

    lax.fori_loop(0, LOOKAHEAD * (PEER_SEL // SEGMENT), prologue_body, 0)
    wait(0)
    key_side(0, 0)
    store_weights(*weight_inputs(0))
    wait(1)
    key_side(1, 1)

    def token_body(t, carry):
        gather = (jnp.minimum(t + LOOKAHEAD, PEER_TB - 1), (t + LOOKAHEAD) % N_SLOT)
        wait((t + 2) % N_SLOT)
        q, gate = weight_inputs(t + 1)
        key_side(t + 2, (t + 2) % N_SLOT, gather)
        value_side(t, t % N_SLOT, gather)
        store_weights(q, gate)
        return carry

    n_here = jnp.minimum(PEER_TB, pl.num_programs(0) * PEER_TB - i * PEER_TB)
    lax.fori_loop(0, n_here - 2, token_body, 0)
    value_side(PEER_TB - 2, (PEER_TB - 2) % N_SLOT)
    store_weights(*weight_inputs(PEER_TB - 1))
    value_side(PEER_TB - 1, (PEER_TB - 1) % N_SLOT)
    for k in range(LOOKAHEAD - 2):
        wait(k)


def _peer_experts(idx, tab, h, gate_t, x, n_tokens):
    t, d = x.shape
    n_blk = n_tokens // PEER_TB
    slabs = pl.BlockSpec((PEER_TB, SLAB, LANES), lambda i: (i, 0, 0))
    sel = (jnp.arange(PEER_SEL)[:, None] == jnp.arange(PEER_SEL * SUBLANES)[None, :] // SUBLANES).astype(BF16)
    out = pl.pallas_call(
        _peer_expert_kernel,
        grid=(n_blk,),
        in_specs=[pl.BlockSpec(memory_space=pl.ANY),
                  pl.BlockSpec(memory_space=pl.ANY),
                  pl.BlockSpec((PEER_SEL, PEER_SEL * SUBLANES), lambda i: (0, 0)),
                  slabs,
                  pl.BlockSpec((PEER_SEL, PEER_TB), lambda i: (0, i))],
        out_specs=slabs,
        out_shape=jax.ShapeDtypeStruct((n_tokens, SLAB, LANES), F32),
        scratch_shapes=[pltpu.SMEM((PEER_TB * PEER_SEL,), jnp.int32),
                        pltpu.VMEM((N_SLOT, PEER_SEL, SLAB, LANES), jnp.uint32),
                        pltpu.VMEM((PEER_SEL * SUBLANES, LANES), BF16),
                        pltpu.VMEM((PEER_SEL, LANES), F32),
                        pltpu.SemaphoreType.DMA,
                        pltpu.SemaphoreType.DMA((N_SLOT,))],
        compiler_params=_cparams("arbitrary"),
        name="peer_experts",
    )(idx, tab, sel, h.reshape(t, SLAB, LANES), gate_t)
    return out.reshape(n_tokens, d)


PACK_ROWS = 8


def _pack_expert_table(u, v, layer):
    _, e, d = u.shape

    def body(u_hbm, v_hbm, o_hbm):
        def block(u_v, v_v, o_v):
            @pl.loop(0, PACK_ROWS)
            def _(r):
                @pl.loop(0, d // SC_LANES, step=SUBLANES)
                def _(c0):
                    for k in range(SUBLANES):
                        sl = pl.ds((c0 + k) * SC_LANES, SC_LANES)
                        pair = plsc.pack(u_v[0, r, sl], v_v[0, r, sl], format=plsc.PackFormat.INTERLEAVED)
                        o_v[r, sl] = plsc.bitcast(pair, jnp.uint32)

        rows = pl.BlockSpec((PACK_ROWS, d), lambda i: (i, 0))
        layer_rows = pl.BlockSpec((1, PACK_ROWS, d), lambda i: (layer, i, 0))
        pltpu.emit_pipeline(
            block,
            grid=(e // PACK_ROWS,),
            in_specs=[layer_rows, layer_rows],
            out_specs=[rows],
            core_axis_name=("core", "subcore"),
            dimension_semantics=(pltpu.PARALLEL,),
        )(u_hbm, v_hbm, o_hbm)

    return pl.kernel(
        body,
        out_type=jax.ShapeDtypeStruct((e, d), jnp.uint32),
        mesh=plsc.VectorSubcoreMesh(core_axis_name="core", subcore_axis_name="subcore"),
        scratch_types=[],
        compiler_params=pltpu.CompilerParams(needs_layout_passes=False),
        name="pack_expert_table_sc",
    )(u, v)


SC_LANES = 16
SC_CHUNKS = D_MODEL // SC_LANES
SC_EXPERTS = 16
SC_BLOCK = 32
SC_PARTIALS = 4
PEER_SC_TOKENS = 2560
PEER_SC_TOKENS_LAST = 2048


def _gelu_tanh_exp(x):
    z = math.sqrt(2.0 / math.pi) * (x + 0.044715 * (x * x * x))
    return 0.5 * x * (2.0 - 2.0 / (jnp.exp(2.0 * z) + 1.0))


def _peer_experts_sc(idx, tab, h, gate, x, first, n):
    d = x.shape[1]
    info = plsc.get_sparse_core_info()
    n_core, n_worker = info.num_cores, info.num_cores * info.num_subcores
    per_worker = n // n_worker
    n_chunk = PEER_SEL // SC_EXPERTS
    hi_mask = jnp.uint32(0xFFFF0000)

    def body(idx_hbm, tab_hbm, h_hbm, gate_hbm, x_hbm, o_hbm, idx_v, gate_v, h_v, out_v, rows_v, w_v, sem):
        worker = lax.axis_index("subcore") * n_core + lax.axis_index("core")

        def gather(cb):
            return pltpu.make_async_copy(tab_hbm.at[idx_v.at[pl.ds(cb * SC_EXPERTS, SC_EXPERTS)]],
                                         rows_v.at[cb % 2], sem.at[cb % 2])

        def mix_chunk(cb):
            rows = rows_v.at[cb % 2]
            for e in range(SC_EXPERTS):
                w_v[e, :] = jnp.zeros((SC_LANES,), F32)

            def key_block(blk, carry):
                base = blk * (SC_BLOCK * SC_LANES)
                hs = [h_v[pl.ds(base + j * SC_LANES, SC_LANES)] for j in range(SC_BLOCK)]
                for e in range(SC_EXPERTS):
                    part = [w_v[e, :]] + [None] * (SC_PARTIALS - 1)
                    for j in range(SC_BLOCK):
                        word = rows[e, pl.ds(base + j * SC_LANES, SC_LANES)]
                        term = plsc.bitcast(word << 16, F32) * hs[j]
                        k = j % SC_PARTIALS
                        part[k] = term if part[k] is None else part[k] + term
                    w_v[e, :] = functools.reduce(lambda a, b: a + b, part)
                return carry

            lax.fori_loop(0, SC_CHUNKS // SC_BLOCK, key_block, 0)
            for e in range(SC_EXPERTS):
                act = jnp.full((SC_LANES,), jnp.sum(w_v[e, :]), F32)
                g = plsc.load_gather(gate_v, [jnp.full((SC_LANES,), cb * SC_EXPERTS + e, jnp.int32)])
                w_v[e, :] = g * _gelu_tanh_exp(act)

            def value_block(blk, carry):
                base = blk * (SC_BLOCK * SC_LANES)
                outs = [out_v[pl.ds(base + j * SC_LANES, SC_LANES)] for j in range(SC_BLOCK)]
                for e in range(SC_EXPERTS):
                    w = w_v[e, :]
                    for j in range(SC_BLOCK):
                        word = rows[e, pl.ds(base + j * SC_LANES, SC_LANES)]
                        outs[j] = outs[j] + w * plsc.bitcast(word & hi_mask, F32)
                for j in range(SC_BLOCK):
                    out_v[pl.ds(base + j * SC_LANES, SC_LANES)] = outs[j]
                return carry

            lax.fori_loop(0, SC_CHUNKS // SC_BLOCK, value_block, 0)

        @pl.loop(0, per_worker)
        def _(i):
            row = worker * per_worker + i
            t = first + row
            pltpu.sync_copy(idx_hbm.at[t], idx_v)
            gather(0).start()
            pltpu.sync_copy(gate_hbm.at[t], gate_v)
            pltpu.sync_copy(h_hbm.at[t], h_v)
            @pl.loop(0, SC_CHUNKS, step=SUBLANES)
            def _(c0):
                for k in range(SUBLANES):
                    out_v[pl.ds((c0 + k) * SC_LANES, SC_LANES)] = jnp.zeros((SC_LANES,), F32)

            @pl.loop(0, n_chunk)
            def _(cb):
                @pl.when(cb + 1 < n_chunk)
                def _():
                    gather(cb + 1).start()

                gather(cb).wait()
                mix_chunk(cb)

            pltpu.sync_copy(out_v, o_hbm.at[row])

    return pl.kernel(
        body,
        out_type=jax.ShapeDtypeStruct((n, d), F32),
        mesh=plsc.VectorSubcoreMesh(core_axis_name="core", subcore_axis_name="subcore"),
        scratch_types=[pltpu.VMEM((PEER_SEL,), jnp.int32),
                       pltpu.VMEM((PEER_SEL,), F32),
                       pltpu.VMEM((d,), F32),
                       pltpu.VMEM((d,), F32),
                       pltpu.VMEM((2, SC_EXPERTS, d), jnp.uint32),
                       pltpu.VMEM((SC_EXPERTS, SC_LANES), F32),
                       pltpu.SemaphoreType.DMA((2,))],
        compiler_params=pltpu.CompilerParams(needs_layout_passes=False),
        name="peer_experts_sc",
    )(idx, tab, h, gate, x)


def _final_norm_kernel(x_ref, g_ref, o_ref):
    o_ref[...] = _rms(x_ref[...], g_ref[...])


def _final_norm(x, g, *, tm=512):
    t, d = x.shape
    row = pl.BlockSpec((tm, d), lambda i: (i, 0))
    return pl.pallas_call(
        _final_norm_kernel,
        grid=(t // tm,),
        in_specs=[row, pl.BlockSpec((1, d), lambda i: (0, 0))],
        out_specs=row,
        out_shape=jax.ShapeDtypeStruct((t, d), F32),
        compiler_params=_cparams("arbitrary"),
        name="final_norm",
    )(x, g.reshape(1, d))


def _mixer_and_retrieval(x, shape, p, band, rot):
    b, s, _ = shape
    t = b * s
    proj = _inproj(x, p["norm_mix"], p["w_in"]).reshape(b, s, IN_COLS)
    att = _attention(proj, band)
    conv = _short_conv(proj, p["conv_w"])
    ret = _retention(proj, p["log_g"], *rot)
    x = _outproj(att.reshape(t, ATT_WIDTH), conv.reshape(t, CONV_WIDTH),
                 ret.reshape(t, RET_WIDTH), x, p["w_out"])
    return (x,) + tuple(_peer_topk(x, p["norm_ffn"], p["peer_wq"], p["peer_keys"]))


def _expert_mix(x, h, eidx_t, gate_t, p, n_sc):
    t = x.shape[0]
    t_tc = t - n_sc
    idx = jnp.transpose(eidx_t)
    x_tc = _peer_experts(idx.reshape(t // PEER_TB, PEER_TB * PEER_SEL),
                         p["peer_tab"].reshape(PEER_EXPERTS, SLAB, LANES), h, gate_t, x, t_tc)
    x_sc = _peer_experts_sc(idx, p["peer_tab"], h, jnp.transpose(gate_t), x, t_tc, n_sc)
    return x + jnp.concatenate([x_tc, x_sc], axis=0)


def _trunks(xs, params):
    shapes = [x.shape for x in xs]
    rots = [_rotary_tables(shape[1]) for shape in shapes]
    xs = [x.reshape(-1, D_MODEL) for x in xs]
    for l in range(DEPTH):
        p = params[l]
        staged = [_mixer_and_retrieval(x, shape, p, params["band"], rot)
                  for x, shape, rot in zip(xs, shapes, rots)]
        p = dict(p, peer_tab=_pack_expert_table(*p["peer_uv"]))
        shares = [PEER_SC_TOKENS_LAST if (l, i) == (DEPTH - 1, len(xs) - 1) else PEER_SC_TOKENS
                  for i in range(len(xs))]
        xs = [_expert_mix(*st, p, n_sc) for st, n_sc in zip(staged, shares)]
    return tuple(_final_norm(x, params["final_norm"]).reshape(shape) for x, shape in zip(xs, shapes))


def kernel(x_prompt, x_sample, rel_bias, final_norm, norm_mix, norm_ffn, w_in, conv_w, ret_decay,
           w_out, peer_wq, peer_keys, peer_u, peer_v):
    params = {"band": _attention_band(rel_bias), "final_norm": final_norm}
    for l in range(DEPTH):
        params[l] = {
            "norm_mix": norm_mix[l],
            "norm_ffn": norm_ffn[l],
            "w_in": w_in[l].astype(BF16),
            "conv_w": conv_w[l],
            "log_g": jax.nn.log_sigmoid(ret_decay[l].astype(F32)),
            "w_out": w_out[l].astype(BF16),
            "peer_wq": peer_wq[l].astype(BF16),
            "peer_keys": peer_keys[l].reshape(2 * PEER_HEADS, PEER_NKEYS, PEER_DHALF).astype(BF16),
            "peer_uv": (peer_u, peer_v, l),
        }
    return _trunks([x_prompt, x_sample], params)
```
